```python
import jax, jax.numpy as jnp
from jax import lax
import numpy as np


D_MODEL = 1024
BATCH = 4
SEQ = 4096
DEPTH = 4

CTX_LEN = 256
GRID_W = 64
N_Q_HEADS = 8
N_KV_HEADS = 2
HEAD_DIM = 64
WINDOW = 128
ATT_BLOCK = 128
ROPE_BASE = 10000.0
ATT_Q = N_Q_HEADS * HEAD_DIM
ATT_KV = N_KV_HEADS * HEAD_DIM
CONV_DIM = 512
CONV_WIDTH = 31
M_HEADS = 4
M_HEAD_DIM = 128
M_WIDTH = M_HEADS * M_HEAD_DIM
M_CHUNK = 64
M_SHORT_CONV = 3
N_GATE_COLS = 4 * M_HEADS
N_BRANCHES = 3
D_FF = -(-8 * D_MODEL // (3 * 256)) * 256

SPLIT_SIZES = (ATT_Q, ATT_KV, ATT_KV, 2 * CONV_DIM, 2 * M_WIDTH, M_WIDTH, M_WIDTH, N_GATE_COLS, N_BRANCHES * D_MODEL)
SPLIT_POINTS = tuple(sum(SPLIT_SIZES[:i + 1]) for i in range(len(SPLIT_SIZES) - 1))
D_IN = sum(SPLIT_SIZES)

EPS = 1e-6
NEG_INF = -1e30

kernel_name = 'hybrid_gated_swa_conformer_mlstm_dit'


def rms_norm(x, g):
    xf = x.astype(jnp.float32)
    y = xf * lax.rsqrt(jnp.mean(xf * xf, axis=-1, keepdims=True) + EPS)
    return (y * g.astype(jnp.float32)).astype(x.dtype)


def layer_norm(x, g, b):
    xf = x.astype(jnp.float32)
    mu = jnp.mean(xf, axis=-1, keepdims=True)
    var = jnp.mean(jnp.square(xf - mu), axis=-1, keepdims=True)
    y = (xf - mu) * lax.rsqrt(var + EPS)
    return (y * g.astype(jnp.float32) + b.astype(jnp.float32)).astype(x.dtype)


def modulate(h, shift, scale):
    return h * (1.0 + scale) + shift


def to_heads(a, n):
    B, T, _ = a.shape
    return a.reshape(B, T, n, -1).transpose(0, 2, 1, 3)


def from_heads(a):
    B, n, T, d = a.shape
    return a.transpose(0, 2, 1, 3).reshape(B, T, n * d)


def rope_tables(n_tokens):
    rows = n_tokens // GRID_W
    row = jnp.repeat(jnp.arange(rows, dtype=jnp.float32), GRID_W)
    col = jnp.tile(jnp.arange(GRID_W, dtype=jnp.float32), rows)
    n_freq = HEAD_DIM // 4
    inv_freq = ROPE_BASE ** (-jnp.arange(n_freq, dtype=jnp.float32) / n_freq)
    ang_r = row[:, None] * inv_freq
    ang_c = col[:, None] * inv_freq
    ang = jnp.concatenate([ang_r, ang_r, ang_c, ang_c], axis=-1)
    return jnp.cos(ang), jnp.sin(ang)


def rotate_half(h):
    a, b = jnp.split(h, 2, axis=-1)
    return jnp.concatenate([-b, a], axis=-1)


def apply_rope(x, cos, sin):
    half = HEAD_DIM // 2
    xr = jnp.concatenate([rotate_half(x[..., :half]), rotate_half(x[..., half:])], axis=-1)
    return x * cos.astype(x.dtype) + xr * sin.astype(x.dtype)


def depthwise_conv(x, w):
    pad = w.shape[0] // 2
    return lax.conv_general_dilated(
        x, w[:, None, :].astype(x.dtype), window_strides=(1,), padding=[(pad, pad)],
        dimension_numbers=('NWC', 'WIO', 'NWC'), feature_group_count=x.shape[-1])


def window_attention(q, k, v, k_ctx, v_ctx, sink):
    B, _, S, d = q.shape
    nb = S // ATT_BLOCK
    G = N_Q_HEADS // N_KV_HEADS
    qb = (q * d ** -0.5).reshape(B, N_KV_HEADS, G, nb, ATT_BLOCK, d)

    def band(a):
        ap = jnp.pad(a, ((0, 0), (0, 0), (ATT_BLOCK, ATT_BLOCK), (0, 0))).reshape(B, N_KV_HEADS, nb + 2, ATT_BLOCK, d)
        return jnp.concatenate([ap[:, :, :-2], ap[:, :, 1:-1], ap[:, :, 2:]], axis=3)

    kb, vb = band(k), band(v)
    s_loc = jnp.einsum('bhgnqd,bhnkd->bhgnqk', qb, kb).astype(jnp.float32)
    s_ctx = jnp.einsum('bhgnqd,bhkd->bhgnqk', qb, k_ctx).astype(jnp.float32)
    blk = jnp.arange(nb)[:, None, None]
    q_pos = blk * ATT_BLOCK + jnp.arange(ATT_BLOCK)[None, :, None]
    key_pos = (blk - 1) * ATT_BLOCK + jnp.arange(3 * ATT_BLOCK)[None, None, :]
    valid = (jnp.abs(key_pos - q_pos) <= WINDOW) & (key_pos >= 0) & (key_pos < S)
    s_loc = jnp.where(valid, s_loc, NEG_INF)
    sk = jnp.broadcast_to(sink.astype(jnp.float32).reshape(1, N_KV_HEADS, G, 1, 1, 1), s_loc.shape[:-1] + (1,))
    p = jax.nn.softmax(jnp.concatenate([s_loc, s_ctx, sk], axis=-1), axis=-1).astype(v.dtype)
    n_loc = 3 * ATT_BLOCK
    n_ctx = k_ctx.shape[2]
    o = (jnp.einsum('bhgnqk,bhnkd->bhgnqd', p[..., :n_loc], vb)
         + jnp.einsum('bhgnqk,bhkd->bhgnqd', p[..., n_loc:n_loc + n_ctx], v_ctx))
    return o.reshape(B, N_Q_HEADS, S, d)


def context_attention(q, k, v, sink):
    B, _, L, d = q.shape
    G = N_Q_HEADS // N_KV_HEADS
    qg = (q * d ** -0.5).reshape(B, N_KV_HEADS, G, L, d)
    s = jnp.einsum('bhgqd,bhkd->bhgqk', qg, k).astype(jnp.float32)
    sk = jnp.broadcast_to(sink.astype(jnp.float32).reshape(1, N_KV_HEADS, G, 1, 1), s.shape[:-1] + (1,))
    p = jax.nn.softmax(jnp.concatenate([s, sk], axis=-1), axis=-1)[..., :L].astype(v.dtype)
    o = jnp.einsum('bhgqk,bhkd->bhgqd', p, v)
    return o.reshape(B, N_Q_HEADS, L, d)


def conformer_conv(u, w_dw, b_dw, g_ln, b_ln, w_pw):
    val, gate = jnp.split(u, 2, axis=-1)
    y = depthwise_conv(val * jax.nn.sigmoid(gate), w_dw) + b_dw
    y = jax.nn.silu(layer_norm(y, g_ln, b_ln))
    return y @ w_pw


def mlstm_chunkwise(q, k, v, log_i, log_f, state):
    B, H, T, dk = q.shape
    dv = v.shape[-1]
    L = M_CHUNK
    nc = T // L

    def chunks(a):
        return jnp.moveaxis(a.reshape(a.shape[:2] + (nc, L) + a.shape[3:]), 2, 0)

    tril = jnp.tril(jnp.ones((L, L), dtype=bool))

    def step(carry, xs):
        C, n, m = carry
        qc, kc, vc, ic, fc = xs
        b = jnp.cumsum(fc, axis=-1)
        a = b + m[..., None]
        dmat = jnp.where(tril, b[..., :, None] - b[..., None, :] + ic[..., None, :], NEG_INF)
        mt = jnp.maximum(a, dmat.max(axis=-1))
        w_inter = jnp.exp(a - mt)
        s = jnp.einsum('bhtd,bhsd->bhts', qc, kc) * jnp.exp(dmat - mt[..., None])
        num = w_inter[..., None] * jnp.einsum('bhtd,bhde->bhte', qc, C) + jnp.einsum('bhts,bhse->bhte', s, vc)
        den = w_inter * jnp.einsum('bhtd,bhd->bht', qc, n) + s.sum(axis=-1)
        h = num / jnp.maximum(jnp.abs(den), jnp.exp(-mt))[..., None]
        b_end = b[..., -1]
        g = b_end[..., None] - b + ic
        m_new = jnp.maximum(b_end + m, g.max(axis=-1))
        decay = jnp.exp(b_end + m - m_new)
        wk = jnp.exp(g - m_new[..., None])
        C_new = decay[..., None, None] * C + jnp.einsum('bhs,bhsd,bhse->bhde', wk, kc, vc)
        n_new = decay[..., None] * n + jnp.einsum('bhs,bhsd->bhd', wk, kc)
        return (C_new, n_new, m_new), h

    state, h = lax.scan(step, state, (chunks(q), chunks(k), chunks(v), chunks(log_i), chunks(log_f)))
    h = jnp.moveaxis(h, 0, 2).reshape(B, H, T, dv)
    return h, state


def mlstm_branch(qk, v, o, gates, qk_c, v_c, o_c, gates_c, b_mgate, w_mconv, g_mnorm, w_mout, with_ctx_out):
    def prep(qk, v, gates):
        q, k = jnp.split(jax.nn.silu(depthwise_conv(qk, w_mconv)), 2, axis=-1)
        q = to_heads(q, M_HEADS).astype(jnp.float32)
        k = to_heads(k, M_HEADS).astype(jnp.float32) * M_HEAD_DIM ** -0.5
        v = to_heads(v, M_HEADS).astype(jnp.float32)
        g = jnp.moveaxis((gates + b_mgate).astype(jnp.float32), -1, 1)
        i_fw, i_bw, f_fw, f_bw = jnp.split(g, 4, axis=1)
        fwd = (q, k, v, i_fw, jax.nn.log_sigmoid(f_fw))
        bwd = tuple(jnp.flip(a, axis=2) for a in (q, k, v, i_bw, jax.nn.log_sigmoid(f_bw)))
        return fwd, bwd

    lat_f, lat_b = prep(qk, v, gates)
    ctx_f, ctx_b = prep(qk_c, v_c, gates_c)
    B = qk.shape[0]
    init = (jnp.zeros((B, M_HEADS, M_HEAD_DIM, M_HEAD_DIM), jnp.float32),
            jnp.zeros((B, M_HEADS, M_HEAD_DIM), jnp.float32),
            jnp.full((B, M_HEADS), NEG_INF, jnp.float32))
    hc_f, st_f = mlstm_chunkwise(*ctx_f, init)
    hc_b, st_b = mlstm_chunkwise(*ctx_b, init)
    hl_f, _ = mlstm_chunkwise(*lat_f, st_f)
    hl_b, _ = mlstm_chunkwise(*lat_b, st_b)

    def readout(h_f, h_b_rev, o):
        h = jax.nn.sigmoid(to_heads(o, M_HEADS).astype(jnp.float32)) * (h_f + jnp.flip(h_b_rev, axis=2))
        mu = jnp.mean(h, axis=-1, keepdims=True)
        var = jnp.mean(jnp.square(h - mu), axis=-1, keepdims=True)
        h = from_heads((h - mu) * lax.rsqrt(var + EPS)) * g_mnorm.astype(jnp.float32)
        return h.astype(o.dtype) @ w_mout

    y = readout(hl_f, hl_b, o)
    y_ctx = readout(hc_f, hc_b, o_c) if with_ctx_out else None
    return y, y_ctx


def gated_merge(gate_logits, ya, yb, yc):
    ga, gb, gc = jnp.split(jax.nn.sigmoid(gate_logits), N_BRANCHES, axis=-1)
    return ga * ya + gb * yb + gc * yc


def hybrid_mixer(h, hc, w_in, b_mgate, sink, w_att_out, w_dw, b_dw, g_cln, b_cln, w_pw,
                 w_mconv, g_mnorm, w_mout, w_out, cos, sin, with_ctx_out):
    qa, ka, va, glu, qk_m, v_m, o_m, gate_m, br = jnp.split(h @ w_in, SPLIT_POINTS, axis=-1)
    qac, kac, vac, gluc, qk_mc, v_mc, o_mc, gate_mc, brc = jnp.split(hc @ w_in, SPLIT_POINTS, axis=-1)
    k_ctx = to_heads(kac, N_KV_HEADS)
    v_ctx = to_heads(vac, N_KV_HEADS)
    q = apply_rope(to_heads(qa, N_Q_HEADS), cos, sin)
    k = apply_rope(to_heads(ka, N_KV_HEADS), cos, sin)
    ya = from_heads(window_attention(q, k, to_heads(va, N_KV_HEADS), k_ctx, v_ctx, sink)) @ w_att_out
    yb = conformer_conv(glu, w_dw, b_dw, g_cln, b_cln, w_pw)
    yc, yc_ctx = mlstm_branch(qk_m, v_m, o_m, gate_m, qk_mc, v_mc, o_mc, gate_mc,
                              b_mgate, w_mconv, g_mnorm, w_mout, with_ctx_out)
    y = gated_merge(br, ya, yb, yc) @ w_out
    if not with_ctx_out:
        return y, None
    ya_c = from_heads(context_attention(to_heads(qac, N_Q_HEADS), k_ctx, v_ctx, sink)) @ w_att_out
    yb_c = conformer_conv(gluc, w_dw, b_dw, g_cln, b_cln, w_pw)
    y_ctx = gated_merge(brc, ya_c, yb_c, yc_ctx) @ w_out
    return y, y_ctx


def swiglu(h, w_gate, w_up, w_down):
    return (jax.nn.silu(h @ w_gate) * (h @ w_up)) @ w_down


def setup_inputs(seed: int = 0) -> dict:
    key = jax.random.key(seed)
    ks = jax.random.split(key, 27)
    f32 = jnp.float32

    def nrm(k, shape):
        return jax.random.normal(k, shape, f32)

    def w(k, shape, fan_in):
        return nrm(k, shape) * fan_in ** -0.5

    def gain(k, shape):
        return 1.0 + 0.05 * nrm(k, shape)

    def small(k, shape):
        return 0.02 * nrm(k, shape)

    b_mgate = jnp.concatenate([0.1 * nrm(ks[9], (DEPTH, 2 * M_HEADS)),
                               3.0 + 0.5 * nrm(ks[25], (DEPTH, 2 * M_HEADS))], axis=-1)
    return {
        'x': nrm(ks[0], (BATCH, SEQ, D_MODEL)),
        'c': nrm(ks[1], (BATCH, D_MODEL)),
        'ctx': nrm(ks[2], (BATCH, CTX_LEN, D_MODEL)),
        'c_ctx': nrm(ks[3], (D_MODEL,)),
        'w_ada': 0.5 * w(ks[4], (DEPTH, D_MODEL, 6 * D_MODEL), D_MODEL),
        'b_ada': small(ks[5], (DEPTH, 6 * D_MODEL)),
        'g_norm_mix': gain(ks[6], (DEPTH, D_MODEL)),
        'g_norm_ffn': gain(ks[7], (DEPTH, D_MODEL)),
        'w_in': w(ks[8], (DEPTH, D_MODEL, D_IN), D_MODEL),
        'b_mgate': b_mgate,
        'att_sink': 0.5 * nrm(ks[10], (DEPTH, N_Q_HEADS)),
        'w_att_out': w(ks[11], (DEPTH, ATT_Q, D_MODEL), ATT_Q),
        'w_conv_dw': w(ks[12], (DEPTH, CONV_WIDTH, CONV_DIM), CONV_WIDTH),
        'b_conv_dw': small(ks[13], (DEPTH, CONV_DIM)),
        'g_conv_ln': gain(ks[14], (DEPTH, CONV_DIM)),
        'b_conv_ln': small(ks[15], (DEPTH, CONV_DIM)),
        'w_conv_pw': w(ks[16], (DEPTH, CONV_DIM, D_MODEL), CONV_DIM),
        'w_mconv': w(ks[17], (DEPTH, M_SHORT_CONV, 2 * M_WIDTH), M_SHORT_CONV),
        'g_mlstm_norm': gain(ks[18], (DEPTH, M_WIDTH)),
        'w_mlstm_out': w(ks[19], (DEPTH, M_WIDTH, D_MODEL), M_WIDTH),
        'w_out': w(ks[20], (DEPTH, D_MODEL, D_MODEL), D_MODEL),
        'w_ff_gate': w(ks[21], (DEPTH, D_MODEL, D_FF), D_MODEL),
        'w_ff_up': w(ks[22], (DEPTH, D_MODEL, D_FF), D_MODEL),
        'w_ff_down': w(ks[23], (DEPTH, D_FF, D_MODEL), D_FF),
        'g_final': gain(ks[24], (D_MODEL,)),
    }


def reference(x, c, ctx, c_ctx, w_ada, b_ada, g_norm_mix, g_norm_ffn, w_in, b_mgate, att_sink,
              w_att_out, w_conv_dw, b_conv_dw, g_conv_ln, b_conv_ln, w_conv_pw, w_mconv,
              g_mlstm_norm, w_mlstm_out, w_out, w_ff_gate, w_ff_up, w_ff_down, g_final):
    cos, sin = rope_tables(x.shape[1])
    for l in range(DEPTH):
        with_ctx_out = l < DEPTH - 1
        mod = jax.nn.silu(c) @ w_ada[l] + b_ada[l]
        sh1, sc1, gt1, sh2, sc2, gt2 = jnp.split(mod[:, None, :], 6, axis=-1)
        mod_c = jax.nn.silu(c_ctx) @ w_ada[l] + b_ada[l]
        csh1, csc1, cgt1, csh2, csc2, cgt2 = jnp.split(mod_c, 6, axis=-1)
        h = modulate(rms_norm(x, g_norm_mix[l]), sh1, sc1)
        hc = modulate(rms_norm(ctx, g_norm_mix[l]), csh1, csc1)
        y, y_ctx = hybrid_mixer(h, hc, w_in[l], b_mgate[l], att_sink[l], w_att_out[l],
                                w_conv_dw[l], b_conv_dw[l], g_conv_ln[l], b_conv_ln[l], w_conv_pw[l],
                                w_mconv[l], g_mlstm_norm[l], w_mlstm_out[l], w_out[l],
                                cos, sin, with_ctx_out)
        x = x + gt1 * y
        x = x + gt2 * swiglu(modulate(rms_norm(x, g_norm_ffn[l]), sh2, sc2), w_ff_gate[l], w_ff_up[l], w_ff_down[l])
        if with_ctx_out:
            ctx = ctx + cgt1 * y_ctx
            ctx = ctx + cgt2 * swiglu(modulate(rms_norm(ctx, g_norm_ffn[l]), csh2, csc2), w_ff_gate[l], w_ff_up[l], w_ff_down[l])
    return rms_norm(x, g_final)
```

```python
import functools

import jax
import jax.numpy as jnp
from jax import lax
from jax.experimental import pallas as pl
from jax.experimental.pallas import tpu as pltpu

F32 = jnp.float32
BF16 = jnp.bfloat16

GRID_W = 64
N_Q_HEADS = 8
N_KV_HEADS = 2
HEAD_DIM = 64
WINDOW = 128
ATT_BLOCK = 128
ROPE_BASE = 10000.0
ATT_Q = N_Q_HEADS * HEAD_DIM
ATT_KV = N_KV_HEADS * HEAD_DIM
CONV_DIM = 512
CONV_WIDTH = 31
M_HEADS = 4
M_HEAD_DIM = 128
M_WIDTH = M_HEADS * M_HEAD_DIM
M_SHORT_CONV = 3
N_GATE_COLS = 4 * M_HEADS
EPS = 1e-6
NEG_INF = -1e30

LANES = 128
SUBLANES = 8
M_CHUNK = 128
CONV_HALO = 16
VMEM_LIMIT = 52 * 1024 * 1024


def _cparams(*sem):
    return pltpu.CompilerParams(dimension_semantics=sem, vmem_limit_bytes=VMEM_LIMIT)


def _resident(shape):
    nd = len(shape)
    return pl.BlockSpec(shape, lambda *_: (0,) * nd, pipeline_mode=pl.Buffered(1))


def _sigmoid(x):
    return 1.0 / (1.0 + jnp.exp(-x))


def _silu(x):
    return x * _sigmoid(x)


def _log_sigmoid(x):
    return jnp.minimum(x, 0.0) - jnp.log(1.0 + jnp.exp(-jnp.abs(x)))


def _norm_mod(x, g, sc, sh):
    ms = jnp.mean(x * x, axis=-1, keepdims=True)
    return (x * lax.rsqrt(ms + EPS) * g) * (1.0 + sc) + sh


def _dot(a, b):
    return jnp.dot(a, b, preferred_element_type=F32)


def _mod_body(c_ref, w_ref, b_ref, o_ref):
    h = _silu(c_ref[...]).astype(BF16)
    o_ref[0] = _dot(h, w_ref[0].astype(BF16)) + b_ref[0]


def _modulation(cvec, w_ada, b_ada):
    depth, d, n6 = w_ada.shape
    rows = cvec.shape[0]
    tn = n6 // 4
    return pl.pallas_call(
        _mod_body,
        grid=(depth, n6 // tn),
        in_specs=[pl.BlockSpec((rows, d), lambda l, j: (0, 0)),
                  pl.BlockSpec((1, d, tn), lambda l, j: (l, 0, j)),
                  pl.BlockSpec((1, 1, tn), lambda l, j: (l, 0, j))],
        out_specs=pl.BlockSpec((1, rows, tn), lambda l, j: (l, 0, j)),
        out_shape=jax.ShapeDtypeStruct((depth, rows, n6), F32),
        compiler_params=_cparams("arbitrary", "arbitrary"),
        name="ada_mod",
    )(cvec, w_ada, b_ada.reshape(depth, 1, n6))


class _Geom:
    def __init__(self, batch, seq, ctx_len):
        self.B, self.S, self.Lc = batch, seq, ctx_len
        self.n_ctx = batch * ctx_len
        self.N = self.n_ctx + batch * seq

    def mod_row(self, tile, tm):
        n_ctx_tiles = self.n_ctx // tm
        return jnp.where(tile < n_ctx_tiles, self.B, (tile - n_ctx_tiles) // (self.S // tm))

    def seq_edges(self, tile, tm):
        n_ctx_tiles = self.n_ctx // tm
        tc, ts = self.Lc // tm, self.S // tm
        pos = jnp.where(tile < n_ctx_tiles, tile % tc, (tile - n_ctx_tiles) % ts)
        last = jnp.where(tile < n_ctx_tiles, tc - 1, ts - 1)
        return pos == 0, pos == last


def _row_tile(geom, want):
    tm = want
    while geom.n_ctx % tm or geom.S % tm:
        tm //= 2
    return tm


def _rope(x, cos, sin_a, sin_b):
    return x * cos + pltpu.roll(x, LANES - 16, 1) * sin_a + pltpu.roll(x, 16, 1) * sin_b


def _proj_body(x_ref, mod_ref, g_ref, cos_ref, sa_ref, sb_ref, bg_ref,
               wqkv_ref, wglu_ref, wqkm_ref, wvm_ref, wom_ref, wgate_ref, wbr_ref,
               q_ref, k_ref, v_ref, u_ref, qkm_ref, vm_ref, om_ref, gate_ref, br_ref):
    d = x_ref.shape[1]
    mod = mod_ref[0]
    h = _norm_mod(x_ref[...], g_ref[...], mod[:, d:2 * d], mod[:, 0:d]).astype(BF16)
    cos, sa, sb = cos_ref[...], sa_ref[...], sb_ref[...]

    qkv = _dot(h, wqkv_ref[...])
    scale = HEAD_DIM ** -0.5
    for j in range(ATT_Q // LANES):
        sl = slice(j * LANES, (j + 1) * LANES)
        q_ref[:, sl] = _rope(qkv[:, sl] * scale, cos, sa, sb).astype(BF16)
    k_ref[...] = _rope(qkv[:, ATT_Q:ATT_Q + ATT_KV], cos, sa, sb).astype(BF16)
    v_ref[...] = qkv[:, ATT_Q + ATT_KV:].astype(BF16)

    glu = _dot(h, wglu_ref[...])
    u_ref[...] = glu[:, :CONV_DIM] * _sigmoid(glu[:, CONV_DIM:])
    qkm_ref[...] = _dot(h, wqkm_ref[...])
    vm_ref[...] = _dot(h, wvm_ref[...]).astype(BF16)
    om_ref[...] = _dot(h, wom_ref[...])
    gate_ref[...] = _dot(h, wgate_ref[...]) + bg_ref[...]
    br_ref[...] = _dot(h, wbr_ref[...])


def _projection(geom, x, mod3, g, rope, bgate, w):
    n, d = x.shape
    tm = _row_tile(geom, 256)
    row = lambda i: (i, 0)
    mod_map = lambda i: (geom.mod_row(i, tm), 0, 0)
    widths = (ATT_Q, ATT_KV, ATT_KV, CONV_DIM, 2 * M_WIDTH, M_WIDTH, M_WIDTH, LANES, 3 * d)
    dtypes = (BF16, BF16, BF16, F32, F32, BF16, F32, F32, F32)
    weights = (w["qkv"], w["glu"], w["qkm"], w["vm"], w["om"], w["gate"], w["br"])
    return pl.pallas_call(
        _proj_body,
        grid=(n // tm,),
        in_specs=[pl.BlockSpec((tm, d), row),
                  pl.BlockSpec((1, 1, mod3.shape[2]), mod_map),
                  _resident((1, d)),
                  pl.BlockSpec((tm, LANES), row), pl.BlockSpec((tm, LANES), row), pl.BlockSpec((tm, LANES), row),
                  _resident((1, LANES))]
                 + [_resident(a.shape) for a in weights],
        out_specs=[pl.BlockSpec((tm, c), row) for c in widths],
        out_shape=[jax.ShapeDtypeStruct((n, c), t) for c, t in zip(widths, dtypes)],
        compiler_params=_cparams("parallel"),
        name="in_proj",
    )(x, mod3, g, *rope, bgate, *weights)


def _att_body(geom, sink_ref, q_ref, kp_ref, kc_ref, kn_ref, vp_ref, vc_ref, vn_ref, kx_ref, vx_ref, o_ref):
    blk = ATT_BLOCK
    i = pl.program_id(1)
    ncb = geom.Lc // blk
    n = i - ncb
    r = lax.broadcasted_iota(jnp.int32, (blk, 3 * blk), 0)
    c = lax.broadcasted_iota(jnp.int32, (blk, 3 * blk), 1)
    key_pos = (n - 1) * blk + c
    q_pos = n * blk + r
    valid = (jnp.abs(key_pos - q_pos) <= WINDOW) & (key_pos >= 0) & (key_pos < geom.S) & (n >= 0)
    group = N_Q_HEADS // N_KV_HEADS
    valid = jnp.concatenate([valid] * group, axis=0)

    q = q_ref[...]
    k_loc = jnp.concatenate([kp_ref[...], kc_ref[...], kn_ref[...]], axis=0)
    v_loc = jnp.concatenate([vp_ref[...], vc_ref[...], vn_ref[...]], axis=0)
    k_ctx, v_ctx = kx_ref[...], vx_ref[...]
    nt = (((1,), (1,)), ((), ()))
    for hk in range(N_KV_HEADS):
        ks = slice(hk * HEAD_DIM, (hk + 1) * HEAD_DIM)
        qg = jnp.concatenate(
            [q[:, (hk * group + j) * HEAD_DIM:(hk * group + j + 1) * HEAD_DIM] for j in range(group)], axis=0)
        s_loc = lax.dot_general(qg, k_loc[:, ks], nt, preferred_element_type=F32)
        s_ctx = lax.dot_general(qg, k_ctx[:, ks], nt, preferred_element_type=F32)
        s_loc = jnp.where(valid, s_loc, NEG_INF)
        sk = jnp.concatenate(
            [jnp.full((blk, 1), sink_ref[hk * group + j], F32) for j in range(group)], axis=0)
        m = jnp.maximum(jnp.maximum(jnp.max(s_loc, axis=-1, keepdims=True),
                                    jnp.max(s_ctx, axis=-1, keepdims=True)), sk)
        e_loc = jnp.exp(s_loc - m)
        e_ctx = jnp.exp(s_ctx - m)
        den = jnp.sum(e_loc, axis=-1, keepdims=True) + jnp.sum(e_ctx, axis=-1, keepdims=True) + jnp.exp(sk - m)
        inv = 1.0 / den
        o = (_dot((e_loc * inv).astype(BF16), v_loc[:, ks]) + _dot((e_ctx * inv).astype(BF16), v_ctx[:, ks]))
        for j in range(group):
            hq = hk * group + j
            o_ref[:, hq * HEAD_DIM:(hq + 1) * HEAD_DIM] = o[j * blk:(j + 1) * blk].astype(BF16)


def _attention(geom, sink, q, k, v):
    blk = ATT_BLOCK
    ncb, nsb = geom.Lc // blk, geom.S // blk
    base = geom.n_ctx // blk

    def q_map(b, i):
        return (jnp.where(i < ncb, b * ncb + i, base + b * nsb + i - ncb), 0)

    def loc_map(off):
        def f(b, i):
            return (base + b * nsb + jnp.clip(i - ncb + off, 0, nsb - 1), 0)
        return f

    ctx_map = lambda b, i: (b, 0)
    kv_specs = [pl.BlockSpec((blk, ATT_KV), loc_map(o)) for o in (-1, 0, 1)]
    return pl.pallas_call(
        functools.partial(_att_body, geom),
        grid=(geom.B, ncb + nsb),
        in_specs=[pl.BlockSpec(memory_space=pltpu.SMEM),
                  pl.BlockSpec((blk, ATT_Q), q_map)] + kv_specs + kv_specs
                 + [pl.BlockSpec((geom.Lc, ATT_KV), ctx_map), pl.BlockSpec((geom.Lc, ATT_KV), ctx_map)],
        out_specs=pl.BlockSpec((blk, ATT_Q), q_map),
        out_shape=jax.ShapeDtypeStruct(q.shape, BF16),
        compiler_params=_cparams("parallel", "parallel"),
        name="window_attention",
    )(sink, q, k, k, k, v, v, v, k, v)


def _halo_specs(geom, tm, halo, width):
    per = tm // halo
    last = geom.N // halo - 1
    return [pl.BlockSpec((halo, width), lambda i: (jnp.maximum(i * per - 1, 0), 0)),
            pl.BlockSpec((tm, width), lambda i: (i, 0)),
            pl.BlockSpec((halo, width), lambda i: (jnp.minimum((i + 1) * per, last), 0))]


def _fill_padded(geom, pad_ref, prev_ref, x_ref, next_ref, tm, halo):
    first, last = geom.seq_edges(pl.program_id(0), tm)
    pad_ref[0:halo, :] = jnp.where(first, 0.0, prev_ref[...])
    pad_ref[halo:halo + tm, :] = x_ref[...]
    pad_ref[halo + tm:2 * halo + tm, :] = jnp.where(last, 0.0, next_ref[...])


def _mconv_body(geom, tm, prev_ref, x_ref, next_ref, w_ref, q_ref, k_ref, pad_ref):
    halo = SUBLANES
    _fill_padded(geom, pad_ref, prev_ref, x_ref, next_ref, tm, halo)
    acc = None
    for j in range(M_SHORT_CONV):
        off = halo + j - M_SHORT_CONV // 2
        t = pad_ref[off:off + tm, :] * w_ref[j:j + 1, :]
        acc = t if acc is None else acc + t
    y = _silu(acc)
    q_ref[...] = y[:, :M_WIDTH].astype(BF16)
    k_ref[...] = (y[:, M_WIDTH:] * M_HEAD_DIM ** -0.5).astype(BF16)


def _mlstm_conv(geom, qkm, w_mconv):
    n, width = qkm.shape
    tm = _row_tile(geom, 256)
    body = functools.partial(_mconv_body, geom, tm)
    return pl.pallas_call(
        body,
        grid=(n // tm,),
        in_specs=_halo_specs(geom, tm, SUBLANES, width) + [_resident(w_mconv.shape)],
        out_specs=[pl.BlockSpec((tm, M_WIDTH), lambda i: (i, 0))] * 2,
        out_shape=[jax.ShapeDtypeStruct((n, M_WIDTH), BF16)] * 2,
        scratch_shapes=[pltpu.VMEM((tm + 2 * SUBLANES, width), F32)],
        compiler_params=_cparams("parallel"),
        name="mlstm_short_conv",
    )(qkm, qkm, qkm, w_mconv)


def _cconv_body(geom, tm, prev_ref, x_ref, next_ref, w_ref, b_ref, g_ref, bl_ref, o_ref, pad_ref):
    halo = CONV_HALO
    _fill_padded(geom, pad_ref, prev_ref, x_ref, next_ref, tm, halo)
    rows = 64
    for r0 in range(0, tm, rows):
        acc = None
        for j in range(CONV_WIDTH):
            off = r0 + halo + j - CONV_WIDTH // 2
            t = pad_ref[off:off + rows, :] * w_ref[j:j + 1, :]
            acc = t if acc is None else acc + t
        y = acc + b_ref[...]
        mu = jnp.mean(y, axis=-1, keepdims=True)
        yc = y - mu
        var = jnp.mean(yc * yc, axis=-1, keepdims=True)
        z = yc * lax.rsqrt(var + EPS) * g_ref[...] + bl_ref[...]
        o_ref[r0:r0 + rows, :] = _silu(z).astype(BF16)


def _conformer_conv(geom, u, w_dw, b_dw, g_ln, b_ln):
    n, width = u.shape
    tm = _row_tile(geom, 256)
    body = functools.partial(_cconv_body, geom, tm)
    return pl.pallas_call(
        body,
        grid=(n // tm,),
        in_specs=_halo_specs(geom, tm, CONV_HALO, width)
                 + [_resident(w_dw.shape), _resident((1, width)), _resident((1, width)), _resident((1, width))],
        out_specs=pl.BlockSpec((tm, width), lambda i: (i, 0)),
        out_shape=jax.ShapeDtypeStruct((n, width), BF16),
        scratch_shapes=[pltpu.VMEM((tm + 2 * CONV_HALO, width), F32)],
        compiler_params=_cparams("parallel"),
        name="conformer_conv",
    )(u, u, u, w_dw, b_dw, g_ln, b_ln)


def _split3(x):
    hi = x.astype(BF16)
    r1 = x - hi.astype(F32)
    mid = r1.astype(BF16)
    lo = (r1 - mid.astype(F32)).astype(BF16)
    return hi, mid, lo


def _mlstm_dir(direction, q_ref, k_ref, v_ref, g_ref, o_ref, c_ref, n_ref, m_ref):
    L = M_CHUNK
    r = lax.broadcasted_iota(jnp.int32, (L, L), 0)
    c = lax.broadcasted_iota(jnp.int32, (L, L), 1)
    keep = (c <= r) if direction == 0 else (c >= r)
    tri = jnp.where(keep, 1.0, 0.0).astype(BF16)
    gates = g_ref[...]
    hi, mid, lo = _split3(_log_sigmoid(gates))
    bcol_all = _dot(tri, hi) + _dot(tri, mid) + _dot(tri, lo)
    brow_all = bcol_all.T
    gates_t = gates.T
    end = L - 1 if direction == 0 else 0
    nt = (((1,), (1,)), ((), ()))
    tn = (((0,), (0,)), ((), ()))
    for h in range(M_HEADS):
        ci = direction * M_HEADS + h
        cf = 2 * M_HEADS + ci
        s = direction * M_HEADS + h
        hs = slice(h * M_HEAD_DIM, (h + 1) * M_HEAD_DIM)
        q, k, v = q_ref[:, hs], k_ref[:, hs], v_ref[:, hs]
        bc = bcol_all[:, cf:cf + 1]
        br = brow_all[cf:cf + 1, :]
        ir = gates_t[ci:ci + 1, :]
        ic = gates[:, ci:ci + 1]
        m_prev = m_ref[s:s + 1, 0:1]
        c_prev = c_ref[s]
        n_prev = n_ref[s:s + 1, :]
        dmat = jnp.where(keep, bc - br + ir, NEG_INF)
        a = bc + m_prev
        mt = jnp.maximum(a, jnp.max(dmat, axis=-1, keepdims=True))
        w_inter = jnp.exp(a - mt)
        smat = lax.dot_general(q, k, nt, preferred_element_type=F32) * jnp.exp(dmat - mt)
        num = w_inter * _dot(q, c_prev.astype(BF16)) + _dot(smat.astype(BF16), v)
        qn = jnp.sum(q.astype(F32) * n_prev, axis=-1, keepdims=True)
        den = w_inter * qn + jnp.sum(smat, axis=-1, keepdims=True)
        o_ref[:, hs] = num / jnp.maximum(jnp.abs(den), jnp.exp(-mt))
        total = bc[end:end + 1, :]
        gcol = total - bc + ic
        m_new = jnp.maximum(total + m_prev, jnp.max(gcol, axis=0, keepdims=True))
        decay = jnp.exp(total + m_prev - m_new)
        kw = k.astype(F32) * jnp.exp(gcol - m_new)
        c_ref[s] = decay * c_prev + lax.dot_general(kw.astype(BF16), v, tn, preferred_element_type=F32)
        n_ref[s:s + 1, :] = decay * n_prev + jnp.sum(kw, axis=0, keepdims=True)
        m_ref[s:s + 1, :] = jnp.broadcast_to(m_new, (1, LANES))


def _mlstm_body(qf, kf, vf, gf, qb, kb, vb, gb, of, ob, c_ref, n_ref, m_ref):
    @pl.when(pl.program_id(1) == 0)
    def _():
        c_ref[...] = jnp.zeros_like(c_ref)
        n_ref[...] = jnp.zeros_like(n_ref)
        m_ref[...] = jnp.full_like(m_ref, NEG_INF)

    _mlstm_dir(0, qf, kf, vf, gf, of, c_ref, n_ref, m_ref)
    _mlstm_dir(1, qb, kb, vb, gb, ob, c_ref, n_ref, m_ref)


def _mlstm_scan(geom, q, k, v, gates):
    L = M_CHUNK
    ncc, nsc = geom.Lc // L, geom.S // L
    base = geom.n_ctx // L

    def fwd(b, j):
        return (jnp.where(j < ncc, b * ncc + j, base + b * nsc + j - ncc), 0)

    def bwd(b, j):
        return (jnp.where(j < ncc, b * ncc + ncc - 1 - j, base + b * nsc + nsc - 1 - (j - ncc)), 0)

    def specs(m):
        return [pl.BlockSpec((L, M_WIDTH), m)] * 3 + [pl.BlockSpec((L, LANES), m)]

    n = q.shape[0]
    return pl.pallas_call(
        _mlstm_body,
        grid=(geom.B, ncc + nsc),
        in_specs=specs(fwd) + specs(bwd),
        out_specs=[pl.BlockSpec((L, M_WIDTH), fwd), pl.BlockSpec((L, M_WIDTH), bwd)],
        out_shape=[jax.ShapeDtypeStruct((n, M_WIDTH), F32)] * 2,
        scratch_shapes=[pltpu.VMEM((2 * M_HEADS, M_HEAD_DIM, M_HEAD_DIM), F32),
                        pltpu.VMEM((2 * M_HEADS, M_HEAD_DIM), F32),
                        pltpu.VMEM((2 * M_HEADS, LANES), F32)],
        compiler_params=_cparams("parallel", "arbitrary"),
        name="mlstm_scan",
    )(q, k, v, gates, q, k, v, gates)


def _merge_body(x_ref, mod_ref, att_ref, cact_ref, hf_ref, hb_ref, om_ref, br_ref, gm_ref,
                wa_ref, wp_ref, wm_ref, wo_ref, o_ref):
    d = x_ref.shape[1]
    hm = _sigmoid(om_ref[...]) * (hf_ref[...] + hb_ref[...])
    parts = []
    for h in range(M_HEADS):
        t = hm[:, h * M_HEAD_DIM:(h + 1) * M_HEAD_DIM]
        mu = jnp.mean(t, axis=-1, keepdims=True)
        tc = t - mu
        var = jnp.mean(tc * tc, axis=-1, keepdims=True)
        parts.append(tc * lax.rsqrt(var + EPS))
    hn = (jnp.concatenate(parts, axis=-1) * gm_ref[...]).astype(BF16)
    ya = _dot(att_ref[...], wa_ref[...])
    yb = _dot(cact_ref[...], wp_ref[...])
    yc = _dot(hn, wm_ref[...])
    merged = (_sigmoid(br_ref[:, 0:d]) * ya + _sigmoid(br_ref[:, d:2 * d]) * yb
              + _sigmoid(br_ref[:, 2 * d:3 * d]) * yc)
    y = _dot(merged.astype(BF16), wo_ref[...])
    o_ref[...] = x_ref[...] + mod_ref[0][:, 2 * d:3 * d] * y


def _merge(geom, x, mod3, att, cact, hf, hb, om, br, g_mnorm, w):
    n, d = x.shape
    tm = _row_tile(geom, 256)
    row = lambda i: (i, 0)
    mod_map = lambda i: (geom.mod_row(i, tm), 0, 0)
    weights = (w["att_out"], w["pw"], w["mout"], w["out"])
    return pl.pallas_call(
        _merge_body,
        grid=(n // tm,),
        in_specs=[pl.BlockSpec((tm, d), row),
                  pl.BlockSpec((1, 1, mod3.shape[2]), mod_map),
                  pl.BlockSpec((tm, ATT_Q), row), pl.BlockSpec((tm, CONV_DIM), row),
                  pl.BlockSpec((tm, M_WIDTH), row), pl.BlockSpec((tm, M_WIDTH), row),
                  pl.BlockSpec((tm, M_WIDTH), row), pl.BlockSpec((tm, 3 * d), row),
                  _resident((1, M_WIDTH))] + [_resident(a.shape) for a in weights],
        out_specs=pl.BlockSpec((tm, d), row),
        out_shape=jax.ShapeDtypeStruct((n, d), F32),
        compiler_params=_cparams("parallel"),
        name="merge_out",
    )(x, mod3, att, cact, hf, hb, om, br, g_mnorm, *weights)


def _ffn_body(n_chunks, x_ref, mod_ref, g_ref, wg_ref, wu_ref, wd_ref, o_ref):
    d = x_ref.shape[1]
    x = x_ref[...]
    mod = mod_ref[0]
    h = _norm_mod(x, g_ref[...], mod[:, 4 * d:5 * d], mod[:, 3 * d:4 * d]).astype(BF16)
    ff = wg_ref.shape[1]
    cw = ff // n_chunks
    acc = None
    for j in range(n_chunks):
        sl = slice(j * cw, (j + 1) * cw)
        act = (_silu(_dot(h, wg_ref[:, sl])) * _dot(h, wu_ref[:, sl])).astype(BF16)
        t = _dot(act, wd_ref[sl, :])
        acc = t if acc is None else acc + t
    o_ref[...] = x + mod[:, 5 * d:6 * d] * acc


def _ffn(geom, x, mod3, g, w):
    n, d = x.shape
    tm = _row_tile(geom, 512)
    row = lambda i: (i, 0)
    mod_map = lambda i: (geom.mod_row(i, tm), 0, 0)
    ff = w["ff_gate"].shape[1]
    n_chunks = 2 if ff % (2 * LANES) == 0 else 1
    weights = (w["ff_gate"], w["ff_up"], w["ff_down"])
    return pl.pallas_call(
        functools.partial(_ffn_body, n_chunks),
        grid=(n // tm,),
        in_specs=[pl.BlockSpec((tm, d), row),
                  pl.BlockSpec((1, 1, mod3.shape[2]), mod_map),
                  _resident((1, d))] + [_resident(a.shape) for a in weights],
        out_specs=pl.BlockSpec((tm, d), row),
        out_shape=jax.ShapeDtypeStruct((n, d), F32),
        compiler_params=_cparams("parallel"),
        name="swiglu_ffn",
    )(x, mod3, g, *weights)


def _final_body(x_ref, g_ref, o_ref):
    x = x_ref[...]
    ms = jnp.mean(x * x, axis=-1, keepdims=True)
    o_ref[...] = x * lax.rsqrt(ms + EPS) * g_ref[...]


def _final_norm(geom, x, g):
    n, d = x.shape
    tm = _row_tile(geom, 512)
    skip = geom.n_ctx // tm
    return pl.pallas_call(
        _final_body,
        grid=((n - geom.n_ctx) // tm,),
        in_specs=[pl.BlockSpec((tm, d), lambda i: (i + skip, 0)), _resident((1, d))],
        out_specs=pl.BlockSpec((tm, d), lambda i: (i, 0)),
        out_shape=jax.ShapeDtypeStruct((n - geom.n_ctx, d), F32),
        compiler_params=_cparams("parallel"),
        name="final_norm",
    )(x, g)


def _rope_tables(geom):
    p = jnp.arange(geom.S)
    n_freq = HEAD_DIM // 4
    inv_freq = ROPE_BASE ** (-jnp.arange(n_freq, dtype=F32) / n_freq)
    ang_r = (p // GRID_W).astype(F32)[:, None] * inv_freq
    ang_c = (p % GRID_W).astype(F32)[:, None] * inv_freq
    ang = jnp.concatenate([ang_r, ang_r, ang_c, ang_c], axis=-1)
    reps = LANES // HEAD_DIM
    cos = jnp.tile(jnp.cos(ang), (geom.B, reps))
    sin = jnp.tile(jnp.sin(ang), (geom.B, reps))
    first_half = (jnp.arange(LANES) % (HEAD_DIM // 2)) < HEAD_DIM // 4
    sin_a = jnp.where(first_half, -sin, 0.0)
    sin_b = jnp.where(first_half, 0.0, sin)
    ones = jnp.ones((geom.n_ctx, LANES), F32)
    zeros = jnp.zeros((geom.n_ctx, LANES), F32)
    return (jnp.concatenate([ones, cos]), jnp.concatenate([zeros, sin_a]), jnp.concatenate([zeros, sin_b]))


def _pack_layer(l, d, w_in, w_att_out, w_conv_pw, w_mlstm_out, w_out, w_ff_gate, w_ff_up, w_ff_down):
    sizes = (ATT_Q + 2 * ATT_KV, 2 * CONV_DIM, 2 * M_WIDTH, M_WIDTH, M_WIDTH, N_GATE_COLS, 3 * d)
    names = ("qkv", "glu", "qkm", "vm", "om", "gate", "br")
    w = {}
    start = 0
    for name, size in zip(names, sizes):
        w[name] = w_in[l, :, start:start + size].astype(BF16)
        start += size
    w["gate"] = jnp.pad(w["gate"], ((0, 0), (0, LANES - N_GATE_COLS)))
    w["att_out"] = w_att_out[l].astype(BF16)
    w["pw"] = w_conv_pw[l].astype(BF16)
    w["mout"] = w_mlstm_out[l].astype(BF16)
    w["out"] = w_out[l].astype(BF16)
    w["ff_gate"] = w_ff_gate[l].astype(BF16)
    w["ff_up"] = w_ff_up[l].astype(BF16)
    w["ff_down"] = w_ff_down[l].astype(BF16)
    return w


def kernel(x, c, ctx, c_ctx, w_ada, b_ada, g_norm_mix, g_norm_ffn, w_in, b_mgate, att_sink, w_att_out, w_conv_dw, b_conv_dw, g_conv_ln, b_conv_ln, w_conv_pw, w_mconv, g_mlstm_norm, w_mlstm_out, w_out, w_ff_gate, w_ff_up, w_ff_down, g_final):
    batch, seq, d = x.shape
    ctx_len = ctx.shape[1]
    depth = w_ada.shape[0]
    geom = _Geom(batch, seq, ctx_len)
    assert seq % ATT_BLOCK == 0 and ctx_len % ATT_BLOCK == 0 and seq % GRID_W == 0
    assert seq % M_CHUNK == 0 and ctx_len % M_CHUNK == 0 and d % LANES == 0

    mod_rows = -(-(batch + 1) // SUBLANES) * SUBLANES
    cvec = jnp.zeros((mod_rows, d), F32).at[:batch].set(c).at[batch].set(c_ctx)
    mod = _modulation(cvec, w_ada, b_ada)
    rope = _rope_tables(geom)
    xs = jnp.concatenate([ctx.reshape(geom.n_ctx, d), x.reshape(batch * seq, d)], axis=0)

    for l in range(depth):
        w = _pack_layer(l, d, w_in, w_att_out, w_conv_pw, w_mlstm_out, w_out, w_ff_gate, w_ff_up, w_ff_down)
        mod3 = mod[l].reshape(mod_rows, 1, 6 * d)
        bgate = jnp.pad(b_mgate[l], (0, LANES - N_GATE_COLS)).reshape(1, LANES)
        q, k, v, u, qkm, vm, om, gates, br = _projection(
            geom, xs, mod3, g_norm_mix[l].reshape(1, d), rope, bgate, w)
        att = _attention(geom, att_sink[l], q, k, v)
        cact = _conformer_conv(geom, u, w_conv_dw[l], b_conv_dw[l].reshape(1, -1),
                               g_conv_ln[l].reshape(1, -1), b_conv_ln[l].reshape(1, -1))
        qm, km = _mlstm_conv(geom, qkm, w_mconv[l])
        hf, hb = _mlstm_scan(geom, qm, km, vm, gates)
        xs = _merge(geom, xs, mod3, att, cact, hf, hb, om, br, g_mlstm_norm[l].reshape(1, -1), w)
        xs = _ffn(geom, xs, mod3, g_norm_ffn[l].reshape(1, d), w)

    return _final_norm(geom, xs, g_final.reshape(1, d)).reshape(batch, seq, d)
```

```python
import functools

import jax
import jax.numpy as jnp
from jax import lax
from jax.experimental import pallas as pl
from jax.experimental.pallas import tpu as pltpu

F32 = jnp.float32
BF16 = jnp.bfloat16

GRID_W = 64
N_Q_HEADS = 8
N_KV_HEADS = 2
HEAD_DIM = 64
WINDOW = 128
ATT_BLOCK = 128
ROPE_BASE = 10000.0
ATT_Q = N_Q_HEADS * HEAD_DIM
ATT_KV = N_KV_HEADS * HEAD_DIM
CONV_DIM = 512
CONV_WIDTH = 31
M_HEADS = 4
M_HEAD_DIM = 128
M_WIDTH = M_HEADS * M_HEAD_DIM
M_SHORT_CONV = 3
N_GATE_COLS = 4 * M_HEADS
EPS = 1e-6
NEG_INF = -1e30

LANES = 128
SUBLANES = 8
M_CHUNK = 128
CONV_HALO = 16
VMEM_LIMIT = 52 * 1024 * 1024


def _cparams(*sem):
    return pltpu.CompilerParams(dimension_semantics=sem, vmem_limit_bytes=VMEM_LIMIT)


def _resident(shape):
    nd = len(shape)
    return pl.BlockSpec(shape, lambda *_: (0,) * nd, pipeline_mode=pl.Buffered(1))


def _sigmoid(x):
    return 1.0 / (1.0 + jnp.exp(-x))


def _silu(x):
    return x * _sigmoid(x)


def _log_sigmoid(x):
    return jnp.minimum(x, 0.0) - jnp.log(1.0 + jnp.exp(-jnp.abs(x)))


def _norm_mod(x, g, sc, sh):
    ms = jnp.mean(x * x, axis=-1, keepdims=True)
    return (x * lax.rsqrt(ms + EPS) * g) * (1.0 + sc) + sh


def _dot(a, b):
    return jnp.dot(a, b, preferred_element_type=F32)


def _mod_body(c_ref, w_ref, b_ref, o_ref):
    h = _silu(c_ref[...]).astype(BF16)
    o_ref[0] = _dot(h, w_ref[0].astype(BF16)) + b_ref[0]


def _modulation(cvec, w_ada, b_ada):
    depth, d, n6 = w_ada.shape
    rows = cvec.shape[0]
    tn = n6 // 4
    return pl.pallas_call(
        _mod_body,
        grid=(depth, n6 // tn),
        in_specs=[pl.BlockSpec((rows, d), lambda l, j: (0, 0)),
                  pl.BlockSpec((1, d, tn), lambda l, j: (l, 0, j)),
                  pl.BlockSpec((1, 1, tn), lambda l, j: (l, 0, j))],
        out_specs=pl.BlockSpec((1, rows, tn), lambda l, j: (l, 0, j)),
        out_shape=jax.ShapeDtypeStruct((depth, rows, n6), F32),
        compiler_params=_cparams("arbitrary", "arbitrary"),
        name="ada_mod",
    )(cvec, w_ada, b_ada.reshape(depth, 1, n6))


class _Geom:
    def __init__(self, batch, seq, ctx_len):
        self.B, self.S, self.Lc = batch, seq, ctx_len
        self.n_ctx = batch * ctx_len
        self.N = self.n_ctx + batch * seq

    def mod_row(self, tile, tm):
        n_ctx_tiles = self.n_ctx // tm
        return jnp.where(tile < n_ctx_tiles, self.B, (tile - n_ctx_tiles) // (self.S // tm))

    def seq_edges(self, tile, tm):
        n_ctx_tiles = self.n_ctx // tm
        tc, ts = self.Lc // tm, self.S // tm
        pos = jnp.where(tile < n_ctx_tiles, tile % tc, (tile - n_ctx_tiles) % ts)
        last = jnp.where(tile < n_ctx_tiles, tc - 1, ts - 1)
        return pos == 0, pos == last


def _row_tile(geom, want):
    tm = want
    while geom.n_ctx % tm or geom.S % tm:
        tm //= 2
    return tm


def _rope(x, cos, sin_a, sin_b):
    return x * cos + pltpu.roll(x, LANES - 16, 1) * sin_a + pltpu.roll(x, 16, 1) * sin_b


def _proj_body(x_ref, mod_ref, g_ref, cos_ref, sa_ref, sb_ref, bg_ref,
               wqkv_ref, wglu_ref, wqkm_ref, wvm_ref, wom_ref, wgate_ref, wbr_ref,
               q_ref, k_ref, v_ref, u_ref, qkm_ref, vm_ref, om_ref, gate_ref, br_ref):
    d = x_ref.shape[1]
    mod = mod_ref[0]
    h = _norm_mod(x_ref[...], g_ref[...], mod[:, d:2 * d], mod[:, 0:d]).astype(BF16)
    cos, sa, sb = cos_ref[...], sa_ref[...], sb_ref[...]

    qkv = _dot(h, wqkv_ref[...])
    scale = HEAD_DIM ** -0.5
    for j in range(ATT_Q // LANES):
        sl = slice(j * LANES, (j + 1) * LANES)
        q_ref[:, sl] = _rope(qkv[:, sl] * scale, cos, sa, sb).astype(BF16)
    lo = lax.broadcasted_iota(jnp.int32, (x_ref.shape[0], LANES), 1) < HEAD_DIM
    for src, ref in ((_rope(qkv[:, ATT_Q:ATT_Q + ATT_KV], cos, sa, sb), k_ref), (qkv[:, ATT_Q + ATT_KV:], v_ref)):
        swapped = pltpu.roll(src, HEAD_DIM, 1)
        ref[:, 0:LANES] = jnp.where(lo, src, swapped).astype(BF16)
        ref[:, LANES:2 * LANES] = jnp.where(lo, swapped, src).astype(BF16)

    glu = _dot(h, wglu_ref[...])
    u_ref[...] = glu[:, :CONV_DIM] * _sigmoid(glu[:, CONV_DIM:])
    qkm_ref[...] = _dot(h, wqkm_ref[...])
    vm_ref[...] = _dot(h, wvm_ref[...]).astype(BF16)
    om_ref[...] = _dot(h, wom_ref[...])
    gate_ref[...] = _dot(h, wgate_ref[...]) + bg_ref[...]
    br_ref[...] = _dot(h, wbr_ref[...])


def _projection(geom, x, mod3, g, rope, bgate, w):
    n, d = x.shape
    tm = _row_tile(geom, 256)
    row = lambda i: (i, 0)
    mod_map = lambda i: (geom.mod_row(i, tm), 0, 0)
    widths = (ATT_Q, N_KV_HEADS * LANES, N_KV_HEADS * LANES, CONV_DIM, 2 * M_WIDTH, M_WIDTH, M_WIDTH, LANES, 3 * d)
    dtypes = (BF16, BF16, BF16, F32, F32, BF16, F32, F32, F32)
    weights = (w["qkv"], w["glu"], w["qkm"], w["vm"], w["om"], w["gate"], w["br"])
    return pl.pallas_call(
        _proj_body,
        grid=(n // tm,),
        in_specs=[pl.BlockSpec((tm, d), row),
                  pl.BlockSpec((1, 1, mod3.shape[2]), mod_map),
                  _resident((1, d)),
                  pl.BlockSpec((tm, LANES), row), pl.BlockSpec((tm, LANES), row), pl.BlockSpec((tm, LANES), row),
                  _resident((1, LANES))]
                 + [_resident(a.shape) for a in weights],
        out_specs=[pl.BlockSpec((tm, c), row) for c in widths],
        out_shape=[jax.ShapeDtypeStruct((n, c), t) for c, t in zip(widths, dtypes)],
        compiler_params=_cparams("parallel"),
        name="in_proj",
    )(x, mod3, g, *rope, bgate, *weights)


def _att_body(geom, sink_ref, q_ref, kp_ref, kc_ref, kn_ref, vp_ref, vc_ref, vn_ref, kx_ref, vx_ref, o_ref):
    blk = ATT_BLOCK
    i = pl.program_id(1)
    ncb = geom.Lc // blk
    n = i - ncb
    r = lax.broadcasted_iota(jnp.int32, (blk, 3 * blk), 0)
    c = lax.broadcasted_iota(jnp.int32, (blk, 3 * blk), 1)
    key_pos = (n - 1) * blk + c
    q_pos = n * blk + r
    valid = (jnp.abs(key_pos - q_pos) <= WINDOW) & (key_pos >= 0) & (key_pos < geom.S) & (n >= 0)
    bias = jnp.where(valid, 0.0, NEG_INF)
    group = N_Q_HEADS // N_KV_HEADS
    lo = lax.broadcasted_iota(jnp.int32, (blk, LANES), 1) < HEAD_DIM
    zero = jnp.zeros((blk, LANES), BF16)
    nt = (((1,), (1,)), ((), ()))
    for hk in range(N_KV_HEADS):
        ks = slice(hk * LANES, (hk + 1) * LANES)
        k_loc = jnp.concatenate([kp_ref[:, ks], kc_ref[:, ks], kn_ref[:, ks]], axis=0)
        v_loc = jnp.concatenate([vp_ref[:, ks], vc_ref[:, ks], vn_ref[:, ks]], axis=0)
        k_ctx, v_ctx = kx_ref[:, ks], vx_ref[:, ks]
        tiles = [q_ref[:, (hk * group + 2 * p) * HEAD_DIM:(hk * group + 2 * p + 2) * HEAD_DIM]
                 for p in range(group // 2)]
        qs = jnp.concatenate([jnp.where(lo if half == 0 else ~lo, t, zero)
                              for t in tiles for half in (0, 1)], axis=0)
        s_loc = lax.dot_general(qs, k_loc, nt, preferred_element_type=F32)
        s_ctx = lax.dot_general(qs, k_ctx, nt, preferred_element_type=F32)
        s_loc = s_loc + jnp.concatenate([bias] * group, axis=0)
        sk = jnp.concatenate(
            [jnp.full((blk, 1), sink_ref[hk * group + j], F32) for j in range(group)], axis=0)
        m = jnp.maximum(jnp.maximum(jnp.max(s_loc, axis=-1, keepdims=True),
                                    jnp.max(s_ctx, axis=-1, keepdims=True)), sk)
        e_loc = jnp.exp(s_loc - m)
        e_ctx = jnp.exp(s_ctx - m)
        den = jnp.sum(e_loc, axis=-1, keepdims=True) + jnp.sum(e_ctx, axis=-1, keepdims=True) + jnp.exp(sk - m)
        o = (_dot(e_loc.astype(BF16), v_loc) + _dot(e_ctx.astype(BF16), v_ctx)) * (1.0 / den)
        for p in range(group // 2):
            pair = jnp.where(lo, o[2 * p * blk:(2 * p + 1) * blk], o[(2 * p + 1) * blk:(2 * p + 2) * blk])
            c0 = (hk * group + 2 * p) * HEAD_DIM
            o_ref[:, c0:c0 + LANES] = pair.astype(BF16)


def _attention(geom, sink, q, k, v):
    blk = ATT_BLOCK
    ncb, nsb = geom.Lc // blk, geom.S // blk
    base = geom.n_ctx // blk

    def q_map(b, i):
        return (jnp.where(i < ncb, b * ncb + i, base + b * nsb + i - ncb), 0)

    def loc_map(off):
        def f(b, i):
            return (base + b * nsb + jnp.clip(i - ncb + off, 0, nsb - 1), 0)
        return f

    ctx_map = lambda b, i: (b, 0)
    kv_w = N_KV_HEADS * LANES
    kv_specs = [pl.BlockSpec((blk, kv_w), loc_map(o)) for o in (-1, 0, 1)]
    return pl.pallas_call(
        functools.partial(_att_body, geom),
        grid=(geom.B, ncb + nsb),
        in_specs=[pl.BlockSpec(memory_space=pltpu.SMEM),
                  pl.BlockSpec((blk, ATT_Q), q_map)] + kv_specs + kv_specs
                 + [pl.BlockSpec((geom.Lc, kv_w), ctx_map), pl.BlockSpec((geom.Lc, kv_w), ctx_map)],
        out_specs=pl.BlockSpec((blk, ATT_Q), q_map),
        out_shape=jax.ShapeDtypeStruct(q.shape, BF16),
        compiler_params=_cparams("parallel", "parallel"),
        name="window_attention",
    )(sink, q, k, k, k, v, v, v, k, v)


def _halo_specs(geom, tm, halo, width):
    per = tm // halo
    last = geom.N // halo - 1
    return [pl.BlockSpec((halo, width), lambda i: (jnp.maximum(i * per - 1, 0), 0)),
            pl.BlockSpec((tm, width), lambda i: (i, 0)),
            pl.BlockSpec((halo, width), lambda i: (jnp.minimum((i + 1) * per, last), 0))]


def _fill_padded(geom, pad_ref, prev_ref, x_ref, next_ref, tm, halo):
    first, last = geom.seq_edges(pl.program_id(0), tm)
    pad_ref[0:halo, :] = jnp.where(first, 0.0, prev_ref[...])
    pad_ref[halo:halo + tm, :] = x_ref[...]
    pad_ref[halo + tm:2 * halo + tm, :] = jnp.where(last, 0.0, next_ref[...])


def _mconv_body(geom, tm, prev_ref, x_ref, next_ref, w_ref, q_ref, k_ref, pad_ref):
    halo = SUBLANES
    _fill_padded(geom, pad_ref, prev_ref, x_ref, next_ref, tm, halo)
    acc = None
    for j in range(M_SHORT_CONV):
        off = halo + j - M_SHORT_CONV // 2
        t = pad_ref[off:off + tm, :] * w_ref[j:j + 1, :]
        acc = t if acc is None else acc + t
    y = _silu(acc)
    q_ref[...] = y[:, :M_WIDTH].astype(BF16)
    k_ref[...] = (y[:, M_WIDTH:] * M_HEAD_DIM ** -0.5).astype(BF16)


def _mlstm_conv(geom, qkm, w_mconv):
    n, width = qkm.shape
    tm = _row_tile(geom, 256)
    body = functools.partial(_mconv_body, geom, tm)
    return pl.pallas_call(
        body,
        grid=(n // tm,),
        in_specs=_halo_specs(geom, tm, SUBLANES, width) + [_resident(w_mconv.shape)],
        out_specs=[pl.BlockSpec((tm, M_WIDTH), lambda i: (i, 0))] * 2,
        out_shape=[jax.ShapeDtypeStruct((n, M_WIDTH), BF16)] * 2,
        scratch_shapes=[pltpu.VMEM((tm + 2 * SUBLANES, width), F32)],
        compiler_params=_cparams("parallel"),
        name="mlstm_short_conv",
    )(qkm, qkm, qkm, w_mconv)


def _cconv_body(geom, tm, prev_ref, x_ref, next_ref, w_ref, b_ref, g_ref, bl_ref, o_ref, sh_ref):
    halo = CONV_HALO
    _fill_padded(geom, sh_ref.at[0], prev_ref, x_ref, next_ref, tm, halo)
    span = tm + 2 * halo - SUBLANES
    for s in range(1, SUBLANES):
        sh_ref[s, 0:span, :] = sh_ref[0, s:s + span, :]
    rows = 64
    for r0 in range(0, tm, rows):
        acc = None
        for j in range(CONV_WIDTH):
            a, s = divmod(halo + j - CONV_WIDTH // 2, SUBLANES)
            off = r0 + a * SUBLANES
            win = sh_ref[s, off:off + rows, :].reshape(rows // SUBLANES, SUBLANES, -1)
            t = win * w_ref[j * SUBLANES:(j + 1) * SUBLANES, :][None]
            acc = t if acc is None else acc + t
        y = acc.reshape(rows, -1) + b_ref[...]
        mu = jnp.mean(y, axis=-1, keepdims=True)
        yc = y - mu
        var = jnp.mean(yc * yc, axis=-1, keepdims=True)
        z = yc * lax.rsqrt(var + EPS) * g_ref[...] + bl_ref[...]
        o_ref[r0:r0 + rows, :] = _silu(z).astype(BF16)


def _conformer_conv(geom, u, w_dw, b_dw, g_ln, b_ln):
    n, width = u.shape
    tm = _row_tile(geom, 256)
    body = functools.partial(_cconv_body, geom, tm)
    return pl.pallas_call(
        body,
        grid=(n // tm,),
        in_specs=_halo_specs(geom, tm, CONV_HALO, width)
                 + [_resident(w_dw.shape), _resident((1, width)), _resident((1, width)), _resident((1, width))],
        out_specs=pl.BlockSpec((tm, width), lambda i: (i, 0)),
        out_shape=jax.ShapeDtypeStruct((n, width), BF16),
        scratch_shapes=[pltpu.VMEM((SUBLANES, tm + 2 * CONV_HALO, width), F32)],
        compiler_params=_cparams("parallel"),
        name="conformer_conv",
    )(u, u, u, w_dw, b_dw, g_ln, b_ln)


def _split3(x):
    hi = x.astype(BF16)
    r1 = x - hi.astype(F32)
    mid = r1.astype(BF16)
    lo = (r1 - mid.astype(F32)).astype(BF16)
    return hi, mid, lo


def _mlstm_dir(direction, q_ref, k_ref, v_ref, g_ref, o_ref, c_ref, n_ref, m_ref):
    L = M_CHUNK
    r = lax.broadcasted_iota(jnp.int32, (L, L), 0)
    c = lax.broadcasted_iota(jnp.int32, (L, L), 1)
    keep = (c <= r) if direction == 0 else (c >= r)
    tri = jnp.where(keep, 1.0, 0.0).astype(BF16)
    gates = g_ref[...]
    hi, mid, lo = _split3(_log_sigmoid(gates))
    bcol_all = _dot(tri, hi) + _dot(tri, mid) + _dot(tri, lo)
    brow_all = bcol_all.T
    gates_t = gates.T
    end = L - 1 if direction == 0 else 0
    nt = (((1,), (1,)), ((), ()))
    tn = (((0,), (0,)), ((), ()))
    for h in range(M_HEADS):
        ci = direction * M_HEADS + h
        cf = 2 * M_HEADS + ci
        s = direction * M_HEADS + h
        hs = slice(h * M_HEAD_DIM, (h + 1) * M_HEAD_DIM)
        q, k, v = q_ref[:, hs], k_ref[:, hs], v_ref[:, hs]
        bc = bcol_all[:, cf:cf + 1]
        br = brow_all[cf:cf + 1, :]
        ir = gates_t[ci:ci + 1, :]
        ic = gates[:, ci:ci + 1]
        m_prev = m_ref[s:s + 1, 0:1]
        c_prev = c_ref[s]
        n_prev = n_ref[s:s + 1, :]
        dmat = jnp.where(keep, bc - br + ir, NEG_INF)
        a = bc + m_prev
        mt = jnp.maximum(a, jnp.max(dmat, axis=-1, keepdims=True))
        w_inter = jnp.exp(a - mt)
        smat = lax.dot_general(q, k, nt, preferred_element_type=F32) * jnp.exp(dmat - mt)
        num = w_inter * _dot(q, c_prev.astype(BF16)) + _dot(smat.astype(BF16), v)
        qn = jnp.sum(q.astype(F32) * n_prev, axis=-1, keepdims=True)
        den = w_inter * qn + jnp.sum(smat, axis=-1, keepdims=True)
        o_ref[:, hs] = num / jnp.maximum(jnp.abs(den), jnp.exp(-mt))
        total = bc[end:end + 1, :]
        gcol = total - bc + ic
        m_new = jnp.maximum(total + m_prev, jnp.max(gcol, axis=0, keepdims=True))
        decay = jnp.exp(total + m_prev - m_new)
        kw = k.astype(F32) * jnp.exp(gcol - m_new)
        c_ref[s] = decay * c_prev + lax.dot_general(kw.astype(BF16), v, tn, preferred_element_type=F32)
        n_ref[s:s + 1, :] = decay * n_prev + jnp.sum(kw, axis=0, keepdims=True)
        m_ref[s:s + 1, :] = jnp.broadcast_to(m_new, (1, LANES))


def _mlstm_body(qf, kf, vf, gf, qb, kb, vb, gb, of, ob, c_ref, n_ref, m_ref):
    @pl.when(pl.program_id(1) == 0)
    def _():
        c_ref[...] = jnp.zeros_like(c_ref)
        n_ref[...] = jnp.zeros_like(n_ref)
        m_ref[...] = jnp.full_like(m_ref, NEG_INF)

    _mlstm_dir(0, qf, kf, vf, gf, of, c_ref, n_ref, m_ref)
    _mlstm_dir(1, qb, kb, vb, gb, ob, c_ref, n_ref, m_ref)


def _mlstm_scan(geom, q, k, v, gates):
    L = M_CHUNK
    ncc, nsc = geom.Lc // L, geom.S // L
    base = geom.n_ctx // L

    def fwd(b, j):
        return (jnp.where(j < ncc, b * ncc + j, base + b * nsc + j - ncc), 0)

    def bwd(b, j):
        return (jnp.where(j < ncc, b * ncc + ncc - 1 - j, base + b * nsc + nsc - 1 - (j - ncc)), 0)

    def specs(m):
        return [pl.BlockSpec((L, M_WIDTH), m)] * 3 + [pl.BlockSpec((L, LANES), m)]

    n = q.shape[0]
    return pl.pallas_call(
        _mlstm_body,
        grid=(geom.B, ncc + nsc),
        in_specs=specs(fwd) + specs(bwd),
        out_specs=[pl.BlockSpec((L, M_WIDTH), fwd), pl.BlockSpec((L, M_WIDTH), bwd)],
        out_shape=[jax.ShapeDtypeStruct((n, M_WIDTH), F32)] * 2,
        scratch_shapes=[pltpu.VMEM((2 * M_HEADS, M_HEAD_DIM, M_HEAD_DIM), F32),
                        pltpu.VMEM((2 * M_HEADS, M_HEAD_DIM), F32),
                        pltpu.VMEM((2 * M_HEADS, LANES), F32)],
        compiler_params=_cparams("parallel", "arbitrary"),
        name="mlstm_scan",
    )(q, k, v, gates, q, k, v, gates)


def _merge_body(x_ref, mod_ref, att_ref, cact_ref, hf_ref, hb_ref, om_ref, br_ref, gm_ref,
                wa_ref, wp_ref, wm_ref, wo_ref, o_ref):
    d = x_ref.shape[1]
    hm = _sigmoid(om_ref[...]) * (hf_ref[...] + hb_ref[...])
    parts = []
    for h in range(M_HEADS):
        t = hm[:, h * M_HEAD_DIM:(h + 1) * M_HEAD_DIM]
        mu = jnp.mean(t, axis=-1, keepdims=True)
        tc = t - mu
        var = jnp.mean(tc * tc, axis=-1, keepdims=True)
        parts.append(tc * lax.rsqrt(var + EPS))
    hn = (jnp.concatenate(parts, axis=-1) * gm_ref[...]).astype(BF16)
    ya = _dot(att_ref[...], wa_ref[...])
    yb = _dot(cact_ref[...], wp_ref[...])
    yc = _dot(hn, wm_ref[...])
    merged = (_sigmoid(br_ref[:, 0:d]) * ya + _sigmoid(br_ref[:, d:2 * d]) * yb
              + _sigmoid(br_ref[:, 2 * d:3 * d]) * yc)
    y = _dot(merged.astype(BF16), wo_ref[...])
    o_ref[...] = x_ref[...] + mod_ref[0][:, 2 * d:3 * d] * y


def _merge(geom, x, mod3, att, cact, hf, hb, om, br, g_mnorm, w):
    n, d = x.shape
    tm = _row_tile(geom, 256)
    row = lambda i: (i, 0)
    mod_map = lambda i: (geom.mod_row(i, tm), 0, 0)
    weights = (w["att_out"], w["pw"], w["mout"], w["out"])
    return pl.pallas_call(
        _merge_body,
        grid=(n // tm,),
        in_specs=[pl.BlockSpec((tm, d), row),
                  pl.BlockSpec((1, 1, mod3.shape[2]), mod_map),
                  pl.BlockSpec((tm, ATT_Q), row), pl.BlockSpec((tm, CONV_DIM), row),
                  pl.BlockSpec((tm, M_WIDTH), row), pl.BlockSpec((tm, M_WIDTH), row),
                  pl.BlockSpec((tm, M_WIDTH), row), pl.BlockSpec((tm, 3 * d), row),
                  _resident((1, M_WIDTH))] + [_resident(a.shape) for a in weights],
        out_specs=pl.BlockSpec((tm, d), row),
        out_shape=jax.ShapeDtypeStruct((n, d), F32),
        compiler_params=_cparams("parallel"),
        name="merge_out",
    )(x, mod3, att, cact, hf, hb, om, br, g_mnorm, *weights)


def _ffn_body(n_chunks, x_ref, mod_ref, g_ref, wg_ref, wu_ref, wd_ref, o_ref):
    d = x_ref.shape[1]
    x = x_ref[...]
    mod = mod_ref[0]
    h = _norm_mod(x, g_ref[...], mod[:, 4 * d:5 * d], mod[:, 3 * d:4 * d]).astype(BF16)
    ff = wg_ref.shape[1]
    cw = ff // n_chunks
    acc = None
    for j in range(n_chunks):
        sl = slice(j * cw, (j + 1) * cw)
        act = (_silu(_dot(h, wg_ref[:, sl])) * _dot(h, wu_ref[:, sl])).astype(BF16)
        t = _dot(act, wd_ref[sl, :])
        acc = t if acc is None else acc + t
    o_ref[...] = x + mod[:, 5 * d:6 * d] * acc


def _ffn(geom, x, mod3, g, w):
    n, d = x.shape
    tm = _row_tile(geom, 512)
    row = lambda i: (i, 0)
    mod_map = lambda i: (geom.mod_row(i, tm), 0, 0)
    ff = w["ff_gate"].shape[1]
    n_chunks = 2 if ff % (2 * LANES) == 0 else 1
    weights = (w["ff_gate"], w["ff_up"], w["ff_down"])
    return pl.pallas_call(
        functools.partial(_ffn_body, n_chunks),
        grid=(n // tm,),
        in_specs=[pl.BlockSpec((tm, d), row),
                  pl.BlockSpec((1, 1, mod3.shape[2]), mod_map),
                  _resident((1, d))] + [_resident(a.shape) for a in weights],
        out_specs=pl.BlockSpec((tm, d), row),
        out_shape=jax.ShapeDtypeStruct((n, d), F32),
        compiler_params=_cparams("parallel"),
        name="swiglu_ffn",
    )(x, mod3, g, *weights)


def _final_body(x_ref, g_ref, o_ref):
    x = x_ref[...]
    ms = jnp.mean(x * x, axis=-1, keepdims=True)
    o_ref[...] = x * lax.rsqrt(ms + EPS) * g_ref[...]


def _final_norm(geom, x, g):
    n, d = x.shape
    tm = _row_tile(geom, 512)
    skip = geom.n_ctx // tm
    return pl.pallas_call(
        _final_body,
        grid=((n - geom.n_ctx) // tm,),
        in_specs=[pl.BlockSpec((tm, d), lambda i: (i + skip, 0)), _resident((1, d))],
        out_specs=pl.BlockSpec((tm, d), lambda i: (i, 0)),
        out_shape=jax.ShapeDtypeStruct((n - geom.n_ctx, d), F32),
        compiler_params=_cparams("parallel"),
        name="final_norm",
    )(x, g)


def _rope_tables(geom):
    p = jnp.arange(geom.S)
    n_freq = HEAD_DIM // 4
    inv_freq = ROPE_BASE ** (-jnp.arange(n_freq, dtype=F32) / n_freq)
    ang_r = (p // GRID_W).astype(F32)[:, None] * inv_freq
    ang_c = (p % GRID_W).astype(F32)[:, None] * inv_freq
    ang = jnp.concatenate([ang_r, ang_r, ang_c, ang_c], axis=-1)
    reps = LANES // HEAD_DIM
    cos = jnp.tile(jnp.cos(ang), (geom.B, reps))
    sin = jnp.tile(jnp.sin(ang), (geom.B, reps))
    first_half = (jnp.arange(LANES) % (HEAD_DIM // 2)) < HEAD_DIM // 4
    sin_a = jnp.where(first_half, -sin, 0.0)
    sin_b = jnp.where(first_half, 0.0, sin)
    ones = jnp.ones((geom.n_ctx, LANES), F32)
    zeros = jnp.zeros((geom.n_ctx, LANES), F32)
    return (jnp.concatenate([ones, cos]), jnp.concatenate([zeros, sin_a]), jnp.concatenate([zeros, sin_b]))


def _pack_layer(l, d, w_in, w_att_out, w_conv_pw, w_mlstm_out, w_out, w_ff_gate, w_ff_up, w_ff_down):
    sizes = (ATT_Q + 2 * ATT_KV, 2 * CONV_DIM, 2 * M_WIDTH, M_WIDTH, M_WIDTH, N_GATE_COLS, 3 * d)
    names = ("qkv", "glu", "qkm", "vm", "om", "gate", "br")
    w = {}
    start = 0
    for name, size in zip(names, sizes):
        w[name] = w_in[l, :, start:start + size].astype(BF16)
        start += size
    w["gate"] = jnp.pad(w["gate"], ((0, 0), (0, LANES - N_GATE_COLS)))
    w["att_out"] = w_att_out[l].astype(BF16)
    w["pw"] = w_conv_pw[l].astype(BF16)
    w["mout"] = w_mlstm_out[l].astype(BF16)
    w["out"] = w_out[l].astype(BF16)
    w["ff_gate"] = w_ff_gate[l].astype(BF16)
    w["ff_up"] = w_ff_up[l].astype(BF16)
    w["ff_down"] = w_ff_down[l].astype(BF16)
    return w


def kernel(x, c, ctx, c_ctx, w_ada, b_ada, g_norm_mix, g_norm_ffn, w_in, b_mgate, att_sink, w_att_out, w_conv_dw, b_conv_dw, g_conv_ln, b_conv_ln, w_conv_pw, w_mconv, g_mlstm_norm, w_mlstm_out, w_out, w_ff_gate, w_ff_up, w_ff_down, g_final):
    batch, seq, d = x.shape
    ctx_len = ctx.shape[1]
    depth = w_ada.shape[0]
    geom = _Geom(batch, seq, ctx_len)
    assert seq % ATT_BLOCK == 0 and ctx_len % ATT_BLOCK == 0 and seq % GRID_W == 0
    assert seq % M_CHUNK == 0 and ctx_len % M_CHUNK == 0 and d % LANES == 0

    mod_rows = -(-(batch + 1) // SUBLANES) * SUBLANES
    cvec = jnp.zeros((mod_rows, d), F32).at[:batch].set(c).at[batch].set(c_ctx)
    mod = _modulation(cvec, w_ada, b_ada)
    rope = _rope_tables(geom)
    xs = jnp.concatenate([ctx.reshape(geom.n_ctx, d), x.reshape(batch * seq, d)], axis=0)

    for l in range(depth):
        w = _pack_layer(l, d, w_in, w_att_out, w_conv_pw, w_mlstm_out, w_out, w_ff_gate, w_ff_up, w_ff_down)
        mod3 = mod[l].reshape(mod_rows, 1, 6 * d)
        bgate = jnp.pad(b_mgate[l], (0, LANES - N_GATE_COLS)).reshape(1, LANES)
        q, k, v, u, qkm, vm, om, gates, br = _projection(
            geom, xs, mod3, g_norm_mix[l].reshape(1, d), rope, bgate, w)
        att = _attention(geom, att_sink[l], q, k, v)
        cact = _conformer_conv(geom, u, jnp.repeat(w_conv_dw[l], SUBLANES, axis=0), b_conv_dw[l].reshape(1, -1),
                               g_conv_ln[l].reshape(1, -1), b_conv_ln[l].reshape(1, -1))
        qm, km = _mlstm_conv(geom, qkm, w_mconv[l])
        hf, hb = _mlstm_scan(geom, qm, km, vm, gates)
        xs = _merge(geom, xs, mod3, att, cact, hf, hb, om, br, g_mlstm_norm[l].reshape(1, -1), w)
        xs = _ffn(geom, xs, mod3, g_norm_ffn[l].reshape(1, d), w)

    return _final_norm(geom, xs, g_final.reshape(1, d)).reshape(batch, seq, d)
```

```python
import functools

import jax
import jax.numpy as jnp
from jax import lax
from jax.experimental import pallas as pl
from jax.experimental.pallas import tpu as pltpu

F32 = jnp.float32
BF16 = jnp.bfloat16

GRID_W = 64
N_Q_HEADS = 8
N_KV_HEADS = 2
HEAD_DIM = 64
WINDOW = 128
ATT_BLOCK = 128
ROPE_BASE = 10000.0
ATT_Q = N_Q_HEADS * HEAD_DIM
ATT_KV = N_KV_HEADS * HEAD_DIM
CONV_DIM = 512
CONV_WIDTH = 31
M_HEADS = 4
M_HEAD_DIM = 128
M_WIDTH = M_HEADS * M_HEAD_DIM
M_SHORT_CONV = 3
N_GATE_COLS = 4 * M_HEADS
EPS = 1e-6
NEG_INF = -1e30

LANES = 128
SUBLANES = 8
M_CHUNK = 256
M_AUG = 16
CONV_HALO = 16
VMEM_LIMIT = 52 * 1024 * 1024


def _cparams(*sem):
    return pltpu.CompilerParams(dimension_semantics=sem, vmem_limit_bytes=VMEM_LIMIT)


def _resident(shape):
    nd = len(shape)
    return pl.BlockSpec(shape, lambda *_: (0,) * nd, pipeline_mode=pl.Buffered(1))


def _sigmoid(x):
    return 1.0 / (1.0 + jnp.exp(-x))


def _silu(x):
    return x * _sigmoid(x)


def _log_sigmoid(x):
    return jnp.minimum(x, 0.0) - jnp.log(1.0 + jnp.exp(-jnp.abs(x)))


def _norm_mod(x, g, sc, sh):
    ms = jnp.mean(x * x, axis=-1, keepdims=True)
    return (x * lax.rsqrt(ms + EPS) * g) * (1.0 + sc) + sh


def _dot(a, b):
    return jnp.dot(a, b, preferred_element_type=F32)


def _mod_body(c_ref, w_ref, b_ref, o_ref):
    h = _silu(c_ref[...]).astype(BF16)
    o_ref[0] = _dot(h, w_ref[0].astype(BF16)) + b_ref[0]


def _modulation(cvec, w_ada, b_ada):
    depth, d, n6 = w_ada.shape
    rows = cvec.shape[0]
    tn = n6 // 4
    return pl.pallas_call(
        _mod_body,
        grid=(depth, n6 // tn),
        in_specs=[pl.BlockSpec((rows, d), lambda l, j: (0, 0)),
                  pl.BlockSpec((1, d, tn), lambda l, j: (l, 0, j)),
                  pl.BlockSpec((1, 1, tn), lambda l, j: (l, 0, j))],
        out_specs=pl.BlockSpec((1, rows, tn), lambda l, j: (l, 0, j)),
        out_shape=jax.ShapeDtypeStruct((depth, rows, n6), F32),
        compiler_params=_cparams("arbitrary", "arbitrary"),
        name="ada_mod",
    )(cvec, w_ada, b_ada.reshape(depth, 1, n6))


class _Geom:
    def __init__(self, batch, seq, ctx_len):
        self.B, self.S, self.Lc = batch, seq, ctx_len
        self.n_ctx = batch * ctx_len
        self.N = self.n_ctx + batch * seq

    def mod_row(self, tile, tm):
        n_ctx_tiles = self.n_ctx // tm
        return jnp.where(tile < n_ctx_tiles, self.B, (tile - n_ctx_tiles) // (self.S // tm))

    def seq_edges(self, tile, tm):
        n_ctx_tiles = self.n_ctx // tm
        tc, ts = self.Lc // tm, self.S // tm
        pos = jnp.where(tile < n_ctx_tiles, tile % tc, (tile - n_ctx_tiles) % ts)
        last = jnp.where(tile < n_ctx_tiles, tc - 1, ts - 1)
        return pos == 0, pos == last


def _row_tile(geom, want):
    tm = want
    while geom.n_ctx % tm or geom.S % tm:
        tm //= 2
    return tm


def _rope(x, cos, sin_a, sin_b):
    return x * cos + pltpu.roll(x, LANES - 16, 1) * sin_a + pltpu.roll(x, 16, 1) * sin_b


def _proj_body(x_ref, mod_ref, g_ref, cos_ref, sa_ref, sb_ref, bg_ref, bgt_ref,
               wqkv_ref, wglu_ref, wqkm_ref, wvgt_ref, wom_ref, wgate_ref, wbr_ref,
               q_ref, k_ref, v_ref, u_ref, qkm_ref, vmt_ref, om_ref, gate_ref, gatet_ref, br_ref):
    d = x_ref.shape[1]
    mod = mod_ref[0]
    h = _norm_mod(x_ref[...], g_ref[...], mod[:, d:2 * d], mod[:, 0:d]).astype(BF16)
    cos, sa, sb = cos_ref[...], sa_ref[...], sb_ref[...]

    qkv = _dot(h, wqkv_ref[...])
    scale = HEAD_DIM ** -0.5
    for j in range(ATT_Q // LANES):
        sl = slice(j * LANES, (j + 1) * LANES)
        q_ref[:, sl] = _rope(qkv[:, sl] * scale, cos, sa, sb).astype(BF16)
    lo = lax.broadcasted_iota(jnp.int32, (x_ref.shape[0], LANES), 1) < HEAD_DIM
    for src, ref in ((_rope(qkv[:, ATT_Q:ATT_Q + ATT_KV], cos, sa, sb), k_ref), (qkv[:, ATT_Q + ATT_KV:], v_ref)):
        swapped = pltpu.roll(src, HEAD_DIM, 1)
        ref[:, 0:LANES] = jnp.where(lo, src, swapped).astype(BF16)
        ref[:, LANES:2 * LANES] = jnp.where(lo, swapped, src).astype(BF16)

    glu = _dot(h, wglu_ref[...])
    u_ref[...] = glu[:, :CONV_DIM] * _sigmoid(glu[:, CONV_DIM:])
    qkm_ref[...] = _dot(h, wqkm_ref[...])
    vgt = lax.dot_general(wvgt_ref[...], h, (((1,), (1,)), ((), ())), preferred_element_type=F32)
    vmt_ref[...] = vgt[:M_WIDTH].astype(BF16)
    gatet_ref[...] = vgt[M_WIDTH:] + bgt_ref[...]
    om_ref[...] = _dot(h, wom_ref[...]).astype(BF16)
    gate_ref[...] = _dot(h, wgate_ref[...]) + bg_ref[...]
    br_ref[...] = _dot(h, wbr_ref[...]).astype(BF16)


def _projection(geom, x, mod3, g, rope, bgate, w):
    n, d = x.shape
    tm = _row_tile(geom, 256)
    row = lambda i: (i, 0)
    mod_map = lambda i: (geom.mod_row(i, tm), 0, 0)
    col = lambda i: (0, i)
    outs = ((ATT_Q, BF16, False), (N_KV_HEADS * LANES, BF16, False), (N_KV_HEADS * LANES, BF16, False),
            (CONV_DIM, F32, False), (2 * M_WIDTH, F32, False), (M_WIDTH, BF16, True), (M_WIDTH, BF16, False),
            (LANES, F32, False), (N_GATE_COLS, F32, True), (3 * d, BF16, False))
    weights = (w["qkv"], w["glu"], w["qkm"], w["vgt"], w["om"], w["gate"], w["br"])
    return pl.pallas_call(
        _proj_body,
        grid=(n // tm,),
        in_specs=[pl.BlockSpec((tm, d), row),
                  pl.BlockSpec((1, 1, mod3.shape[2]), mod_map),
                  _resident((1, d)),
                  pl.BlockSpec((tm, LANES), row), pl.BlockSpec((tm, LANES), row), pl.BlockSpec((tm, LANES), row),
                  _resident((1, LANES)), _resident((N_GATE_COLS, 1))]
                 + [_resident(a.shape) for a in weights],
        out_specs=[pl.BlockSpec((c, tm), col) if fm else pl.BlockSpec((tm, c), row) for c, _, fm in outs],
        out_shape=[jax.ShapeDtypeStruct((c, n) if fm else (n, c), t) for c, t, fm in outs],
        compiler_params=_cparams("parallel"),
        name="in_proj",
    )(x, mod3, g, *rope, bgate, bgate[0, :N_GATE_COLS].reshape(N_GATE_COLS, 1), *weights)


def _att_body(geom, sink_ref, q_ref, kp_ref, kc_ref, kn_ref, vp_ref, vc_ref, vn_ref, kx_ref, vx_ref, o_ref):
    blk = ATT_BLOCK
    i = pl.program_id(1)
    ncb = geom.Lc // blk
    n = i - ncb
    r = lax.broadcasted_iota(jnp.int32, (blk, 3 * blk), 0)
    c = lax.broadcasted_iota(jnp.int32, (blk, 3 * blk), 1)
    key_pos = (n - 1) * blk + c
    q_pos = n * blk + r
    valid = (jnp.abs(key_pos - q_pos) <= WINDOW) & (key_pos >= 0) & (key_pos < geom.S) & (n >= 0)
    bias = jnp.where(valid, 0.0, NEG_INF)
    group = N_Q_HEADS // N_KV_HEADS
    lo = lax.broadcasted_iota(jnp.int32, (blk, LANES), 1) < HEAD_DIM
    zero = jnp.zeros((blk, LANES), BF16)
    nt = (((1,), (1,)), ((), ()))
    for hk in range(N_KV_HEADS):
        ks = slice(hk * LANES, (hk + 1) * LANES)
        k_loc = jnp.concatenate([kp_ref[:, ks], kc_ref[:, ks], kn_ref[:, ks]], axis=0)
        v_loc = jnp.concatenate([vp_ref[:, ks], vc_ref[:, ks], vn_ref[:, ks]], axis=0)
        k_ctx, v_ctx = kx_ref[:, ks], vx_ref[:, ks]
        tiles = [q_ref[:, (hk * group + 2 * p) * HEAD_DIM:(hk * group + 2 * p + 2) * HEAD_DIM]
                 for p in range(group // 2)]
        qs = jnp.concatenate([jnp.where(lo if half == 0 else ~lo, t, zero)
                              for t in tiles for half in (0, 1)], axis=0)
        s_loc = lax.dot_general(qs, k_loc, nt, preferred_element_type=F32)
        s_ctx = lax.dot_general(qs, k_ctx, nt, preferred_element_type=F32)
        s_loc = s_loc + jnp.concatenate([bias] * group, axis=0)
        sk = jnp.concatenate(
            [jnp.full((blk, 1), sink_ref[hk * group + j], F32) for j in range(group)], axis=0)
        m = jnp.maximum(jnp.maximum(jnp.max(s_loc, axis=-1, keepdims=True),
                                    jnp.max(s_ctx, axis=-1, keepdims=True)), sk)
        e_loc = jnp.exp(s_loc - m)
        e_ctx = jnp.exp(s_ctx - m)
        den = jnp.sum(e_loc, axis=-1, keepdims=True) + jnp.sum(e_ctx, axis=-1, keepdims=True) + jnp.exp(sk - m)
        o = (_dot(e_loc.astype(BF16), v_loc) + _dot(e_ctx.astype(BF16), v_ctx)) * (1.0 / den)
        for p in range(group // 2):
            pair = jnp.where(lo, o[2 * p * blk:(2 * p + 1) * blk], o[(2 * p + 1) * blk:(2 * p + 2) * blk])
            c0 = (hk * group + 2 * p) * HEAD_DIM
            o_ref[:, c0:c0 + LANES] = pair.astype(BF16)


def _attention(geom, sink, q, k, v):
    blk = ATT_BLOCK
    ncb, nsb = geom.Lc // blk, geom.S // blk
    base = geom.n_ctx // blk

    def q_map(b, i):
        return (jnp.where(i < ncb, b * ncb + i, base + b * nsb + i - ncb), 0)

    def loc_map(off):
        def f(b, i):
            return (base + b * nsb + jnp.clip(i - ncb + off, 0, nsb - 1), 0)
        return f

    ctx_map = lambda b, i: (b, 0)
    kv_w = N_KV_HEADS * LANES
    kv_specs = [pl.BlockSpec((blk, kv_w), loc_map(o)) for o in (-1, 0, 1)]
    return pl.pallas_call(
        functools.partial(_att_body, geom),
        grid=(geom.B, ncb + nsb),
        in_specs=[pl.BlockSpec(memory_space=pltpu.SMEM),
                  pl.BlockSpec((blk, ATT_Q), q_map)] + kv_specs + kv_specs
                 + [pl.BlockSpec((geom.Lc, kv_w), ctx_map), pl.BlockSpec((geom.Lc, kv_w), ctx_map)],
        out_specs=pl.BlockSpec((blk, ATT_Q), q_map),
        out_shape=jax.ShapeDtypeStruct(q.shape, BF16),
        compiler_params=_cparams("parallel", "parallel"),
        name="window_attention",
    )(sink, q, k, k, k, v, v, v, k, v)


def _halo_specs(geom, tm, halo, width):
    per = tm // halo
    last = geom.N // halo - 1
    return [pl.BlockSpec((halo, width), lambda i: (jnp.maximum(i * per - 1, 0), 0)),
            pl.BlockSpec((tm, width), lambda i: (i, 0)),
            pl.BlockSpec((halo, width), lambda i: (jnp.minimum((i + 1) * per, last), 0))]


def _fill_padded(geom, pad_ref, prev_ref, x_ref, next_ref, tm, halo):
    first, last = geom.seq_edges(pl.program_id(0), tm)
    pad_ref[0:halo, :] = jnp.where(first, 0.0, prev_ref[...])
    pad_ref[halo:halo + tm, :] = x_ref[...]
    pad_ref[halo + tm:2 * halo + tm, :] = jnp.where(last, 0.0, next_ref[...])


def _mconv_body(geom, tm, prev_ref, x_ref, next_ref, w_ref, q_ref, k_ref, pad_ref):
    halo = SUBLANES
    _fill_padded(geom, pad_ref, prev_ref, x_ref, next_ref, tm, halo)
    acc = None
    for j in range(M_SHORT_CONV):
        off = halo + j - M_SHORT_CONV // 2
        t = pad_ref[off:off + tm, :] * w_ref[j:j + 1, :]
        acc = t if acc is None else acc + t
    y = _silu(acc)
    q_ref[...] = y[:, :M_WIDTH].astype(BF16)
    k_ref[...] = (y[:, M_WIDTH:] * M_HEAD_DIM ** -0.5).astype(BF16)


def _mlstm_conv(geom, qkm, w_mconv):
    n, width = qkm.shape
    tm = _row_tile(geom, 256)
    body = functools.partial(_mconv_body, geom, tm)
    return pl.pallas_call(
        body,
        grid=(n // tm,),
        in_specs=_halo_specs(geom, tm, SUBLANES, width) + [_resident(w_mconv.shape)],
        out_specs=[pl.BlockSpec((tm, M_WIDTH), lambda i: (i, 0))] * 2,
        out_shape=[jax.ShapeDtypeStruct((n, M_WIDTH), BF16)] * 2,
        scratch_shapes=[pltpu.VMEM((tm + 2 * SUBLANES, width), F32)],
        compiler_params=_cparams("parallel"),
        name="mlstm_short_conv",
    )(qkm, qkm, qkm, w_mconv)


def _cconv_body(geom, tm, prev_ref, x_ref, next_ref, w_ref, b_ref, g_ref, bl_ref, o_ref, sh_ref):
    halo = CONV_HALO
    _fill_padded(geom, sh_ref.at[0], prev_ref, x_ref, next_ref, tm, halo)
    span = tm + 2 * halo - SUBLANES
    for s in range(1, SUBLANES):
        sh_ref[s, 0:span, :] = sh_ref[0, s:s + span, :]
    rows = 64
    for r0 in range(0, tm, rows):
        acc = None
        for j in range(CONV_WIDTH):
            a, s = divmod(halo + j - CONV_WIDTH // 2, SUBLANES)
            off = r0 + a * SUBLANES
            win = sh_ref[s, off:off + rows, :].reshape(rows // SUBLANES, SUBLANES, -1)
            t = win * w_ref[j * SUBLANES:(j + 1) * SUBLANES, :][None]
            acc = t if acc is None else acc + t
        y = acc.reshape(rows, -1) + b_ref[...]
        mu = jnp.mean(y, axis=-1, keepdims=True)
        yc = y - mu
        var = jnp.mean(yc * yc, axis=-1, keepdims=True)
        z = yc * lax.rsqrt(var + EPS) * g_ref[...] + bl_ref[...]
        o_ref[r0:r0 + rows, :] = _silu(z).astype(BF16)


def _conformer_conv(geom, u, w_dw, b_dw, g_ln, b_ln):
    n, width = u.shape
    tm = _row_tile(geom, 256)
    body = functools.partial(_cconv_body, geom, tm)
    return pl.pallas_call(
        body,
        grid=(n // tm,),
        in_specs=_halo_specs(geom, tm, CONV_HALO, width)
                 + [_resident(w_dw.shape), _resident((1, width)), _resident((1, width)), _resident((1, width))],
        out_specs=pl.BlockSpec((tm, width), lambda i: (i, 0)),
        out_shape=jax.ShapeDtypeStruct((n, width), BF16),
        scratch_shapes=[pltpu.VMEM((SUBLANES, tm + 2 * CONV_HALO, width), F32)],
        compiler_params=_cparams("parallel"),
        name="conformer_conv",
    )(u, u, u, w_dw, b_dw, g_ln, b_ln)


def _split3(x):
    hi = x.astype(BF16)
    r1 = x - hi.astype(F32)
    mid = r1.astype(BF16)
    lo = (r1 - mid.astype(F32)).astype(BF16)
    return hi, mid, lo


def _mlstm_dir(direction, q_ref, k_ref, vt_ref, g_ref, gt_ref, o_ref, st_ref, m_ref):
    L = M_CHUNK
    r = lax.broadcasted_iota(jnp.int32, (L, L), 0)
    c = lax.broadcasted_iota(jnp.int32, (L, L), 1)
    keep = (r <= c) if direction == 0 else (r >= c)
    keep_t = (c <= r) if direction == 0 else (c >= r)
    tri = jnp.where(keep, 1.0, 0.0).astype(BF16)
    tri_t = jnp.where(keep_t, 1.0, 0.0).astype(BF16)
    gates = g_ref[...]
    gates_t = gt_ref[...]
    hi, mid, lo = _split3(_log_sigmoid(gates_t))
    brow_all = _dot(hi, tri) + _dot(mid, tri) + _dot(lo, tri)
    hi, mid, lo = _split3(_log_sigmoid(gates))
    bcol_all = _dot(tri_t, hi) + _dot(tri_t, mid) + _dot(tri_t, lo)
    end = L - 1 if direction == 0 else 0
    nt = (((1,), (1,)), ((), ()))
    ones = jnp.ones((M_AUG, L), BF16)
    for h in range(M_HEADS):
        ci = direction * M_HEADS + h
        cf = 2 * M_HEADS + ci
        u = direction * M_HEADS + h
        hs = slice(h * M_HEAD_DIM, (h + 1) * M_HEAD_DIM)
        q, k = q_ref[:, hs], k_ref[:, hs]
        vt_aug = jnp.concatenate([vt_ref[hs, :], ones], axis=0)
        b_row = brow_all[cf:cf + 1, :]
        i_row = gates_t[ci:ci + 1, :]
        c_col = gates[:, ci:ci + 1] - bcol_all[:, cf:cf + 1]
        m_prev = m_ref[u:u + 1, 0:1]
        state = st_ref[u]
        dmat = jnp.where(keep, b_row + c_col, NEG_INF)
        a_row = b_row + m_prev
        mt = jnp.maximum(a_row, jnp.max(dmat, axis=0, keepdims=True))
        w_row = jnp.exp(a_row - mt)
        big = lax.dot_general(jnp.concatenate([k, state.astype(BF16)], axis=0), q, nt,
                              preferred_element_type=F32)
        smat = (big[:L] * jnp.exp(dmat - mt)).astype(BF16)
        tot = w_row * big[L:] + _dot(vt_aug, smat)
        den = tot[M_HEAD_DIM:M_HEAD_DIM + 1, :]
        scale = 1.0 / jnp.maximum(jnp.abs(den), jnp.exp(-mt))
        o_ref[:, hs] = (tot[:M_HEAD_DIM] * scale).T
        total = b_row[:, end:end + 1]
        g_row = total - b_row + i_row
        m_new = jnp.maximum(total + m_prev, jnp.max(g_row, axis=1, keepdims=True))
        decay = jnp.exp(total + m_prev - m_new)
        vw = (vt_aug.astype(F32) * jnp.exp(g_row - m_new)).astype(BF16)
        st_ref[u] = decay * state + _dot(vw, k)
        m_ref[u:u + 1, :] = jnp.broadcast_to(m_new, (1, LANES))


def _mlstm_body(qf, kf, vf, gf, gtf, qb, kb, vb, gb, gtb, of, ob, st_ref, m_ref):
    @pl.when(pl.program_id(1) == 0)
    def _():
        st_ref[...] = jnp.zeros_like(st_ref)
        m_ref[...] = jnp.full_like(m_ref, NEG_INF)

    _mlstm_dir(0, qf, kf, vf, gf, gtf, of, st_ref, m_ref)
    _mlstm_dir(1, qb, kb, vb, gb, gtb, ob, st_ref, m_ref)


def _mlstm_scan(geom, q, k, vt, gates, gates_t):
    L = M_CHUNK
    ncc, nsc = geom.Lc // L, geom.S // L
    base = geom.n_ctx // L

    def fwd(b, j):
        return jnp.where(j < ncc, b * ncc + j, base + b * nsc + j - ncc)

    def bwd(b, j):
        return jnp.where(j < ncc, b * ncc + ncc - 1 - j, base + b * nsc + nsc - 1 - (j - ncc))

    def specs(m):
        rows = lambda b, j: (m(b, j), 0)
        cols = lambda b, j: (0, m(b, j))
        return [pl.BlockSpec((L, M_WIDTH), rows), pl.BlockSpec((L, M_WIDTH), rows),
                pl.BlockSpec((M_WIDTH, L), cols), pl.BlockSpec((L, LANES), rows),
                pl.BlockSpec((N_GATE_COLS, L), cols)]

    n = q.shape[0]
    return pl.pallas_call(
        _mlstm_body,
        grid=(geom.B, ncc + nsc),
        in_specs=specs(fwd) + specs(bwd),
        out_specs=[pl.BlockSpec((L, M_WIDTH), lambda b, j: (fwd(b, j), 0)),
                   pl.BlockSpec((L, M_WIDTH), lambda b, j: (bwd(b, j), 0))],
        out_shape=[jax.ShapeDtypeStruct((n, M_WIDTH), F32)] * 2,
        scratch_shapes=[pltpu.VMEM((2 * M_HEADS, M_HEAD_DIM + M_AUG, M_HEAD_DIM), F32),
                        pltpu.VMEM((2 * M_HEADS, LANES), F32)],
        compiler_params=_cparams("parallel", "arbitrary"),
        name="mlstm_scan",
    )(q, k, vt, gates, gates_t, q, k, vt, gates, gates_t)


def _merge_body(x_ref, mod_ref, att_ref, cact_ref, hf_ref, hb_ref, om_ref, br_ref, gm_ref,
                wa_ref, wp_ref, wm_ref, wo_ref, o_ref):
    d = x_ref.shape[1]
    hm = _sigmoid(om_ref[...].astype(F32)) * (hf_ref[...] + hb_ref[...])
    parts = []
    for h in range(M_HEADS):
        t = hm[:, h * M_HEAD_DIM:(h + 1) * M_HEAD_DIM]
        mu = jnp.mean(t, axis=-1, keepdims=True)
        tc = t - mu
        var = jnp.mean(tc * tc, axis=-1, keepdims=True)
        parts.append(tc * lax.rsqrt(var + EPS))
    hn = (jnp.concatenate(parts, axis=-1) * gm_ref[...]).astype(BF16)
    ya = _dot(att_ref[...], wa_ref[...])
    yb = _dot(cact_ref[...], wp_ref[...])
    yc = _dot(hn, wm_ref[...])
    merged = (_sigmoid(br_ref[:, 0:d].astype(F32)) * ya + _sigmoid(br_ref[:, d:2 * d].astype(F32)) * yb
              + _sigmoid(br_ref[:, 2 * d:3 * d].astype(F32)) * yc)
    y = _dot(merged.astype(BF16), wo_ref[...])
    o_ref[...] = x_ref[...] + mod_ref[0][:, 2 * d:3 * d] * y


def _merge(geom, x, mod3, att, cact, hf, hb, om, br, g_mnorm, w):
    n, d = x.shape
    tm = _row_tile(geom, 256)
    row = lambda i: (i, 0)
    mod_map = lambda i: (geom.mod_row(i, tm), 0, 0)
    weights = (w["att_out"], w["pw"], w["mout"], w["out"])
    return pl.pallas_call(
        _merge_body,
        grid=(n // tm,),
        in_specs=[pl.BlockSpec((tm, d), row),
                  pl.BlockSpec((1, 1, mod3.shape[2]), mod_map),
                  pl.BlockSpec((tm, ATT_Q), row), pl.BlockSpec((tm, CONV_DIM), row),
                  pl.BlockSpec((tm, M_WIDTH), row), pl.BlockSpec((tm, M_WIDTH), row),
                  pl.BlockSpec((tm, M_WIDTH), row), pl.BlockSpec((tm, 3 * d), row),
                  _resident((1, M_WIDTH))] + [_resident(a.shape) for a in weights],
        out_specs=pl.BlockSpec((tm, d), row),
        out_shape=jax.ShapeDtypeStruct((n, d), F32),
        compiler_params=_cparams("parallel"),
        name="merge_out",
    )(x, mod3, att, cact, hf, hb, om, br, g_mnorm, *weights)


def _ffn_body(n_chunks, x_ref, mod_ref, g_ref, wg_ref, wu_ref, wd_ref, o_ref):
    d = x_ref.shape[1]
    x = x_ref[...]
    mod = mod_ref[0]
    h = _norm_mod(x, g_ref[...], mod[:, 4 * d:5 * d], mod[:, 3 * d:4 * d]).astype(BF16)
    ff = wg_ref.shape[1]
    cw = ff // n_chunks
    acc = None
    for j in range(n_chunks):
        sl = slice(j * cw, (j + 1) * cw)
        act = (_silu(_dot(h, wg_ref[:, sl])) * _dot(h, wu_ref[:, sl])).astype(BF16)
        t = _dot(act, wd_ref[sl, :])
        acc = t if acc is None else acc + t
    o_ref[...] = x + mod[:, 5 * d:6 * d] * acc


def _ffn(geom, x, mod3, g, w):
    n, d = x.shape
    tm = _row_tile(geom, 512)
    row = lambda i: (i, 0)
    mod_map = lambda i: (geom.mod_row(i, tm), 0, 0)
    ff = w["ff_gate"].shape[1]
    n_chunks = 2 if ff % (2 * LANES) == 0 else 1
    weights = (w["ff_gate"], w["ff_up"], w["ff_down"])
    return pl.pallas_call(
        functools.partial(_ffn_body, n_chunks),
        grid=(n // tm,),
        in_specs=[pl.BlockSpec((tm, d), row),
                  pl.BlockSpec((1, 1, mod3.shape[2]), mod_map),
                  _resident((1, d))] + [_resident(a.shape) for a in weights],
        out_specs=pl.BlockSpec((tm, d), row),
        out_shape=jax.ShapeDtypeStruct((n, d), F32),
        compiler_params=_cparams("parallel"),
        name="swiglu_ffn",
    )(x, mod3, g, *weights)


def _final_body(x_ref, g_ref, o_ref):
    x = x_ref[...]
    ms = jnp.mean(x * x, axis=-1, keepdims=True)
    o_ref[...] = x * lax.rsqrt(ms + EPS) * g_ref[...]


def _final_norm(geom, x, g):
    n, d = x.shape
    tm = _row_tile(geom, 512)
    skip = geom.n_ctx // tm
    return pl.pallas_call(
        _final_body,
        grid=((n - geom.n_ctx) // tm,),
        in_specs=[pl.BlockSpec((tm, d), lambda i: (i + skip, 0)), _resident((1, d))],
        out_specs=pl.BlockSpec((tm, d), lambda i: (i, 0)),
        out_shape=jax.ShapeDtypeStruct((n - geom.n_ctx, d), F32),
        compiler_params=_cparams("parallel"),
        name="final_norm",
    )(x, g)


def _rope_tables(geom):
    p = jnp.arange(geom.S)
    n_freq = HEAD_DIM // 4
    inv_freq = ROPE_BASE ** (-jnp.arange(n_freq, dtype=F32) / n_freq)
    ang_r = (p // GRID_W).astype(F32)[:, None] * inv_freq
    ang_c = (p % GRID_W).astype(F32)[:, None] * inv_freq
    ang = jnp.concatenate([ang_r, ang_r, ang_c, ang_c], axis=-1)
    reps = LANES // HEAD_DIM
    cos = jnp.tile(jnp.cos(ang), (geom.B, reps))
    sin = jnp.tile(jnp.sin(ang), (geom.B, reps))
    first_half = (jnp.arange(LANES) % (HEAD_DIM // 2)) < HEAD_DIM // 4
    sin_a = jnp.where(first_half, -sin, 0.0)
    sin_b = jnp.where(first_half, 0.0, sin)
    ones = jnp.ones((geom.n_ctx, LANES), F32)
    zeros = jnp.zeros((geom.n_ctx, LANES), F32)
    return (jnp.concatenate([ones, cos]), jnp.concatenate([zeros, sin_a]), jnp.concatenate([zeros, sin_b]))


def _pack_layer(l, d, w_in, w_att_out, w_conv_pw, w_mlstm_out, w_out, w_ff_gate, w_ff_up, w_ff_down):
    sizes = (ATT_Q + 2 * ATT_KV, 2 * CONV_DIM, 2 * M_WIDTH, M_WIDTH, M_WIDTH, N_GATE_COLS, 3 * d)
    names = ("qkv", "glu", "qkm", "vm", "om", "gate", "br")
    w = {}
    start = 0
    for name, size in zip(names, sizes):
        w[name] = w_in[l, :, start:start + size].astype(BF16)
        start += size
    w["vgt"] = jnp.concatenate([w.pop("vm"), w["gate"]], axis=1).T
    w["gate"] = jnp.pad(w["gate"], ((0, 0), (0, LANES - N_GATE_COLS)))
    w["att_out"] = w_att_out[l].astype(BF16)
    w["pw"] = w_conv_pw[l].astype(BF16)
    w["mout"] = w_mlstm_out[l].astype(BF16)
    w["out"] = w_out[l].astype(BF16)
    w["ff_gate"] = w_ff_gate[l].astype(BF16)
    w["ff_up"] = w_ff_up[l].astype(BF16)
    w["ff_down"] = w_ff_down[l].astype(BF16)
    return w


def kernel(x, c, ctx, c_ctx, w_ada, b_ada, g_norm_mix, g_norm_ffn, w_in, b_mgate, att_sink, w_att_out, w_conv_dw, b_conv_dw, g_conv_ln, b_conv_ln, w_conv_pw, w_mconv, g_mlstm_norm, w_mlstm_out, w_out, w_ff_gate, w_ff_up, w_ff_down, g_final):
    batch, seq, d = x.shape
    ctx_len = ctx.shape[1]
    depth = w_ada.shape[0]
    geom = _Geom(batch, seq, ctx_len)
    assert seq % ATT_BLOCK == 0 and ctx_len % ATT_BLOCK == 0 and seq % GRID_W == 0
    assert seq % M_CHUNK == 0 and ctx_len % M_CHUNK == 0 and d % LANES == 0

    mod_rows = -(-(batch + 1) // SUBLANES) * SUBLANES
    cvec = jnp.zeros((mod_rows, d), F32).at[:batch].set(c).at[batch].set(c_ctx)
    mod = _modulation(cvec, w_ada, b_ada)
    rope = _rope_tables(geom)
    xs = jnp.concatenate([ctx.reshape(geom.n_ctx, d), x.reshape(batch * seq, d)], axis=0)

    for l in range(depth):
        w = _pack_layer(l, d, w_in, w_att_out, w_conv_pw, w_mlstm_out, w_out, w_ff_gate, w_ff_up, w_ff_down)
        mod3 = mod[l].reshape(mod_rows, 1, 6 * d)
        bgate = jnp.pad(b_mgate[l], (0, LANES - N_GATE_COLS)).reshape(1, LANES)
        q, k, v, u, qkm, vmt, om, gates, gates_t, br = _projection(
            geom, xs, mod3, g_norm_mix[l].reshape(1, d), rope, bgate, w)
        att = _attention(geom, att_sink[l], q, k, v)
        cact = _conformer_conv(geom, u, jnp.repeat(w_conv_dw[l], SUBLANES, axis=0), b_conv_dw[l].reshape(1, -1),
                               g_conv_ln[l].reshape(1, -1), b_conv_ln[l].reshape(1, -1))
        qm, km = _mlstm_conv(geom, qkm, w_mconv[l])
        hf, hb = _mlstm_scan(geom, qm, km, vmt, gates, gates_t)
        xs = _merge(geom, xs, mod3, att, cact, hf, hb, om, br, g_mlstm_norm[l].reshape(1, -1), w)
        xs = _ffn(geom, xs, mod3, g_norm_ffn[l].reshape(1, d), w)

    return _final_norm(geom, xs, g_final.reshape(1, d)).reshape(batch, seq, d)
```

```python
import functools

import jax
import jax.numpy as jnp
from jax import lax
from jax.experimental import pallas as pl
from jax.experimental.pallas import tpu as pltpu

F32 = jnp.float32
BF16 = jnp.bfloat16

GRID_W = 64
N_Q_HEADS = 8
N_KV_HEADS = 2
HEAD_DIM = 64
WINDOW = 128
ATT_BLOCK = 128
ROPE_BASE = 10000.0
ATT_Q = N_Q_HEADS * HEAD_DIM
ATT_KV = N_KV_HEADS * HEAD_DIM
CONV_DIM = 512
CONV_WIDTH = 31
M_HEADS = 4
M_HEAD_DIM = 128
M_WIDTH = M_HEADS * M_HEAD_DIM
M_SHORT_CONV = 3
N_GATE_COLS = 4 * M_HEADS
EPS = 1e-6
NEG_INF = -1e30

LANES = 128
SUBLANES = 8
M_CHUNK = 256
M_AUG = 16
ATT_STEP = 256
CONV_HALO = 16
KV_W = N_KV_HEADS * LANES
VMEM_LIMIT = 52 * 1024 * 1024


def _cparams(*sem):
    return pltpu.CompilerParams(dimension_semantics=sem, vmem_limit_bytes=VMEM_LIMIT)


def _resident(shape):
    nd = len(shape)
    return pl.BlockSpec(shape, lambda *_: (0,) * nd, pipeline_mode=pl.Buffered(1))


def _sigmoid(x):
    return 1.0 / (1.0 + jnp.exp(-x))


def _silu(x):
    return x * _sigmoid(x)


def _log_sigmoid(x):
    return jnp.minimum(x, 0.0) - jnp.log(1.0 + jnp.exp(-jnp.abs(x)))


def _rms(x):
    return x * lax.rsqrt(jnp.mean(x * x, axis=-1, keepdims=True) + EPS)


def _norm_mod(x, g, sc, sh):
    return (_rms(x) * g) * (1.0 + sc) + sh


def _dot(a, b):
    return jnp.dot(a, b, preferred_element_type=F32)


_NT = (((1,), (1,)), ((), ()))


def _mod_body(c_ref, w_ref, b_ref, o_ref):
    h = _silu(c_ref[...]).astype(BF16)
    o_ref[0] = _dot(h, w_ref[0].astype(BF16)) + b_ref[0]


def _modulation(cvec, w_ada, b_ada):
    depth, d, n6 = w_ada.shape
    rows = cvec.shape[0]
    tn = n6 // 4
    return pl.pallas_call(
        _mod_body,
        grid=(depth, n6 // tn),
        in_specs=[pl.BlockSpec((rows, d), lambda l, j: (0, 0)),
                  pl.BlockSpec((1, d, tn), lambda l, j: (l, 0, j)),
                  pl.BlockSpec((1, 1, tn), lambda l, j: (l, 0, j))],
        out_specs=pl.BlockSpec((1, rows, tn), lambda l, j: (l, 0, j)),
        out_shape=jax.ShapeDtypeStruct((depth, rows, n6), F32),
        compiler_params=_cparams("arbitrary", "arbitrary"),
        name="ada_mod",
    )(cvec, w_ada, b_ada.reshape(depth, 1, n6))


class _Geom:
    def __init__(self, batch, seq, ctx_len):
        self.B, self.S, self.Lc = batch, seq, ctx_len
        self.n_ctx = batch * ctx_len
        self.N = self.n_ctx + batch * seq

    def mod_row(self, tile, tm):
        n_ctx_tiles = self.n_ctx // tm
        return jnp.where(tile < n_ctx_tiles, self.B, (tile - n_ctx_tiles) // (self.S // tm))

    def seq_tile(self, tile, tm):
        n_ctx_tiles = self.n_ctx // tm
        tc, ts = self.Lc // tm, self.S // tm
        return jnp.where(tile < n_ctx_tiles, tile % tc, tc + (tile - n_ctx_tiles) % ts)

    def seq_edges(self, tile, tm):
        n_ctx_tiles = self.n_ctx // tm
        tc, ts = self.Lc // tm, self.S // tm
        pos = jnp.where(tile < n_ctx_tiles, tile % tc, (tile - n_ctx_tiles) % ts)
        last = jnp.where(tile < n_ctx_tiles, tc - 1, ts - 1)
        return pos == 0, pos == last


def _row_tile(geom, want):
    tm = want
    while geom.n_ctx % tm or geom.S % tm:
        tm //= 2
    return tm


def _halo_specs(geom, tm, halo, width):
    per = tm // halo
    last = geom.N // halo - 1
    return [pl.BlockSpec((halo, width), lambda i: (jnp.maximum(i * per - 1, 0), 0)),
            pl.BlockSpec((tm, width), lambda i: (i, 0)),
            pl.BlockSpec((halo, width), lambda i: (jnp.minimum((i + 1) * per, last), 0))]


def _rope(x, cos, sin_a, sin_b):
    return x * cos + pltpu.roll(x, LANES - 16, 1) * sin_a + pltpu.roll(x, 16, 1) * sin_b


def _proj_body(geom, tm, xp_ref, x_ref, xn_ref, mod_ref, g_ref, rope_ref, bgt_ref, wc_ref,
               wqkv_ref, wglu_ref, wqkm_ref, wvgt_ref, wom_ref, wbr_ref,
               q_ref, k_ref, vt_ref, u_ref, qm_ref, km_ref, vmt_ref, om_ref, gatet_ref, br_ref, pad_ref):
    d = x_ref.shape[1]
    mod = mod_ref[0]
    g, sc, sh = g_ref[...], mod[:, d:2 * d], mod[:, 0:d]
    h = _norm_mod(x_ref[...], g, sc, sh).astype(BF16)
    cos, sa, sb = rope_ref[:, 0:LANES], rope_ref[:, LANES:2 * LANES], rope_ref[:, 2 * LANES:3 * LANES]

    qkv = _dot(h, wqkv_ref[...])
    scale = HEAD_DIM ** -0.5
    for j in range(ATT_Q // LANES):
        sl = slice(j * LANES, (j + 1) * LANES)
        q_ref[:, sl] = _rope(qkv[:, sl] * scale, cos, sa, sb).astype(BF16)
    lo = lax.broadcasted_iota(jnp.int32, (tm, LANES), 1) < HEAD_DIM
    keys = _rope(qkv[:, ATT_Q:ATT_Q + ATT_KV], cos, sa, sb)
    swapped = pltpu.roll(keys, HEAD_DIM, 1)
    k_ref[:, 0:LANES] = jnp.where(lo, keys, swapped).astype(BF16)
    k_ref[:, LANES:2 * LANES] = jnp.where(lo, swapped, keys).astype(BF16)

    glu = _dot(h, wglu_ref[...])
    u_ref[...] = glu[:, :CONV_DIM] * _sigmoid(glu[:, CONV_DIM:])

    halo = SUBLANES
    h_halo = _norm_mod(jnp.concatenate([xp_ref[...], xn_ref[...]], axis=0), g, sc, sh).astype(BF16)
    qkm = _dot(jnp.concatenate([h, h_halo], axis=0), wqkm_ref[...])
    first, last = geom.seq_edges(pl.program_id(0), tm)
    pad_ref[0:halo, :] = jnp.where(first, 0.0, qkm[tm:tm + halo])
    pad_ref[halo:halo + tm, :] = qkm[0:tm]
    pad_ref[halo + tm:2 * halo + tm, :] = jnp.where(last, 0.0, qkm[tm + halo:tm + 2 * halo])
    acc = None
    for j in range(M_SHORT_CONV):
        off = halo + j - M_SHORT_CONV // 2
        t = pad_ref[off:off + tm, :] * wc_ref[j:j + 1, :]
        acc = t if acc is None else acc + t
    y = _silu(acc)
    qm_ref[...] = y[:, :M_WIDTH].astype(BF16)
    km_ref[...] = (y[:, M_WIDTH:] * M_HEAD_DIM ** -0.5).astype(BF16)

    vgt = lax.dot_general(wvgt_ref[...], h, _NT, preferred_element_type=F32)
    vmt_ref[...] = vgt[:M_WIDTH].astype(BF16)
    vt_ref[...] = vgt[M_WIDTH:M_WIDTH + KV_W].astype(BF16)
    gatet_ref[...] = vgt[M_WIDTH + KV_W:] + bgt_ref[...]
    om_ref[...] = _dot(h, wom_ref[...]).astype(BF16)
    br_ref[...] = _dot(h, wbr_ref[...]).astype(BF16)


def _projection(geom, x, mod3, g, rope, b_gate, w_mconv, w):
    n, d = x.shape
    tm = _row_tile(geom, 256)
    row = lambda i: (i, 0)
    col = lambda i: (0, i)
    mod_map = lambda i: (geom.mod_row(i, tm), 0, 0)
    rope_map = lambda i: (geom.seq_tile(i, tm), 0)
    outs = ((ATT_Q, BF16, False), (KV_W, BF16, False), (KV_W, BF16, True), (CONV_DIM, F32, False),
            (M_WIDTH, BF16, False), (M_WIDTH, BF16, False), (M_WIDTH, BF16, True), (M_WIDTH, BF16, False),
            (N_GATE_COLS, F32, True), (3 * d, BF16, False))
    weights = (w["qkv"], w["glu"], w["qkm"], w["vgt"], w["om"], w["br"])
    return pl.pallas_call(
        functools.partial(_proj_body, geom, tm),
        grid=(n // tm,),
        in_specs=_halo_specs(geom, tm, SUBLANES, d)
                 + [pl.BlockSpec((1, 1, mod3.shape[2]), mod_map),
                    _resident((1, d)),
                    pl.BlockSpec((tm, rope.shape[1]), rope_map),
                    _resident((N_GATE_COLS, 1)), _resident(w_mconv.shape)]
                 + [_resident(a.shape) for a in weights],
        out_specs=[pl.BlockSpec((c, tm), col) if fm else pl.BlockSpec((tm, c), row) for c, _, fm in outs],
        out_shape=[jax.ShapeDtypeStruct((c, n) if fm else (n, c), t) for c, t, fm in outs],
        scratch_shapes=[pltpu.VMEM((tm + 2 * SUBLANES, 2 * M_WIDTH), F32)],
        compiler_params=_cparams("parallel"),
        name="in_proj",
    )(x, x, x, mod3, g, rope, b_gate.reshape(N_GATE_COLS, 1), w_mconv, *weights)


def _att_body(geom, sink_ref, q_ref, kp_ref, kc_ref, kn_ref, kx_ref, vp_ref, vc_ref, vn_ref, vx_ref, o_ref):
    blk = ATT_BLOCK
    nsub = ATT_STEP // blk
    step = pl.program_id(1) - geom.Lc // ATT_STEP
    group = N_Q_HEADS // N_KV_HEADS
    kk = lax.broadcasted_iota(jnp.int32, (3 * blk, blk), 0)
    qq = lax.broadcasted_iota(jnp.int32, (3 * blk, blk), 1)
    lo = lax.broadcasted_iota(jnp.int32, (blk, LANES), 1) < HEAD_DIM
    zero = jnp.zeros((blk, LANES), BF16)
    k_loc = jnp.concatenate([kp_ref[...], kc_ref[...], kn_ref[...]], axis=0)
    vt_loc = jnp.concatenate([vp_ref[...], vc_ref[...], vn_ref[...]], axis=1)
    k_ctx, vt_ctx = kx_ref[...], vx_ref[...]
    for sub in range(nsub):
        n = step * nsub + sub
        key_pos = (n - 1) * blk + kk
        q_pos = n * blk + qq
        valid = (jnp.abs(key_pos - q_pos) <= WINDOW) & (key_pos >= 0) & (key_pos < geom.S) & (n >= 0)
        bias = jnp.where(valid, 0.0, NEG_INF)
        bias = jnp.concatenate([bias] * group, axis=1)
        rows = slice(sub * blk, (sub + 1) * blk)
        win = slice(sub * blk, (sub + 3) * blk)
        for hk in range(N_KV_HEADS):
            ks = slice(hk * LANES, (hk + 1) * LANES)
            tiles = [q_ref[rows, (hk * group + 2 * p) * HEAD_DIM:(hk * group + 2 * p + 2) * HEAD_DIM]
                     for p in range(group // 2)]
            qs = jnp.concatenate([jnp.where(lo if half == 0 else ~lo, t, zero)
                                  for t in tiles for half in (0, 1)], axis=0)
            s = jnp.concatenate(
                [lax.dot_general(k_loc[win, ks], qs, _NT, preferred_element_type=F32) + bias,
                 lax.dot_general(k_ctx[:, ks], qs, _NT, preferred_element_type=F32)], axis=0)
            sk = jnp.concatenate(
                [jnp.full((1, blk), sink_ref[hk * group + j], F32) for j in range(group)], axis=1)
            m = jnp.maximum(jnp.max(s, axis=0, keepdims=True), sk)
            e = jnp.exp(s - m)
            den = jnp.sum(e, axis=0, keepdims=True) + jnp.exp(sk - m)
            vt = jnp.concatenate([vt_loc[ks, win], vt_ctx[ks, :]], axis=1)
            o = (_dot(vt, e.astype(BF16)) * (1.0 / den)).T
            for p in range(group // 2):
                pair = jnp.where(lo, o[2 * p * blk:(2 * p + 1) * blk], o[(2 * p + 1) * blk:(2 * p + 2) * blk])
                c0 = (hk * group + 2 * p) * HEAD_DIM
                o_ref[rows, c0:c0 + LANES] = pair.astype(BF16)


def _attention(geom, sink, q, k, vt):
    blk, stp = ATT_BLOCK, ATT_STEP
    nsub = stp // blk
    ncs, nss = geom.Lc // stp, geom.S // stp
    nsb = geom.S // blk
    base_s, base_b = geom.n_ctx // stp, geom.n_ctx // blk

    def q_blk(b, i):
        return jnp.where(i < ncs, b * ncs + i, base_s + b * nss + i - ncs)

    def edge_blk(off):
        return lambda b, i: base_b + b * nsb + jnp.clip((i - ncs) * nsub + off, 0, nsb - 1)

    cur_blk = lambda b, i: base_s + b * nss + jnp.clip(i - ncs, 0, nss - 1)
    ctx_blk = lambda b, i: b
    blocks = ((blk, edge_blk(-1)), (stp, cur_blk), (blk, edge_blk(nsub)), (geom.Lc, ctx_blk))
    rows = lambda f: (lambda b, i: (f(b, i), 0))
    cols = lambda f: (lambda b, i: (0, f(b, i)))
    return pl.pallas_call(
        functools.partial(_att_body, geom),
        grid=(geom.B, ncs + nss),
        in_specs=[pl.BlockSpec(memory_space=pltpu.SMEM),
                  pl.BlockSpec((stp, ATT_Q), rows(q_blk))]
                 + [pl.BlockSpec((size, KV_W), rows(f)) for size, f in blocks]
                 + [pl.BlockSpec((KV_W, size), cols(f)) for size, f in blocks],
        out_specs=pl.BlockSpec((stp, ATT_Q), rows(q_blk)),
        out_shape=jax.ShapeDtypeStruct(q.shape, BF16),
        compiler_params=_cparams("parallel", "parallel"),
        name="window_attention",
    )(sink, q, k, k, k, k, vt, vt, vt, vt)


def _cconv_body(geom, tm, prev_ref, x_ref, next_ref, w_ref, b_ref, g_ref, bl_ref, o_ref, sh_ref):
    halo = CONV_HALO
    first, last = geom.seq_edges(pl.program_id(0), tm)
    sh_ref[0, 0:halo, :] = jnp.where(first, 0.0, prev_ref[...])
    sh_ref[0, halo:halo + tm, :] = x_ref[...]
    sh_ref[0, halo + tm:2 * halo + tm, :] = jnp.where(last, 0.0, next_ref[...])
    span = tm + 2 * halo - SUBLANES
    for s in range(1, SUBLANES):
        sh_ref[s, 0:span, :] = sh_ref[0, s:s + span, :]
    rows = 64
    for r0 in range(0, tm, rows):
        acc = None
        for j in range(CONV_WIDTH):
            a, s = divmod(halo + j - CONV_WIDTH // 2, SUBLANES)
            off = r0 + a * SUBLANES
            win = sh_ref[s, off:off + rows, :].reshape(rows // SUBLANES, SUBLANES, -1)
            t = win * w_ref[j * SUBLANES:(j + 1) * SUBLANES, :][None]
            acc = t if acc is None else acc + t
        y = acc.reshape(rows, -1) + b_ref[...]
        mu = jnp.mean(y, axis=-1, keepdims=True)
        yc = y - mu
        var = jnp.mean(yc * yc, axis=-1, keepdims=True)
        z = yc * lax.rsqrt(var + EPS) * g_ref[...] + bl_ref[...]
        o_ref[r0:r0 + rows, :] = _silu(z).astype(BF16)


def _conformer_conv(geom, u, w_dw, b_dw, g_ln, b_ln):
    n, width = u.shape
    tm = _row_tile(geom, 256)
    body = functools.partial(_cconv_body, geom, tm)
    return pl.pallas_call(
        body,
        grid=(n // tm,),
        in_specs=_halo_specs(geom, tm, CONV_HALO, width)
                 + [_resident(w_dw.shape), _resident((1, width)), _resident((1, width)), _resident((1, width))],
        out_specs=pl.BlockSpec((tm, width), lambda i: (i, 0)),
        out_shape=jax.ShapeDtypeStruct((n, width), BF16),
        scratch_shapes=[pltpu.VMEM((SUBLANES, tm + 2 * CONV_HALO, width), F32)],
        compiler_params=_cparams("parallel"),
        name="conformer_conv",
    )(u, u, u, w_dw, b_dw, g_ln, b_ln)


def _split3(x):
    hi = x.astype(BF16)
    r1 = x - hi.astype(F32)
    mid = r1.astype(BF16)
    lo = (r1 - mid.astype(F32)).astype(BF16)
    return hi, mid, lo


def _mlstm_dir(direction, q_ref, k_ref, vt_ref, gt_ref, o_ref, st_ref, m_ref):
    L = M_CHUNK
    r = lax.broadcasted_iota(jnp.int32, (L, L), 0)
    c = lax.broadcasted_iota(jnp.int32, (L, L), 1)
    keep = (r <= c) if direction == 0 else (r >= c)
    tri = jnp.where(keep, 1.0, 0.0).astype(BF16)
    gates_t = gt_ref[...]
    hi, mid, lo = _split3(_log_sigmoid(gates_t))
    brow_all = _dot(hi, tri) + _dot(mid, tri) + _dot(lo, tri)
    u0 = direction * M_HEADS
    c_rows = gates_t[u0:u0 + M_HEADS] - brow_all[2 * M_HEADS + u0:2 * M_HEADS + u0 + M_HEADS]
    c_cols = jnp.concatenate([c_rows, jnp.zeros((LANES - M_HEADS, L), F32)], axis=0).T
    end = L - 1 if direction == 0 else 0
    ones = jnp.ones((M_AUG, L), BF16)
    for h in range(M_HEADS):
        ci = direction * M_HEADS + h
        cf = 2 * M_HEADS + ci
        u = direction * M_HEADS + h
        hs = slice(h * M_HEAD_DIM, (h + 1) * M_HEAD_DIM)
        q, k = q_ref[:, hs], k_ref[:, hs]
        vt_aug = jnp.concatenate([vt_ref[hs, :], ones], axis=0)
        b_row = brow_all[cf:cf + 1, :]
        i_row = gates_t[ci:ci + 1, :]
        c_col = c_cols[:, h:h + 1]
        m_prev = m_ref[u:u + 1, 0:1]
        state = st_ref[u]
        dmat = jnp.where(keep, b_row + c_col, NEG_INF)
        a_row = b_row + m_prev
        mt = jnp.maximum(a_row, jnp.max(dmat, axis=0, keepdims=True))
        w_row = jnp.exp(a_row - mt)
        big = lax.dot_general(jnp.concatenate([k, state.astype(BF16)], axis=0), q, _NT,
                              preferred_element_type=F32)
        smat = (big[:L] * jnp.exp(dmat - mt)).astype(BF16)
        tot = w_row * big[L:] + _dot(vt_aug, smat)
        den = tot[M_HEAD_DIM:M_HEAD_DIM + 1, :]
        scale = 1.0 / jnp.maximum(jnp.abs(den), jnp.exp(-mt))
        o_ref[:, hs] = (tot[:M_HEAD_DIM] * scale).T
        total = b_row[:, end:end + 1]
        g_row = total - b_row + i_row
        m_new = jnp.maximum(total + m_prev, jnp.max(g_row, axis=1, keepdims=True))
        decay = jnp.exp(total + m_prev - m_new)
        vw = (vt_aug.astype(F32) * jnp.exp(g_row - m_new)).astype(BF16)
        st_ref[u] = decay * state + _dot(vw, k)
        m_ref[u:u + 1, :] = jnp.broadcast_to(m_new, (1, LANES))


def _mlstm_body(qf, kf, vf, gtf, qb, kb, vb, gtb, of, ob, st_ref, m_ref):
    @pl.when(pl.program_id(1) == 0)
    def _():
        st_ref[...] = jnp.zeros_like(st_ref)
        m_ref[...] = jnp.full_like(m_ref, NEG_INF)

    _mlstm_dir(0, qf, kf, vf, gtf, of, st_ref, m_ref)
    _mlstm_dir(1, qb, kb, vb, gtb, ob, st_ref, m_ref)


def _mlstm_scan(geom, q, k, vt, gates_t):
    L = M_CHUNK
    ncc, nsc = geom.Lc // L, geom.S // L
    base = geom.n_ctx // L

    def fwd(b, j):
        return jnp.where(j < ncc, b * ncc + j, base + b * nsc + j - ncc)

    def bwd(b, j):
        return jnp.where(j < ncc, b * ncc + ncc - 1 - j, base + b * nsc + nsc - 1 - (j - ncc))

    def specs(m):
        rows = lambda b, j: (m(b, j), 0)
        cols = lambda b, j: (0, m(b, j))
        return [pl.BlockSpec((L, M_WIDTH), rows), pl.BlockSpec((L, M_WIDTH), rows),
                pl.BlockSpec((M_WIDTH, L), cols), pl.BlockSpec((N_GATE_COLS, L), cols)]

    n = q.shape[0]
    return pl.pallas_call(
        _mlstm_body,
        grid=(geom.B, ncc + nsc),
        in_specs=specs(fwd) + specs(bwd),
        out_specs=[pl.BlockSpec((L, M_WIDTH), lambda b, j: (fwd(b, j), 0)),
                   pl.BlockSpec((L, M_WIDTH), lambda b, j: (bwd(b, j), 0))],
        out_shape=[jax.ShapeDtypeStruct((n, M_WIDTH), F32)] * 2,
        scratch_shapes=[pltpu.VMEM((2 * M_HEADS, M_HEAD_DIM + M_AUG, M_HEAD_DIM), F32),
                        pltpu.VMEM((2 * M_HEADS, LANES), F32)],
        compiler_params=_cparams("parallel", "arbitrary"),
        name="mlstm_scan",
    )(q, k, vt, gates_t, q, k, vt, gates_t)


def _merge_body(x_ref, mod_ref, att_ref, cact_ref, hf_ref, hb_ref, om_ref, br_ref, gm_ref,
                wa_ref, wp_ref, wm_ref, wo_ref, o_ref):
    d = x_ref.shape[1]
    hm = _sigmoid(om_ref[...].astype(F32)) * (hf_ref[...] + hb_ref[...])
    parts = []
    for h in range(M_HEADS):
        t = hm[:, h * M_HEAD_DIM:(h + 1) * M_HEAD_DIM]
        mu = jnp.mean(t, axis=-1, keepdims=True)
        tc = t - mu
        var = jnp.mean(tc * tc, axis=-1, keepdims=True)
        parts.append(tc * lax.rsqrt(var + EPS))
    hn = (jnp.concatenate(parts, axis=-1) * gm_ref[...]).astype(BF16)
    ya = _dot(att_ref[...], wa_ref[...])
    yb = _dot(cact_ref[...], wp_ref[...])
    yc = _dot(hn, wm_ref[...])
    merged = (_sigmoid(br_ref[:, 0:d].astype(F32)) * ya + _sigmoid(br_ref[:, d:2 * d].astype(F32)) * yb
              + _sigmoid(br_ref[:, 2 * d:3 * d].astype(F32)) * yc)
    y = _dot(merged.astype(BF16), wo_ref[...])
    o_ref[...] = x_ref[...] + mod_ref[0][:, 2 * d:3 * d] * y


def _merge(geom, x, mod3, att, cact, hf, hb, om, br, g_mnorm, w, latent_only):
    n, d = x.shape
    tm = _row_tile(geom, 256)
    skip = geom.n_ctx // tm if latent_only else 0
    row = lambda i: (i + skip, 0)
    mod_map = lambda i: (geom.mod_row(i + skip, tm), 0, 0)
    weights = (w["att_out"], w["pw"], w["mout"], w["out"])
    return pl.pallas_call(
        _merge_body,
        grid=(n // tm - skip,),
        in_specs=[pl.BlockSpec((tm, d), row),
                  pl.BlockSpec((1, 1, mod3.shape[2]), mod_map),
                  pl.BlockSpec((tm, ATT_Q), row), pl.BlockSpec((tm, CONV_DIM), row),
                  pl.BlockSpec((tm, M_WIDTH), row), pl.BlockSpec((tm, M_WIDTH), row),
                  pl.BlockSpec((tm, M_WIDTH), row), pl.BlockSpec((tm, 3 * d), row),
                  _resident((1, M_WIDTH))] + [_resident(a.shape) for a in weights],
        out_specs=pl.BlockSpec((tm, d), lambda i: (i, 0)),
        out_shape=jax.ShapeDtypeStruct((n - skip * tm, d), F32),
        compiler_params=_cparams("parallel"),
        name="merge_out",
    )(x, mod3, att, cact, hf, hb, om, br, g_mnorm, *weights)


def _ffn_body(n_chunks, final, x_ref, mod_ref, g_ref, gf_ref, wg_ref, wu_ref, wd_ref, o_ref):
    d = x_ref.shape[1]
    x = x_ref[...]
    mod = mod_ref[0]
    h = _norm_mod(x, g_ref[...], mod[:, 4 * d:5 * d], mod[:, 3 * d:4 * d]).astype(BF16)
    ff = wg_ref.shape[1]
    cw = ff // n_chunks
    acc = None
    for j in range(n_chunks):
        sl = slice(j * cw, (j + 1) * cw)
        act = (_silu(_dot(h, wg_ref[:, sl])) * _dot(h, wu_ref[:, sl])).astype(BF16)
        t = _dot(act, wd_ref[sl, :])
        acc = t if acc is None else acc + t
    y = x + mod[:, 5 * d:6 * d] * acc
    o_ref[...] = _rms(y) * gf_ref[...] if final else y


def _ffn(geom, x, mod3, g, g_final, w, latent_only):
    n, d = x.shape
    tm = _row_tile(geom, 512)
    skip = geom.n_ctx // tm if latent_only else 0
    row = lambda i: (i, 0)
    mod_map = lambda i: (geom.mod_row(i + skip, tm), 0, 0)
    ff = w["ff_gate"].shape[1]
    n_chunks = 2 if ff % (2 * LANES) == 0 else 1
    weights = (w["ff_gate"], w["ff_up"], w["ff_down"])
    return pl.pallas_call(
        functools.partial(_ffn_body, n_chunks, latent_only),
        grid=(n // tm,),
        in_specs=[pl.BlockSpec((tm, d), row),
                  pl.BlockSpec((1, 1, mod3.shape[2]), mod_map),
                  _resident((1, d)), _resident((1, d))] + [_resident(a.shape) for a in weights],
        out_specs=pl.BlockSpec((tm, d), row),
        out_shape=jax.ShapeDtypeStruct((n, d), F32),
        compiler_params=_cparams("parallel"),
        name="swiglu_ffn",
    )(x, mod3, g, g_final, *weights)


def _rope_table(geom):
    p = jnp.arange(geom.S)
    n_freq = HEAD_DIM // 4
    inv_freq = ROPE_BASE ** (-jnp.arange(n_freq, dtype=F32) / n_freq)
    ang_r = (p // GRID_W).astype(F32)[:, None] * inv_freq
    ang_c = (p % GRID_W).astype(F32)[:, None] * inv_freq
    ang = jnp.concatenate([ang_r, ang_r, ang_c, ang_c], axis=-1)
    reps = LANES // HEAD_DIM
    cos = jnp.tile(jnp.cos(ang), (1, reps))
    sin = jnp.tile(jnp.sin(ang), (1, reps))
    first_half = (jnp.arange(LANES) % (HEAD_DIM // 2)) < HEAD_DIM // 4
    lat = jnp.concatenate([cos, jnp.where(first_half, -sin, 0.0), jnp.where(first_half, 0.0, sin)], axis=1)
    ident = jnp.concatenate([jnp.ones((geom.Lc, LANES), F32), jnp.zeros((geom.Lc, 2 * LANES), F32)], axis=1)
    return jnp.concatenate([ident, lat], axis=0)


def _pack_layer(l, d, w_in, w_att_out, w_conv_pw, w_mlstm_out, w_out, w_ff_gate, w_ff_up, w_ff_down):
    sizes = (ATT_Q + 2 * ATT_KV, 2 * CONV_DIM, 2 * M_WIDTH, M_WIDTH, M_WIDTH, N_GATE_COLS, 3 * d)
    names = ("qkv", "glu", "qkm", "vm", "om", "gate", "br")
    w = {}
    start = 0
    for name, size in zip(names, sizes):
        w[name] = w_in[l, :, start:start + size].astype(BF16)
        start += size
    v_cols = w["qkv"][:, ATT_Q + ATT_KV:].reshape(d, N_KV_HEADS, 1, HEAD_DIM)
    v_dup = jnp.broadcast_to(v_cols, (d, N_KV_HEADS, 2, HEAD_DIM)).reshape(d, KV_W)
    w["vgt"] = jnp.concatenate([w.pop("vm"), v_dup, w["gate"]], axis=1).T
    w["qkv"] = w["qkv"][:, :ATT_Q + ATT_KV]
    del w["gate"]
    w["att_out"] = w_att_out[l].astype(BF16)
    w["pw"] = w_conv_pw[l].astype(BF16)
    w["mout"] = w_mlstm_out[l].astype(BF16)
    w["out"] = w_out[l].astype(BF16)
    w["ff_gate"] = w_ff_gate[l].astype(BF16)
    w["ff_up"] = w_ff_up[l].astype(BF16)
    w["ff_down"] = w_ff_down[l].astype(BF16)
    return w


def kernel(x, c, ctx, c_ctx, w_ada, b_ada, g_norm_mix, g_norm_ffn, w_in, b_mgate, att_sink, w_att_out, w_conv_dw, b_conv_dw, g_conv_ln, b_conv_ln, w_conv_pw, w_mconv, g_mlstm_norm, w_mlstm_out, w_out, w_ff_gate, w_ff_up, w_ff_down, g_final):
    batch, seq, d = x.shape
    ctx_len = ctx.shape[1]
    depth = w_ada.shape[0]
    geom = _Geom(batch, seq, ctx_len)
    assert seq % ATT_STEP == 0 and ctx_len % ATT_STEP == 0 and seq % GRID_W == 0
    assert seq % M_CHUNK == 0 and ctx_len % M_CHUNK == 0 and d % LANES == 0
    assert LANES == 2 * HEAD_DIM and ATT_KV == LANES

    mod_rows = -(-(batch + 1) // SUBLANES) * SUBLANES
    cvec = jnp.zeros((mod_rows, d), F32).at[:batch].set(c).at[batch].set(c_ctx)
    mod = _modulation(cvec, w_ada, b_ada)
    rope = _rope_table(geom)
    xs = jnp.concatenate([ctx.reshape(geom.n_ctx, d), x.reshape(batch * seq, d)], axis=0)
    g_fin = g_final.reshape(1, d)

    for l in range(depth):
        last = l == depth - 1
        w = _pack_layer(l, d, w_in, w_att_out, w_conv_pw, w_mlstm_out, w_out, w_ff_gate, w_ff_up, w_ff_down)
        mod3 = mod[l].reshape(mod_rows, 1, 6 * d)
        q, k, vt, u, qm, km, vmt, om, gates_t, br = _projection(
            geom, xs, mod3, g_norm_mix[l].reshape(1, d), rope, b_mgate[l], w_mconv[l], w)
        att = _attention(geom, att_sink[l], q, k, vt)
        cact = _conformer_conv(geom, u, jnp.repeat(w_conv_dw[l], SUBLANES, axis=0), b_conv_dw[l].reshape(1, -1),
                               g_conv_ln[l].reshape(1, -1), b_conv_ln[l].reshape(1, -1))
        hf, hb = _mlstm_scan(geom, qm, km, vmt, gates_t)
        xs = _merge(geom, xs, mod3, att, cact, hf, hb, om, br, g_mlstm_norm[l].reshape(1, -1), w, last)
        xs = _ffn(geom, xs, mod3, g_norm_ffn[l].reshape(1, d), g_fin, w, last)

    return xs.reshape(batch, seq, d)
```

```python
import functools

import jax
import jax.numpy as jnp
from jax import lax
from jax.experimental import pallas as pl
from jax.experimental.pallas import tpu as pltpu

F32 = jnp.float32
BF16 = jnp.bfloat16

GRID_W = 64
N_Q_HEADS = 8
N_KV_HEADS = 2
HEAD_DIM = 64
WINDOW = 128
ATT_BLOCK = 128
ROPE_BASE = 10000.0
ATT_Q = N_Q_HEADS * HEAD_DIM
ATT_KV = N_KV_HEADS * HEAD_DIM
CONV_DIM = 512
CONV_WIDTH = 31
M_HEADS = 4
M_HEAD_DIM = 128
M_WIDTH = M_HEADS * M_HEAD_DIM
M_SHORT_CONV = 3
N_GATE_COLS = 4 * M_HEADS
EPS = 1e-6
NEG_INF = -1e30

LANES = 128
SUBLANES = 8
M_CHUNK = 256
M_AUG = 16
ATT_STEP = 256
CONV_HALO = 16
KV_W = N_KV_HEADS * LANES
VMEM_LIMIT = 52 * 1024 * 1024


def _cparams(*sem):
    return pltpu.CompilerParams(dimension_semantics=sem, vmem_limit_bytes=VMEM_LIMIT)


def _resident(shape):
    nd = len(shape)
    return pl.BlockSpec(shape, lambda *_: (0,) * nd, pipeline_mode=pl.Buffered(1))


def _sigmoid(x):
    return 1.0 / (1.0 + jnp.exp(-x))


def _silu(x):
    return x * _sigmoid(x)


def _log_sigmoid(x):
    return jnp.minimum(x, 0.0) - jnp.log(1.0 + jnp.exp(-jnp.abs(x)))


def _rms(x):
    return x * lax.rsqrt(jnp.mean(x * x, axis=-1, keepdims=True) + EPS)


def _norm_mod(x, g, sc, sh):
    return (_rms(x) * g) * (1.0 + sc) + sh


def _dot(a, b):
    return jnp.dot(a, b, preferred_element_type=F32)


_NT = (((1,), (1,)), ((), ()))


def _mod_body(c_ref, w_ref, b_ref, o_ref):
    h = _silu(c_ref[...]).astype(BF16)
    o_ref[0] = _dot(h, w_ref[0].astype(BF16)) + b_ref[0]


def _modulation(cvec, w_ada, b_ada):
    depth, d, n6 = w_ada.shape
    rows = cvec.shape[0]
    tn = n6 // 4
    return pl.pallas_call(
        _mod_body,
        grid=(depth, n6 // tn),
        in_specs=[pl.BlockSpec((rows, d), lambda l, j: (0, 0)),
                  pl.BlockSpec((1, d, tn), lambda l, j: (l, 0, j)),
                  pl.BlockSpec((1, 1, tn), lambda l, j: (l, 0, j))],
        out_specs=pl.BlockSpec((1, rows, tn), lambda l, j: (l, 0, j)),
        out_shape=jax.ShapeDtypeStruct((depth, rows, n6), F32),
        compiler_params=_cparams("arbitrary", "arbitrary"),
        name="ada_mod",
    )(cvec, w_ada, b_ada.reshape(depth, 1, n6))


class _Geom:
    def __init__(self, batch, seq, ctx_len):
        self.B, self.S, self.Lc = batch, seq, ctx_len
        self.n_ctx = batch * ctx_len
        self.N = self.n_ctx + batch * seq

    def mod_row(self, tile, tm):
        n_ctx_tiles = self.n_ctx // tm
        return jnp.where(tile < n_ctx_tiles, self.B, (tile - n_ctx_tiles) // (self.S // tm))

    def seq_tile(self, tile, tm):
        n_ctx_tiles = self.n_ctx // tm
        tc, ts = self.Lc // tm, self.S // tm
        return jnp.where(tile < n_ctx_tiles, tile % tc, tc + (tile - n_ctx_tiles) % ts)

    def seq_edges(self, tile, tm):
        n_ctx_tiles = self.n_ctx // tm
        tc, ts = self.Lc // tm, self.S // tm
        pos = jnp.where(tile < n_ctx_tiles, tile % tc, (tile - n_ctx_tiles) % ts)
        last = jnp.where(tile < n_ctx_tiles, tc - 1, ts - 1)
        return pos == 0, pos == last


def _row_tile(geom, want):
    tm = want
    while geom.n_ctx % tm or geom.S % tm:
        tm //= 2
    return tm


def _halo_specs(geom, tm, halo, width):
    per = tm // halo
    last = geom.N // halo - 1
    return [pl.BlockSpec((halo, width), lambda i: (jnp.maximum(i * per - 1, 0), 0)),
            pl.BlockSpec((tm, width), lambda i: (i, 0)),
            pl.BlockSpec((halo, width), lambda i: (jnp.minimum((i + 1) * per, last), 0))]


def _assemble_body(n_ctx_tiles, ctx_ref, x_ref, o_ref):
    o_ref[...] = jnp.where(pl.program_id(0) < n_ctx_tiles, ctx_ref[...], x_ref[...])


def _assemble(geom, ctx2, x2):
    d = x2.shape[1]
    tm = _row_tile(geom, 1024)
    nct = geom.n_ctx // tm
    return pl.pallas_call(
        functools.partial(_assemble_body, nct),
        grid=(geom.N // tm,),
        in_specs=[pl.BlockSpec((tm, d), lambda i: (jnp.minimum(i, nct - 1), 0)),
                  pl.BlockSpec((tm, d), lambda i: (jnp.maximum(i - nct, 0), 0))],
        out_specs=pl.BlockSpec((tm, d), lambda i: (i, 0)),
        out_shape=jax.ShapeDtypeStruct((geom.N, d), F32),
        compiler_params=_cparams("arbitrary"),
        name="assemble_tokens",
    )(ctx2, x2)


def _rope(x, cos, sin_a, sin_b):
    return x * cos + pltpu.roll(x, LANES - 16, 1) * sin_a + pltpu.roll(x, 16, 1) * sin_b


def _proj_body(geom, tm, xp_ref, x_ref, xn_ref, mod_ref, g_ref, rope_ref, bgt_ref, wc_ref,
               wqkv_ref, wglu_ref, wqkm_ref, wvgt_ref, wom_ref, wbr_ref,
               q_ref, k_ref, vt_ref, u_ref, qm_ref, km_ref, vmt_ref, om_ref, gatet_ref, br_ref, pad_ref):
    d = x_ref.shape[1]
    mod = mod_ref[0]
    g, sc, sh = g_ref[...], mod[:, d:2 * d], mod[:, 0:d]
    h = _norm_mod(x_ref[...], g, sc, sh).astype(BF16)
    cos, sa, sb = rope_ref[:, 0:LANES], rope_ref[:, LANES:2 * LANES], rope_ref[:, 2 * LANES:3 * LANES]

    qkv = _dot(h, wqkv_ref[...])
    scale = HEAD_DIM ** -0.5
    for j in range(ATT_Q // LANES):
        sl = slice(j * LANES, (j + 1) * LANES)
        q_ref[:, sl] = _rope(qkv[:, sl] * scale, cos, sa, sb).astype(BF16)
    lo = lax.broadcasted_iota(jnp.int32, (tm, LANES), 1) < HEAD_DIM
    keys = _rope(qkv[:, ATT_Q:ATT_Q + ATT_KV], cos, sa, sb)
    swapped = pltpu.roll(keys, HEAD_DIM, 1)
    k_ref[:, 0:LANES] = jnp.where(lo, keys, swapped).astype(BF16)
    k_ref[:, LANES:2 * LANES] = jnp.where(lo, swapped, keys).astype(BF16)

    glu = _dot(h, wglu_ref[...])
    u_ref[...] = glu[:, :CONV_DIM] * _sigmoid(glu[:, CONV_DIM:])

    halo = SUBLANES
    h_halo = _norm_mod(jnp.concatenate([xp_ref[...], xn_ref[...]], axis=0), g, sc, sh).astype(BF16)
    qkm = _dot(jnp.concatenate([h, h_halo], axis=0), wqkm_ref[...])
    first, last = geom.seq_edges(pl.program_id(0), tm)
    pad_ref[0:halo, :] = jnp.where(first, 0.0, qkm[tm:tm + halo])
    pad_ref[halo:halo + tm, :] = qkm[0:tm]
    pad_ref[halo + tm:2 * halo + tm, :] = jnp.where(last, 0.0, qkm[tm + halo:tm + 2 * halo])
    acc = None
    for j in range(M_SHORT_CONV):
        off = halo + j - M_SHORT_CONV // 2
        t = pad_ref[off:off + tm, :] * wc_ref[j:j + 1, :]
        acc = t if acc is None else acc + t
    y = _silu(acc)
    qm_ref[...] = y[:, :M_WIDTH].astype(BF16)
    km_ref[...] = (y[:, M_WIDTH:] * M_HEAD_DIM ** -0.5).astype(BF16)

    vgt = lax.dot_general(wvgt_ref[...], h, _NT, preferred_element_type=F32)
    vmt_ref[...] = vgt[:M_WIDTH].astype(BF16)
    vt_ref[...] = vgt[M_WIDTH:M_WIDTH + KV_W].astype(BF16)
    gatet_ref[...] = vgt[M_WIDTH + KV_W:] + bgt_ref[...]
    om_ref[...] = _dot(h, wom_ref[...]).astype(BF16)
    br_ref[...] = _dot(h, wbr_ref[...]).astype(BF16)


def _projection(geom, x, mod3, g, rope, b_gate, w_mconv, w):
    n, d = x.shape
    tm = _row_tile(geom, 256)
    row = lambda i: (i, 0)
    col = lambda i: (0, i)
    mod_map = lambda i: (geom.mod_row(i, tm), 0, 0)
    rope_map = lambda i: (geom.seq_tile(i, tm), 0)
    outs = ((ATT_Q, BF16, False), (KV_W, BF16, False), (KV_W, BF16, True), (CONV_DIM, F32, False),
            (M_WIDTH, BF16, False), (M_WIDTH, BF16, False), (M_WIDTH, BF16, True), (M_WIDTH, BF16, False),
            (N_GATE_COLS, F32, True), (3 * d, BF16, False))
    weights = (w["qkv"], w["glu"], w["qkm"], w["vgt"], w["om"], w["br"])
    return pl.pallas_call(
        functools.partial(_proj_body, geom, tm),
        grid=(n // tm,),
        in_specs=_halo_specs(geom, tm, SUBLANES, d)
                 + [pl.BlockSpec((1, 1, mod3.shape[2]), mod_map),
                    _resident((1, d)),
                    pl.BlockSpec((tm, rope.shape[1]), rope_map),
                    _resident((N_GATE_COLS, 1)), _resident(w_mconv.shape)]
                 + [_resident(a.shape) for a in weights],
        out_specs=[pl.BlockSpec((c, tm), col) if fm else pl.BlockSpec((tm, c), row) for c, _, fm in outs],
        out_shape=[jax.ShapeDtypeStruct((c, n) if fm else (n, c), t) for c, t, fm in outs],
        scratch_shapes=[pltpu.VMEM((tm + 2 * SUBLANES, 2 * M_WIDTH), F32)],
        compiler_params=_cparams("parallel"),
        name="in_proj",
    )(x, x, x, mod3, g, rope, b_gate.reshape(N_GATE_COLS, 1), w_mconv, *weights)


def _att_body(geom, sink_ref, q_ref, kp_ref, kc_ref, kn_ref, kx_ref, vp_ref, vc_ref, vn_ref, vx_ref, o_ref):
    blk = ATT_BLOCK
    nsub = ATT_STEP // blk
    step = pl.program_id(1) - geom.Lc // ATT_STEP
    group = N_Q_HEADS // N_KV_HEADS
    kk = lax.broadcasted_iota(jnp.int32, (3 * blk, blk), 0)
    qq = lax.broadcasted_iota(jnp.int32, (3 * blk, blk), 1)
    lo = lax.broadcasted_iota(jnp.int32, (blk, LANES), 1) < HEAD_DIM
    zero = jnp.zeros((blk, LANES), BF16)
    k_loc = jnp.concatenate([kp_ref[...], kc_ref[...], kn_ref[...]], axis=0)
    vt_loc = jnp.concatenate([vp_ref[...], vc_ref[...], vn_ref[...]], axis=1)
    k_ctx, vt_ctx = kx_ref[...], vx_ref[...]
    chains = []
    for sub in range(nsub):
        n = step * nsub + sub
        key_pos = (n - 1) * blk + kk
        q_pos = n * blk + qq
        valid = (jnp.abs(key_pos - q_pos) <= WINDOW) & (key_pos >= 0) & (key_pos < geom.S) & (n >= 0)
        bias = jnp.where(valid, 0.0, NEG_INF)
        bias = jnp.concatenate([bias] * group, axis=1)
        rows = slice(sub * blk, (sub + 1) * blk)
        win = slice(sub * blk, (sub + 3) * blk)
        for hk in range(N_KV_HEADS):
            ks = slice(hk * LANES, (hk + 1) * LANES)
            tiles = [q_ref[rows, (hk * group + 2 * p) * HEAD_DIM:(hk * group + 2 * p + 2) * HEAD_DIM]
                     for p in range(group // 2)]
            qs = jnp.concatenate([jnp.where(lo if half == 0 else ~lo, t, zero)
                                  for t in tiles for half in (0, 1)], axis=0)
            s = jnp.concatenate(
                [lax.dot_general(k_loc[win, ks], qs, _NT, preferred_element_type=F32) + bias,
                 lax.dot_general(k_ctx[:, ks], qs, _NT, preferred_element_type=F32)], axis=0)
            chains.append((rows, win, hk, ks, s))

    stage2 = []
    for rows, win, hk, ks, s in chains:
        sk = jnp.concatenate(
            [jnp.full((1, blk), sink_ref[hk * group + j], F32) for j in range(group)], axis=1)
        m = jnp.maximum(jnp.max(s, axis=0, keepdims=True), sk)
        e = jnp.exp(s - m)
        den = jnp.sum(e, axis=0, keepdims=True) + jnp.exp(sk - m)
        stage2.append((rows, win, hk, ks, e.astype(BF16), den))

    for rows, win, hk, ks, e, den in stage2:
        vt = jnp.concatenate([vt_loc[ks, win], vt_ctx[ks, :]], axis=1)
        o = (_dot(vt, e) * (1.0 / den)).T
        for p in range(group // 2):
            pair = jnp.where(lo, o[2 * p * blk:(2 * p + 1) * blk], o[(2 * p + 1) * blk:(2 * p + 2) * blk])
            c0 = (hk * group + 2 * p) * HEAD_DIM
            o_ref[rows, c0:c0 + LANES] = pair.astype(BF16)


def _attention(geom, sink, q, k, vt):
    blk, stp = ATT_BLOCK, ATT_STEP
    nsub = stp // blk
    ncs, nss = geom.Lc // stp, geom.S // stp
    nsb = geom.S // blk
    base_s, base_b = geom.n_ctx // stp, geom.n_ctx // blk

    def q_blk(b, i):
        return jnp.where(i < ncs, b * ncs + i, base_s + b * nss + i - ncs)

    def edge_blk(off):
        return lambda b, i: base_b + b * nsb + jnp.clip((i - ncs) * nsub + off, 0, nsb - 1)

    cur_blk = lambda b, i: base_s + b * nss + jnp.clip(i - ncs, 0, nss - 1)
    ctx_blk = lambda b, i: b
    blocks = ((blk, edge_blk(-1)), (stp, cur_blk), (blk, edge_blk(nsub)), (geom.Lc, ctx_blk))
    rows = lambda f: (lambda b, i: (f(b, i), 0))
    cols = lambda f: (lambda b, i: (0, f(b, i)))
    return pl.pallas_call(
        functools.partial(_att_body, geom),
        grid=(geom.B, ncs + nss),
        in_specs=[pl.BlockSpec(memory_space=pltpu.SMEM),
                  pl.BlockSpec((stp, ATT_Q), rows(q_blk))]
                 + [pl.BlockSpec((size, KV_W), rows(f)) for size, f in blocks]
                 + [pl.BlockSpec((KV_W, size), cols(f)) for size, f in blocks],
        out_specs=pl.BlockSpec((stp, ATT_Q), rows(q_blk)),
        out_shape=jax.ShapeDtypeStruct(q.shape, BF16),
        compiler_params=_cparams("parallel", "parallel"),
        name="window_attention",
    )(sink, q, k, k, k, k, vt, vt, vt, vt)


def _cconv_body(geom, tm, prev_ref, x_ref, next_ref, w_ref, b_ref, g_ref, bl_ref, o_ref, sh_ref):
    halo = CONV_HALO
    first, last = geom.seq_edges(pl.program_id(0), tm)
    sh_ref[0, 0:halo, :] = jnp.where(first, 0.0, prev_ref[...])
    sh_ref[0, halo:halo + tm, :] = x_ref[...]
    sh_ref[0, halo + tm:2 * halo + tm, :] = jnp.where(last, 0.0, next_ref[...])
    span = tm + 2 * halo - SUBLANES
    for s in range(1, SUBLANES):
        sh_ref[s, 0:span, :] = sh_ref[0, s:s + span, :]
    rows = 64
    for r0 in range(0, tm, rows):
        acc = None
        for j in range(CONV_WIDTH):
            a, s = divmod(halo + j - CONV_WIDTH // 2, SUBLANES)
            off = r0 + a * SUBLANES
            win = sh_ref[s, off:off + rows, :].reshape(rows // SUBLANES, SUBLANES, -1)
            t = win * w_ref[j * SUBLANES:(j + 1) * SUBLANES, :][None]
            acc = t if acc is None else acc + t
        y = acc.reshape(rows, -1) + b_ref[...]
        mu = jnp.mean(y, axis=-1, keepdims=True)
        yc = y - mu
        var = jnp.mean(yc * yc, axis=-1, keepdims=True)
        z = yc * lax.rsqrt(var + EPS) * g_ref[...] + bl_ref[...]
        o_ref[r0:r0 + rows, :] = _silu(z).astype(BF16)


def _conformer_conv(geom, u, w_dw, b_dw, g_ln, b_ln):
    n, width = u.shape
    tm = _row_tile(geom, 256)
    body = functools.partial(_cconv_body, geom, tm)
    return pl.pallas_call(
        body,
        grid=(n // tm,),
        in_specs=_halo_specs(geom, tm, CONV_HALO, width)
                 + [_resident(w_dw.shape), _resident((1, width)), _resident((1, width)), _resident((1, width))],
        out_specs=pl.BlockSpec((tm, width), lambda i: (i, 0)),
        out_shape=jax.ShapeDtypeStruct((n, width), BF16),
        scratch_shapes=[pltpu.VMEM((SUBLANES, tm + 2 * CONV_HALO, width), F32)],
        compiler_params=_cparams("parallel"),
        name="conformer_conv",
    )(u, u, u, w_dw, b_dw, g_ln, b_ln)


def _split3(x):
    hi = x.astype(BF16)
    r1 = x - hi.astype(F32)
    mid = r1.astype(BF16)
    lo = (r1 - mid.astype(F32)).astype(BF16)
    return hi, mid, lo


def _mlstm_body(qf, kf, vf, gtf, qb, kb, vb, gtb, of, ob, st_ref, m_ref):
    @pl.when(pl.program_id(1) == 0)
    def _():
        st_ref[...] = jnp.zeros_like(st_ref)
        m_ref[...] = jnp.full_like(m_ref, NEG_INF)

    L = M_CHUNK
    r = lax.broadcasted_iota(jnp.int32, (L, L), 0)
    c = lax.broadcasted_iota(jnp.int32, (L, L), 1)
    ones = jnp.ones((M_AUG, L), BF16)
    refs = ((qf, kf, vf, gtf, of), (qb, kb, vb, gtb, ob))
    prep = []
    for direction in range(2):
        keep = (r <= c) if direction == 0 else (r >= c)
        tri = jnp.where(keep, 1.0, 0.0).astype(BF16)
        gates_t = refs[direction][3][...]
        hi, mid, lo = _split3(_log_sigmoid(gates_t))
        brow_all = _dot(hi, tri) + _dot(mid, tri) + _dot(lo, tri)
        u0 = direction * M_HEADS
        c_rows = gates_t[u0:u0 + M_HEADS] - brow_all[2 * M_HEADS + u0:2 * M_HEADS + u0 + M_HEADS]
        c_cols = jnp.concatenate([c_rows, jnp.zeros((LANES - M_HEADS, L), F32)], axis=0).T
        prep.append((keep, gates_t, brow_all, c_cols))

    units = []
    for direction in range(2):
        keep, gates_t, brow_all, c_cols = prep[direction]
        q_ref, k_ref = refs[direction][0], refs[direction][1]
        for h in range(M_HEADS):
            u = direction * M_HEADS + h
            hs = slice(h * M_HEAD_DIM, (h + 1) * M_HEAD_DIM)
            q, k = q_ref[:, hs], k_ref[:, hs]
            b_row = brow_all[2 * M_HEADS + u:2 * M_HEADS + u + 1, :]
            i_row = gates_t[u:u + 1, :]
            m_prev = m_ref[u:u + 1, 0:1]
            state = st_ref[u]
            dmat = jnp.where(keep, b_row + c_cols[:, h:h + 1], NEG_INF)
            a_row = b_row + m_prev
            mt = jnp.maximum(a_row, jnp.max(dmat, axis=0, keepdims=True))
            big = lax.dot_general(jnp.concatenate([k, state.astype(BF16)], axis=0), q, _NT,
                                  preferred_element_type=F32)
            units.append((direction, u, hs, k, b_row, i_row, m_prev, state, dmat, a_row, mt, big))

    units2 = []
    for (direction, u, hs, k, b_row, i_row, m_prev, state, dmat, a_row, mt, big) in units:
        vt_aug = jnp.concatenate([refs[direction][2][hs, :], ones], axis=0)
        w_row = jnp.exp(a_row - mt)
        smat = (big[:L] * jnp.exp(dmat - mt)).astype(BF16)
        tot = w_row * big[L:] + _dot(vt_aug, smat)
        den = tot[M_HEAD_DIM:M_HEAD_DIM + 1, :]
        scale = 1.0 / jnp.maximum(jnp.abs(den), jnp.exp(-mt))
        refs[direction][4][:, hs] = (tot[:M_HEAD_DIM] * scale).T
        units2.append((direction, u, k, b_row, i_row, m_prev, state, vt_aug))

    for (direction, u, k, b_row, i_row, m_prev, state, vt_aug) in units2:
        end = L - 1 if direction == 0 else 0
        total = b_row[:, end:end + 1]
        g_row = total - b_row + i_row
        m_new = jnp.maximum(total + m_prev, jnp.max(g_row, axis=1, keepdims=True))
        decay = jnp.exp(total + m_prev - m_new)
        vw = (vt_aug.astype(F32) * jnp.exp(g_row - m_new)).astype(BF16)
        st_ref[u] = decay * state + _dot(vw, k)
        m_ref[u:u + 1, :] = jnp.broadcast_to(m_new, (1, LANES))


def _mlstm_scan(geom, q, k, vt, gates_t):
    L = M_CHUNK
    ncc, nsc = geom.Lc // L, geom.S // L
    base = geom.n_ctx // L

    def fwd(b, j):
        return jnp.where(j < ncc, b * ncc + j, base + b * nsc + j - ncc)

    def bwd(b, j):
        return jnp.where(j < ncc, b * ncc + ncc - 1 - j, base + b * nsc + nsc - 1 - (j - ncc))

    def specs(m):
        rows = lambda b, j: (m(b, j), 0)
        cols = lambda b, j: (0, m(b, j))
        return [pl.BlockSpec((L, M_WIDTH), rows), pl.BlockSpec((L, M_WIDTH), rows),
                pl.BlockSpec((M_WIDTH, L), cols), pl.BlockSpec((N_GATE_COLS, L), cols)]

    n = q.shape[0]
    return pl.pallas_call(
        _mlstm_body,
        grid=(geom.B, ncc + nsc),
        in_specs=specs(fwd) + specs(bwd),
        out_specs=[pl.BlockSpec((L, M_WIDTH), lambda b, j: (fwd(b, j), 0)),
                   pl.BlockSpec((L, M_WIDTH), lambda b, j: (bwd(b, j), 0))],
        out_shape=[jax.ShapeDtypeStruct((n, M_WIDTH), F32)] * 2,
        scratch_shapes=[pltpu.VMEM((2 * M_HEADS, M_HEAD_DIM + M_AUG, M_HEAD_DIM), F32),
                        pltpu.VMEM((2 * M_HEADS, LANES), F32)],
        compiler_params=_cparams("parallel", "arbitrary"),
        name="mlstm_scan",
    )(q, k, vt, gates_t, q, k, vt, gates_t)


def _mix_ffn_body(n_chunks, final, x_ref, mod_ref, att_ref, cact_ref, hf_ref, hb_ref, om_ref, br_ref, gm_ref,
                  g_ref, gf_ref, wa_ref, wp_ref, wm_ref, wo_ref, wg_ref, wu_ref, wd_ref, o_ref):
    d = x_ref.shape[1]
    mod = mod_ref[0]
    hm = _sigmoid(om_ref[...].astype(F32)) * (hf_ref[...] + hb_ref[...])
    parts = []
    for h in range(M_HEADS):
        t = hm[:, h * M_HEAD_DIM:(h + 1) * M_HEAD_DIM]
        mu = jnp.mean(t, axis=-1, keepdims=True)
        tc = t - mu
        var = jnp.mean(tc * tc, axis=-1, keepdims=True)
        parts.append(tc * lax.rsqrt(var + EPS))
    hn = (jnp.concatenate(parts, axis=-1) * gm_ref[...]).astype(BF16)
    ya = _dot(att_ref[...], wa_ref[...])
    yb = _dot(cact_ref[...], wp_ref[...])
    yc = _dot(hn, wm_ref[...])
    merged = (_sigmoid(br_ref[:, 0:d].astype(F32)) * ya + _sigmoid(br_ref[:, d:2 * d].astype(F32)) * yb
              + _sigmoid(br_ref[:, 2 * d:3 * d].astype(F32)) * yc)
    x = x_ref[...] + mod[:, 2 * d:3 * d] * _dot(merged.astype(BF16), wo_ref[...])
    h = _norm_mod(x, g_ref[...], mod[:, 4 * d:5 * d], mod[:, 3 * d:4 * d]).astype(BF16)
    cw = wg_ref.shape[1] // n_chunks
    acc = None
    for j in range(n_chunks):
        sl = slice(j * cw, (j + 1) * cw)
        act = (_silu(_dot(h, wg_ref[:, sl])) * _dot(h, wu_ref[:, sl])).astype(BF16)
        t = _dot(act, wd_ref[sl, :])
        acc = t if acc is None else acc + t
    y = x + mod[:, 5 * d:6 * d] * acc
    o_ref[...] = _rms(y) * gf_ref[...] if final else y


def _mix_ffn(geom, x, mod3, att, cact, hf, hb, om, br, g_mnorm, g_ffn, g_final, w, latent_only):
    n, d = x.shape
    tm = _row_tile(geom, 256)
    skip = geom.n_ctx // tm if latent_only else 0
    row = lambda i: (i + skip, 0)
    mod_map = lambda i: (geom.mod_row(i + skip, tm), 0, 0)
    ff = w["ff_gate"].shape[1]
    n_chunks = 2 if ff % (2 * LANES) == 0 else 1
    weights = (w["att_out"], w["pw"], w["mout"], w["out"], w["ff_gate"], w["ff_up"], w["ff_down"])
    return pl.pallas_call(
        functools.partial(_mix_ffn_body, n_chunks, latent_only),
        grid=(n // tm - skip,),
        in_specs=[pl.BlockSpec((tm, d), row),
                  pl.BlockSpec((1, 1, mod3.shape[2]), mod_map),
                  pl.BlockSpec((tm, ATT_Q), row), pl.BlockSpec((tm, CONV_DIM), row),
                  pl.BlockSpec((tm, M_WIDTH), row), pl.BlockSpec((tm, M_WIDTH), row),
                  pl.BlockSpec((tm, M_WIDTH), row), pl.BlockSpec((tm, 3 * d), row),
                  _resident((1, M_WIDTH)), _resident((1, d)), _resident((1, d))]
                 + [_resident(a.shape) for a in weights],
        out_specs=pl.BlockSpec((tm, d), lambda i: (i, 0)),
        out_shape=jax.ShapeDtypeStruct((n - skip * tm, d), F32),
        compiler_params=_cparams("parallel"),
        name="mix_ffn",
    )(x, mod3, att, cact, hf, hb, om, br, g_mnorm, g_ffn, g_final, *weights)


def _rope_table(geom):
    p = jnp.arange(geom.S)
    n_freq = HEAD_DIM // 4
    inv_freq = ROPE_BASE ** (-jnp.arange(n_freq, dtype=F32) / n_freq)
    ang_r = (p // GRID_W).astype(F32)[:, None] * inv_freq
    ang_c = (p % GRID_W).astype(F32)[:, None] * inv_freq
    ang = jnp.concatenate([ang_r, ang_r, ang_c, ang_c], axis=-1)
    reps = LANES // HEAD_DIM
    cos = jnp.tile(jnp.cos(ang), (1, reps))
    sin = jnp.tile(jnp.sin(ang), (1, reps))
    first_half = (jnp.arange(LANES) % (HEAD_DIM // 2)) < HEAD_DIM // 4
    lat = jnp.concatenate([cos, jnp.where(first_half, -sin, 0.0), jnp.where(first_half, 0.0, sin)], axis=1)
    ident = jnp.concatenate([jnp.ones((geom.Lc, LANES), F32), jnp.zeros((geom.Lc, 2 * LANES), F32)], axis=1)
    return jnp.concatenate([ident, lat], axis=0)


def _pack_layer(l, d, w_in, w_att_out, w_conv_pw, w_mlstm_out, w_out, w_ff_gate, w_ff_up, w_ff_down):
    sizes = (ATT_Q + 2 * ATT_KV, 2 * CONV_DIM, 2 * M_WIDTH, M_WIDTH, M_WIDTH, N_GATE_COLS, 3 * d)
    names = ("qkv", "glu", "qkm", "vm", "om", "gate", "br")
    w = {}
    start = 0
    for name, size in zip(names, sizes):
        w[name] = w_in[l, :, start:start + size].astype(BF16)
        start += size
    v_cols = w["qkv"][:, ATT_Q + ATT_KV:].reshape(d, N_KV_HEADS, 1, HEAD_DIM)
    v_dup = jnp.broadcast_to(v_cols, (d, N_KV_HEADS, 2, HEAD_DIM)).reshape(d, KV_W)
    w["vgt"] = jnp.concatenate([w.pop("vm"), v_dup, w["gate"]], axis=1).T
    w["qkv"] = w["qkv"][:, :ATT_Q + ATT_KV]
    del w["gate"]
    w["att_out"] = w_att_out[l].astype(BF16)
    w["pw"] = w_conv_pw[l].astype(BF16)
    w["mout"] = w_mlstm_out[l].astype(BF16)
    w["out"] = w_out[l].astype(BF16)
    w["ff_gate"] = w_ff_gate[l].astype(BF16)
    w["ff_up"] = w_ff_up[l].astype(BF16)
    w["ff_down"] = w_ff_down[l].astype(BF16)
    return w


def kernel(x, c, ctx, c_ctx, w_ada, b_ada, g_norm_mix, g_norm_ffn, w_in, b_mgate, att_sink, w_att_out, w_conv_dw, b_conv_dw, g_conv_ln, b_conv_ln, w_conv_pw, w_mconv, g_mlstm_norm, w_mlstm_out, w_out, w_ff_gate, w_ff_up, w_ff_down, g_final):
    batch, seq, d = x.shape
    ctx_len = ctx.shape[1]
    depth = w_ada.shape[0]
    geom = _Geom(batch, seq, ctx_len)
    assert seq % ATT_STEP == 0 and ctx_len % ATT_STEP == 0 and seq % GRID_W == 0
    assert seq % M_CHUNK == 0 and ctx_len % M_CHUNK == 0 and d % LANES == 0
    assert LANES == 2 * HEAD_DIM and ATT_KV == LANES

    mod_rows = -(-(batch + 1) // SUBLANES) * SUBLANES
    cvec = jnp.zeros((mod_rows, d), F32).at[:batch].set(c).at[batch].set(c_ctx)
    mod = _modulation(cvec, w_ada, b_ada)
    rope = _rope_table(geom)
    xs = _assemble(geom, ctx.reshape(geom.n_ctx, d), x.reshape(batch * seq, d))
    g_fin = g_final.reshape(1, d)

    for l in range(depth):
        last = l == depth - 1
        w = _pack_layer(l, d, w_in, w_att_out, w_conv_pw, w_mlstm_out, w_out, w_ff_gate, w_ff_up, w_ff_down)
        mod3 = mod[l].reshape(mod_rows, 1, 6 * d)
        q, k, vt, u, qm, km, vmt, om, gates_t, br = _projection(
            geom, xs, mod3, g_norm_mix[l].reshape(1, d), rope, b_mgate[l], w_mconv[l], w)
        att = _attention(geom, att_sink[l], q, k, vt)
        cact = _conformer_conv(geom, u, jnp.repeat(w_conv_dw[l], SUBLANES, axis=0), b_conv_dw[l].reshape(1, -1),
                               g_conv_ln[l].reshape(1, -1), b_conv_ln[l].reshape(1, -1))
        hf, hb = _mlstm_scan(geom, qm, km, vmt, gates_t)
        xs = _mix_ffn(geom, xs, mod3, att, cact, hf, hb, om, br, g_mlstm_norm[l].reshape(1, -1),
                      g_norm_ffn[l].reshape(1, d), g_fin, w, last)

    return xs.reshape(batch, seq, d)
```

```python
import functools

import jax
import jax.numpy as jnp
from jax import lax
from jax.experimental import pallas as pl
from jax.experimental.pallas import tpu as pltpu

F32 = jnp.float32
BF16 = jnp.bfloat16

GRID_W = 64
N_Q_HEADS = 8
N_KV_HEADS = 2
HEAD_DIM = 64
WINDOW = 128
ATT_BLOCK = 128
ROPE_BASE = 10000.0
ATT_Q = N_Q_HEADS * HEAD_DIM
ATT_KV = N_KV_HEADS * HEAD_DIM
CONV_DIM = 512
CONV_WIDTH = 31
M_HEADS = 4
M_HEAD_DIM = 128
M_WIDTH = M_HEADS * M_HEAD_DIM
M_SHORT_CONV = 3
N_GATE_COLS = 4 * M_HEADS
EPS = 1e-6
NEG_INF = -1e30
LOG2E = 1.4426950408889634

LANES = 128
SUBLANES = 8
MXU_TILE = 256
M_CHUNK = 256
M_AUG = 16
ATT_STEP = 256
CONV_HALO = 16
KV_W = N_KV_HEADS * LANES
VMEM_LIMIT = 52 * 1024 * 1024


def _cparams(*sem):
    return pltpu.CompilerParams(dimension_semantics=sem, vmem_limit_bytes=VMEM_LIMIT)


def _resident(shape):
    nd = len(shape)
    return pl.BlockSpec(shape, lambda *_: (0,) * nd, pipeline_mode=pl.Buffered(1))


def _sigmoid(x):
    return 1.0 / (1.0 + jnp.exp(-x))


def _silu(x):
    return x * _sigmoid(x)


def _log_sigmoid(x):
    return jnp.minimum(x, 0.0) - jnp.log(1.0 + jnp.exp(-jnp.abs(x)))


def _rms(x):
    return x * lax.rsqrt(jnp.mean(x * x, axis=-1, keepdims=True) + EPS)


def _norm_mod(x, g, sc, sh):
    return (_rms(x) * g) * (1.0 + sc) + sh


def _dot(a, b):
    return jnp.dot(a, b, preferred_element_type=F32)


_NT = (((1,), (1,)), ((), ()))


def _mod_body(c_ref, w_ref, b_ref, o_ref):
    h = _silu(c_ref[...]).astype(BF16)
    o_ref[0] = _dot(h, w_ref[0].astype(BF16)) + b_ref[0]


def _modulation(cvec, w_ada, b_ada):
    depth, d, n6 = w_ada.shape
    rows = cvec.shape[0]
    tn = n6 // 4
    return pl.pallas_call(
        _mod_body,
        grid=(depth, n6 // tn),
        in_specs=[pl.BlockSpec((rows, d), lambda l, j: (0, 0)),
                  pl.BlockSpec((1, d, tn), lambda l, j: (l, 0, j)),
                  pl.BlockSpec((1, 1, tn), lambda l, j: (l, 0, j))],
        out_specs=pl.BlockSpec((1, rows, tn), lambda l, j: (l, 0, j)),
        out_shape=jax.ShapeDtypeStruct((depth, rows, n6), F32),
        compiler_params=_cparams("arbitrary", "arbitrary"),
        name="ada_mod",
    )(cvec, w_ada, b_ada.reshape(depth, 1, n6))


class _Geom:
    def __init__(self, batch, seq, ctx_len):
        self.B, self.S, self.Lc = batch, seq, ctx_len
        self.n_ctx = batch * ctx_len
        self.N = self.n_ctx + batch * seq

    def mod_row(self, tile, tm):
        n_ctx_tiles = self.n_ctx // tm
        return jnp.where(tile < n_ctx_tiles, self.B, (tile - n_ctx_tiles) // (self.S // tm))

    def seq_tile(self, tile, tm):
        n_ctx_tiles = self.n_ctx // tm
        tc, ts = self.Lc // tm, self.S // tm
        return jnp.where(tile < n_ctx_tiles, tile % tc, tc + (tile - n_ctx_tiles) % ts)

    def seq_edges(self, tile, tm):
        n_ctx_tiles = self.n_ctx // tm
        tc, ts = self.Lc // tm, self.S // tm
        pos = jnp.where(tile < n_ctx_tiles, tile % tc, (tile - n_ctx_tiles) % ts)
        last = jnp.where(tile < n_ctx_tiles, tc - 1, ts - 1)
        return pos == 0, pos == last


def _row_tile(geom, want):
    tm = want
    while geom.n_ctx % tm or geom.S % tm:
        tm //= 2
    return tm


def _halo_specs(geom, tm, halo, width):
    per = tm // halo
    last = geom.N // halo - 1
    return [pl.BlockSpec((halo, width), lambda i: (jnp.maximum(i * per - 1, 0), 0)),
            pl.BlockSpec((tm, width), lambda i: (i, 0)),
            pl.BlockSpec((halo, width), lambda i: (jnp.minimum((i + 1) * per, last), 0))]


def _assemble_body(n_ctx_tiles, ctx_ref, x_ref, o_ref):
    o_ref[...] = jnp.where(pl.program_id(0) < n_ctx_tiles, ctx_ref[...], x_ref[...])


def _assemble(geom, ctx2, x2):
    d = x2.shape[1]
    tm = _row_tile(geom, 1024)
    nct = geom.n_ctx // tm
    return pl.pallas_call(
        functools.partial(_assemble_body, nct),
        grid=(geom.N // tm,),
        in_specs=[pl.BlockSpec((tm, d), lambda i: (jnp.minimum(i, nct - 1), 0)),
                  pl.BlockSpec((tm, d), lambda i: (jnp.maximum(i - nct, 0), 0))],
        out_specs=pl.BlockSpec((tm, d), lambda i: (i, 0)),
        out_shape=jax.ShapeDtypeStruct((geom.N, d), F32),
        compiler_params=_cparams("arbitrary"),
        name="assemble_tokens",
    )(ctx2, x2)


def _rope(x, cos, sin_a, sin_b):
    return x * cos + pltpu.roll(x, LANES - 16, 1) * sin_a + pltpu.roll(x, 16, 1) * sin_b


def _proj_body(geom, tm, xp_ref, x_ref, xn_ref, mod_ref, g_ref, rope_ref, bgt_ref, wc_ref,
               wqkv_ref, wglu_ref, wqkm_ref, wvgt_ref, wom_ref, wbr_ref,
               q_ref, k_ref, vt_ref, u_ref, qm_ref, km_ref, vmt_ref, om_ref, gatet_ref, br_ref, pad_ref):
    d = x_ref.shape[1]
    mod = mod_ref[0]
    g, sc, sh = g_ref[...], mod[:, d:2 * d], mod[:, 0:d]
    h = _norm_mod(x_ref[...], g, sc, sh).astype(BF16)
    cos, sa, sb = rope_ref[:, 0:LANES], rope_ref[:, LANES:2 * LANES], rope_ref[:, 2 * LANES:3 * LANES]

    glu = _dot(h, wglu_ref[...])
    u_ref[...] = glu[:, :CONV_DIM] * _sigmoid(glu[:, CONV_DIM:])

    halo = SUBLANES
    h_halo = _norm_mod(jnp.concatenate([xp_ref[...], xn_ref[...]], axis=0), g, sc, sh).astype(BF16)
    qkm = _dot(jnp.concatenate([h, h_halo], axis=0), wqkm_ref[...])
    first, last = geom.seq_edges(pl.program_id(0), tm)
    pad_ref[0:halo, :] = jnp.where(first, 0.0, qkm[tm:tm + halo])
    pad_ref[halo:halo + tm, :] = qkm[0:tm]
    pad_ref[halo + tm:2 * halo + tm, :] = jnp.where(last, 0.0, qkm[tm + halo:tm + 2 * halo])
    acc = None
    for j in range(M_SHORT_CONV):
        off = halo + j - M_SHORT_CONV // 2
        t = pad_ref[off:off + tm, :] * wc_ref[j:j + 1, :]
        acc = t if acc is None else acc + t
    y = _silu(acc)
    qm_ref[...] = y[:, :M_WIDTH].astype(BF16)
    km_ref[...] = (y[:, M_WIDTH:] * M_HEAD_DIM ** -0.5).astype(BF16)

    qkv = _dot(h, wqkv_ref[...])
    scale = HEAD_DIM ** -0.5 * LOG2E
    for j in range(ATT_Q // LANES):
        sl = slice(j * LANES, (j + 1) * LANES)
        q_ref[:, sl] = _rope(qkv[:, sl] * scale, cos, sa, sb).astype(BF16)
    lo = lax.broadcasted_iota(jnp.int32, (tm, LANES), 1) < HEAD_DIM
    keys = _rope(qkv[:, ATT_Q:ATT_Q + ATT_KV], cos, sa, sb)
    swapped = pltpu.roll(keys, HEAD_DIM, 1)
    k_ref[:, 0:LANES] = jnp.where(lo, keys, swapped).astype(BF16)
    k_ref[:, LANES:2 * LANES] = jnp.where(lo, swapped, keys).astype(BF16)

    vgt = lax.dot_general(wvgt_ref[...], h, _NT, preferred_element_type=F32)
    vmt_ref[...] = vgt[:M_WIDTH].astype(BF16)
    vt_ref[...] = vgt[M_WIDTH:M_WIDTH + KV_W].astype(BF16)
    gatet_ref[...] = vgt[M_WIDTH + KV_W:] + bgt_ref[...]
    om_ref[...] = _dot(h, wom_ref[...]).astype(BF16)
    br_ref[...] = _dot(h, wbr_ref[...]).astype(BF16)


def _projection(geom, x, mod3, g, rope, b_gate, w_mconv, w):
    n, d = x.shape
    tm = _row_tile(geom, 256)
    row = lambda i: (i, 0)
    col = lambda i: (0, i)
    mod_map = lambda i: (geom.mod_row(i, tm), 0, 0)
    rope_map = lambda i: (geom.seq_tile(i, tm), 0)
    outs = ((ATT_Q, BF16, False), (KV_W, BF16, False), (KV_W, BF16, True), (CONV_DIM, F32, False),
            (M_WIDTH, BF16, False), (M_WIDTH, BF16, False), (M_WIDTH, BF16, True), (M_WIDTH, BF16, False),
            (N_GATE_COLS, F32, True), (3 * d, BF16, False))
    weights = (w["qkv"], w["glu"], w["qkm"], w["vgt"], w["om"], w["br"])
    return pl.pallas_call(
        functools.partial(_proj_body, geom, tm),
        grid=(n // tm,),
        in_specs=_halo_specs(geom, tm, SUBLANES, d)
                 + [pl.BlockSpec((1, 1, mod3.shape[2]), mod_map),
                    _resident((1, d)),
                    pl.BlockSpec((tm, rope.shape[1]), rope_map),
                    _resident((N_GATE_COLS, 1)), _resident(w_mconv.shape)]
                 + [_resident(a.shape) for a in weights],
        out_specs=[pl.BlockSpec((c, tm), col) if fm else pl.BlockSpec((tm, c), row) for c, _, fm in outs],
        out_shape=[jax.ShapeDtypeStruct((c, n) if fm else (n, c), t) for c, t, fm in outs],
        scratch_shapes=[pltpu.VMEM((tm + 2 * SUBLANES, 2 * M_WIDTH), F32)],
        compiler_params=_cparams("parallel"),
        name="in_proj",
    )(x, x, x, mod3, g, rope, b_gate.reshape(N_GATE_COLS, 1), w_mconv, *weights)


def _att_body(geom, sink_ref, q_ref, kp_ref, kc_ref, kn_ref, kx_ref, vp_ref, vc_ref, vn_ref, vx_ref, o_ref):
    blk = ATT_BLOCK
    nsub = ATT_STEP // blk
    step = pl.program_id(1) - geom.Lc // ATT_STEP
    group = N_Q_HEADS // N_KV_HEADS
    kk = lax.broadcasted_iota(jnp.int32, (3 * blk, blk), 0)
    qq = lax.broadcasted_iota(jnp.int32, (3 * blk, blk), 1)
    lo = lax.broadcasted_iota(jnp.int32, (blk, LANES), 1) < HEAD_DIM
    zero = jnp.zeros((blk, LANES), BF16)
    k_loc = jnp.concatenate([kp_ref[...], kc_ref[...], kn_ref[...]], axis=0)
    vt_loc = jnp.concatenate([vp_ref[...], vc_ref[...], vn_ref[...]], axis=1)
    k_ctx, vt_ctx = kx_ref[...], vx_ref[...]
    chains = []
    for sub in range(nsub):
        n = step * nsub + sub
        key_pos = (n - 1) * blk + kk
        q_pos = n * blk + qq
        valid = (jnp.abs(key_pos - q_pos) <= WINDOW) & (key_pos >= 0) & (key_pos < geom.S) & (n >= 0)
        bias = jnp.where(valid, 0.0, NEG_INF)
        bias = jnp.concatenate([bias] * group, axis=1)
        rows = slice(sub * blk, (sub + 1) * blk)
        win = slice(sub * blk, (sub + 3) * blk)
        for hk in range(N_KV_HEADS):
            ks = slice(hk * LANES, (hk + 1) * LANES)
            tiles = [q_ref[rows, (hk * group + 2 * p) * HEAD_DIM:(hk * group + 2 * p + 2) * HEAD_DIM]
                     for p in range(group // 2)]
            qs = jnp.concatenate([jnp.where(lo if half == 0 else ~lo, t, zero)
                                  for t in tiles for half in (0, 1)], axis=0)
            s = jnp.concatenate(
                [lax.dot_general(k_loc[win, ks], qs, _NT, preferred_element_type=F32) + bias,
                 lax.dot_general(k_ctx[:, ks], qs, _NT, preferred_element_type=F32)], axis=0)
            chains.append((rows, win, hk, ks, s))

    stage2 = []
    for rows, win, hk, ks, s in chains:
        sk = jnp.concatenate(
            [jnp.full((1, blk), sink_ref[hk * group + j] * LOG2E, F32) for j in range(group)], axis=1)
        m = jnp.maximum(jnp.max(s, axis=0, keepdims=True), sk)
        e = jnp.exp2(s - m)
        den = jnp.sum(e, axis=0, keepdims=True) + jnp.exp2(sk - m)
        stage2.append((rows, win, hk, ks, e.astype(BF16), den))

    for rows, win, hk, ks, e, den in stage2:
        vt = jnp.concatenate([vt_loc[ks, win], vt_ctx[ks, :]], axis=1)
        o = (_dot(vt, e) * (1.0 / den)).T
        for p in range(group // 2):
            pair = jnp.where(lo, o[2 * p * blk:(2 * p + 1) * blk], o[(2 * p + 1) * blk:(2 * p + 2) * blk])
            c0 = (hk * group + 2 * p) * HEAD_DIM
            o_ref[rows, c0:c0 + LANES] = pair.astype(BF16)


def _attention(geom, sink, q, k, vt):
    blk, stp = ATT_BLOCK, ATT_STEP
    nsub = stp // blk
    ncs, nss = geom.Lc // stp, geom.S // stp
    nsb = geom.S // blk
    base_s, base_b = geom.n_ctx // stp, geom.n_ctx // blk

    def q_blk(b, i):
        return jnp.where(i < ncs, b * ncs + i, base_s + b * nss + i - ncs)

    def edge_blk(off):
        return lambda b, i: base_b + b * nsb + jnp.clip((i - ncs) * nsub + off, 0, nsb - 1)

    cur_blk = lambda b, i: base_s + b * nss + jnp.clip(i - ncs, 0, nss - 1)
    ctx_blk = lambda b, i: b
    blocks = ((blk, edge_blk(-1)), (stp, cur_blk), (blk, edge_blk(nsub)), (geom.Lc, ctx_blk))
    rows = lambda f: (lambda b, i: (f(b, i), 0))
    cols = lambda f: (lambda b, i: (0, f(b, i)))
    return pl.pallas_call(
        functools.partial(_att_body, geom),
        grid=(geom.B, ncs + nss),
        in_specs=[pl.BlockSpec(memory_space=pltpu.SMEM),
                  pl.BlockSpec((stp, ATT_Q), rows(q_blk))]
                 + [pl.BlockSpec((size, KV_W), rows(f)) for size, f in blocks]
                 + [pl.BlockSpec((KV_W, size), cols(f)) for size, f in blocks],
        out_specs=pl.BlockSpec((stp, ATT_Q), rows(q_blk)),
        out_shape=jax.ShapeDtypeStruct(q.shape, BF16),
        compiler_params=_cparams("parallel", "parallel"),
        name="window_attention",
    )(sink, q, k, k, k, k, vt, vt, vt, vt)


def _cconv_body(geom, tm, prev_ref, x_ref, next_ref, w_ref, b_ref, g_ref, bl_ref, o_ref, sh_ref):
    halo = CONV_HALO
    first, last = geom.seq_edges(pl.program_id(0), tm)
    sh_ref[0, 0:halo, :] = jnp.where(first, 0.0, prev_ref[...])
    sh_ref[0, halo:halo + tm, :] = x_ref[...]
    sh_ref[0, halo + tm:2 * halo + tm, :] = jnp.where(last, 0.0, next_ref[...])
    span = tm + 2 * halo - SUBLANES
    for s in range(1, SUBLANES):
        sh_ref[s, 0:span, :] = sh_ref[0, s:s + span, :]
    rows = 64
    for r0 in range(0, tm, rows):
        acc = None
        for j in range(CONV_WIDTH):
            a, s = divmod(halo + j - CONV_WIDTH // 2, SUBLANES)
            off = r0 + a * SUBLANES
            win = sh_ref[s, off:off + rows, :].reshape(rows // SUBLANES, SUBLANES, -1)
            t = win * w_ref[j * SUBLANES:(j + 1) * SUBLANES, :][None]
            acc = t if acc is None else acc + t
        y = acc.reshape(rows, -1) + b_ref[...]
        mu = jnp.mean(y, axis=-1, keepdims=True)
        yc = y - mu
        var = jnp.mean(yc * yc, axis=-1, keepdims=True)
        z = yc * lax.rsqrt(var + EPS) * g_ref[...] + bl_ref[...]
        o_ref[r0:r0 + rows, :] = _silu(z).astype(BF16)


def _conformer_conv(geom, u, w_dw, b_dw, g_ln, b_ln):
    n, width = u.shape
    tm = _row_tile(geom, 256)
    body = functools.partial(_cconv_body, geom, tm)
    return pl.pallas_call(
        body,
        grid=(n // tm,),
        in_specs=_halo_specs(geom, tm, CONV_HALO, width)
                 + [_resident(w_dw.shape), _resident((1, width)), _resident((1, width)), _resident((1, width))],
        out_specs=pl.BlockSpec((tm, width), lambda i: (i, 0)),
        out_shape=jax.ShapeDtypeStruct((n, width), BF16),
        scratch_shapes=[pltpu.VMEM((SUBLANES, tm + 2 * CONV_HALO, width), F32)],
        compiler_params=_cparams("parallel"),
        name="conformer_conv",
    )(u, u, u, w_dw, b_dw, g_ln, b_ln)


def _split3(x):
    hi = x.astype(BF16)
    r1 = x - hi.astype(F32)
    mid = r1.astype(BF16)
    lo = (r1 - mid.astype(F32)).astype(BF16)
    return hi, mid, lo


def _mlstm_body(qf, kf, vf, gtf, qb, kb, vb, gtb, of, ob, st_ref, m_ref):
    @pl.when(pl.program_id(1) == 0)
    def _():
        st_ref[...] = jnp.zeros_like(st_ref)
        m_ref[...] = jnp.full_like(m_ref, NEG_INF)

    L = M_CHUNK
    r = lax.broadcasted_iota(jnp.int32, (L, L), 0)
    c = lax.broadcasted_iota(jnp.int32, (L, L), 1)
    ones = jnp.ones((M_AUG, L), BF16)
    refs = ((qf, kf, vf, gtf, of), (qb, kb, vb, gtb, ob))
    prep = []
    for direction in range(2):
        keep = (r <= c) if direction == 0 else (r >= c)
        tri = jnp.where(keep, 1.0, 0.0).astype(BF16)
        gates_t = refs[direction][3][...]
        hi, mid, lo = _split3(_log_sigmoid(gates_t))
        brow_all = _dot(hi, tri) + _dot(mid, tri) + _dot(lo, tri)
        u0 = direction * M_HEADS
        c_rows = gates_t[u0:u0 + M_HEADS] - brow_all[2 * M_HEADS + u0:2 * M_HEADS + u0 + M_HEADS]
        c_cols = jnp.concatenate([c_rows, jnp.zeros((LANES - M_HEADS, L), F32)], axis=0).T
        prep.append((keep, gates_t, brow_all, c_cols))

    units = []
    for direction in range(2):
        keep, gates_t, brow_all, c_cols = prep[direction]
        q_ref, k_ref = refs[direction][0], refs[direction][1]
        for h in range(M_HEADS):
            u = direction * M_HEADS + h
            hs = slice(h * M_HEAD_DIM, (h + 1) * M_HEAD_DIM)
            q, k = q_ref[:, hs], k_ref[:, hs]
            b_row = brow_all[2 * M_HEADS + u:2 * M_HEADS + u + 1, :]
            i_row = gates_t[u:u + 1, :]
            m_prev = m_ref[u:u + 1, 0:1]
            state = st_ref[u]
            dmat = jnp.where(keep, b_row + c_cols[:, h:h + 1], NEG_INF)
            a_row = b_row + m_prev
            mt = jnp.maximum(a_row, jnp.max(dmat, axis=0, keepdims=True))
            big = lax.dot_general(jnp.concatenate([k, state.astype(BF16)], axis=0), q, _NT,
                                  preferred_element_type=F32)
            units.append((direction, u, hs, k, b_row, i_row, m_prev, state, dmat, a_row, mt, big))

    units2 = []
    for (direction, u, hs, k, b_row, i_row, m_prev, state, dmat, a_row, mt, big) in units:
        vt_aug = jnp.concatenate([refs[direction][2][hs, :], ones], axis=0)
        w_row = jnp.exp(a_row - mt)
        smat = (big[:L] * jnp.exp(dmat - mt)).astype(BF16)
        tot = w_row * big[L:] + _dot(vt_aug, smat)
        den = tot[M_HEAD_DIM:M_HEAD_DIM + 1, :]
        scale = 1.0 / jnp.maximum(jnp.abs(den), jnp.exp(-mt))
        refs[direction][4][:, hs] = (tot[:M_HEAD_DIM] * scale).T
        units2.append((direction, u, k, b_row, i_row, m_prev, state, vt_aug))

    for (direction, u, k, b_row, i_row, m_prev, state, vt_aug) in units2:
        end = L - 1 if direction == 0 else 0
        total = b_row[:, end:end + 1]
        g_row = total - b_row + i_row
        m_new = jnp.maximum(total + m_prev, jnp.max(g_row, axis=1, keepdims=True))
        decay = jnp.exp(total + m_prev - m_new)
        vw = (vt_aug.astype(F32) * jnp.exp(g_row - m_new)).astype(BF16)
        st_ref[u] = decay * state + _dot(vw, k)
        m_ref[u:u + 1, :] = jnp.broadcast_to(m_new, (1, LANES))


def _mlstm_scan(geom, q, k, vt, gates_t):
    L = M_CHUNK
    ncc, nsc = geom.Lc // L, geom.S // L
    base = geom.n_ctx // L

    def fwd(b, j):
        return jnp.where(j < ncc, b * ncc + j, base + b * nsc + j - ncc)

    def bwd(b, j):
        return jnp.where(j < ncc, b * ncc + ncc - 1 - j, base + b * nsc + nsc - 1 - (j - ncc))

    def specs(m):
        rows = lambda b, j: (m(b, j), 0)
        cols = lambda b, j: (0, m(b, j))
        return [pl.BlockSpec((L, M_WIDTH), rows), pl.BlockSpec((L, M_WIDTH), rows),
                pl.BlockSpec((M_WIDTH, L), cols), pl.BlockSpec((N_GATE_COLS, L), cols)]

    n = q.shape[0]
    return pl.pallas_call(
        _mlstm_body,
        grid=(geom.B, ncc + nsc),
        in_specs=specs(fwd) + specs(bwd),
        out_specs=[pl.BlockSpec((L, M_WIDTH), lambda b, j: (fwd(b, j), 0)),
                   pl.BlockSpec((L, M_WIDTH), lambda b, j: (bwd(b, j), 0))],
        out_shape=[jax.ShapeDtypeStruct((n, M_WIDTH), F32)] * 2,
        scratch_shapes=[pltpu.VMEM((2 * M_HEADS, M_HEAD_DIM + M_AUG, M_HEAD_DIM), F32),
                        pltpu.VMEM((2 * M_HEADS, LANES), F32)],
        compiler_params=_cparams("parallel", "arbitrary"),
        name="mlstm_scan",
    )(q, k, vt, gates_t, q, k, vt, gates_t)


def _mix_ffn_body(final, x_ref, mod_ref, att_ref, cact_ref, hf_ref, hb_ref, om_ref, br_ref, gm_ref,
                  g_ref, gf_ref, wa_ref, wp_ref, wm_ref, wo_ref, wg_ref, wu_ref, wd_ref, o_ref):
    d = x_ref.shape[1]
    mod = mod_ref[0]
    hm = _sigmoid(om_ref[...].astype(F32)) * (hf_ref[...] + hb_ref[...])
    parts = []
    for h in range(M_HEADS):
        t = hm[:, h * M_HEAD_DIM:(h + 1) * M_HEAD_DIM]
        mu = jnp.mean(t, axis=-1, keepdims=True)
        tc = t - mu
        var = jnp.mean(tc * tc, axis=-1, keepdims=True)
        parts.append(tc * lax.rsqrt(var + EPS))
    hn = (jnp.concatenate(parts, axis=-1) * gm_ref[...]).astype(BF16)
    ya = _dot(att_ref[...], wa_ref[...])
    yb = _dot(cact_ref[...], wp_ref[...])
    yc = _dot(hn, wm_ref[...])
    merged = (_sigmoid(br_ref[:, 0:d].astype(F32)) * ya + _sigmoid(br_ref[:, d:2 * d].astype(F32)) * yb
              + _sigmoid(br_ref[:, 2 * d:3 * d].astype(F32)) * yc)
    x = x_ref[...] + mod[:, 2 * d:3 * d] * _dot(merged.astype(BF16), wo_ref[...])
    h = _norm_mod(x, g_ref[...], mod[:, 4 * d:5 * d], mod[:, 3 * d:4 * d]).astype(BF16)
    ff = wg_ref.shape[1]
    half = -(-ff // (2 * MXU_TILE)) * MXU_TILE
    acc = None
    for sl in (slice(0, half), slice(half, ff)):
        act = (_silu(_dot(h, wg_ref[:, sl])) * _dot(h, wu_ref[:, sl])).astype(BF16)
        t = _dot(act, wd_ref[sl, :])
        acc = t if acc is None else acc + t
    y = x + mod[:, 5 * d:6 * d] * acc
    o_ref[...] = _rms(y) * gf_ref[...] if final else y


def _mix_ffn(geom, x, mod3, att, cact, hf, hb, om, br, g_mnorm, g_ffn, g_final, w, latent_only):
    n, d = x.shape
    tm = _row_tile(geom, 256)
    skip = geom.n_ctx // tm if latent_only else 0
    row = lambda i: (i + skip, 0)
    mod_map = lambda i: (geom.mod_row(i + skip, tm), 0, 0)
    weights = (w["att_out"], w["pw"], w["mout"], w["out"], w["ff_gate"], w["ff_up"], w["ff_down"])
    return pl.pallas_call(
        functools.partial(_mix_ffn_body, latent_only),
        grid=(n // tm - skip,),
        in_specs=[pl.BlockSpec((tm, d), row),
                  pl.BlockSpec((1, 1, mod3.shape[2]), mod_map),
                  pl.BlockSpec((tm, ATT_Q), row), pl.BlockSpec((tm, CONV_DIM), row),
                  pl.BlockSpec((tm, M_WIDTH), row), pl.BlockSpec((tm, M_WIDTH), row),
                  pl.BlockSpec((tm, M_WIDTH), row), pl.BlockSpec((tm, 3 * d), row),
                  _resident((1, M_WIDTH)), _resident((1, d)), _resident((1, d))]
                 + [_resident(a.shape) for a in weights],
        out_specs=pl.BlockSpec((tm, d), lambda i: (i, 0)),
        out_shape=jax.ShapeDtypeStruct((n - skip * tm, d), F32),
        compiler_params=_cparams("parallel"),
        name="mix_ffn",
    )(x, mod3, att, cact, hf, hb, om, br, g_mnorm, g_ffn, g_final, *weights)


def _rope_table(geom):
    p = jnp.arange(geom.S)
    n_freq = HEAD_DIM // 4
    inv_freq = ROPE_BASE ** (-jnp.arange(n_freq, dtype=F32) / n_freq)
    ang_r = (p // GRID_W).astype(F32)[:, None] * inv_freq
    ang_c = (p % GRID_W).astype(F32)[:, None] * inv_freq
    ang = jnp.concatenate([ang_r, ang_r, ang_c, ang_c], axis=-1)
    reps = LANES // HEAD_DIM
    cos = jnp.tile(jnp.cos(ang), (1, reps))
    sin = jnp.tile(jnp.sin(ang), (1, reps))
    first_half = (jnp.arange(LANES) % (HEAD_DIM // 2)) < HEAD_DIM // 4
    lat = jnp.concatenate([cos, jnp.where(first_half, -sin, 0.0), jnp.where(first_half, 0.0, sin)], axis=1)
    ident = jnp.concatenate([jnp.ones((geom.Lc, LANES), F32), jnp.zeros((geom.Lc, 2 * LANES), F32)], axis=1)
    return jnp.concatenate([ident, lat], axis=0)


def _pack_layer(l, d, w_in, w_att_out, w_conv_pw, w_mlstm_out, w_out, w_ff_gate, w_ff_up, w_ff_down):
    sizes = (ATT_Q + 2 * ATT_KV, 2 * CONV_DIM, 2 * M_WIDTH, M_WIDTH, M_WIDTH, N_GATE_COLS, 3 * d)
    names = ("qkv", "glu", "qkm", "vm", "om", "gate", "br")
    w = {}
    start = 0
    for name, size in zip(names, sizes):
        w[name] = w_in[l, :, start:start + size].astype(BF16)
        start += size
    v_cols = w["qkv"][:, ATT_Q + ATT_KV:].reshape(d, N_KV_HEADS, 1, HEAD_DIM)
    v_dup = jnp.broadcast_to(v_cols, (d, N_KV_HEADS, 2, HEAD_DIM)).reshape(d, KV_W)
    w["vgt"] = jnp.concatenate([w.pop("vm"), v_dup, w["gate"]], axis=1).T
    w["qkv"] = w["qkv"][:, :ATT_Q + ATT_KV]
    del w["gate"]
    w["att_out"] = w_att_out[l].astype(BF16)
    w["pw"] = w_conv_pw[l].astype(BF16)
    w["mout"] = w_mlstm_out[l].astype(BF16)
    w["out"] = w_out[l].astype(BF16)
    w["ff_gate"] = w_ff_gate[l].astype(BF16)
    w["ff_up"] = w_ff_up[l].astype(BF16)
    w["ff_down"] = w_ff_down[l].astype(BF16)
    return w


def kernel(x, c, ctx, c_ctx, w_ada, b_ada, g_norm_mix, g_norm_ffn, w_in, b_mgate, att_sink, w_att_out, w_conv_dw, b_conv_dw, g_conv_ln, b_conv_ln, w_conv_pw, w_mconv, g_mlstm_norm, w_mlstm_out, w_out, w_ff_gate, w_ff_up, w_ff_down, g_final):
    batch, seq, d = x.shape
    ctx_len = ctx.shape[1]
    depth = w_ada.shape[0]
    geom = _Geom(batch, seq, ctx_len)
    assert seq % ATT_STEP == 0 and ctx_len % ATT_STEP == 0 and seq % GRID_W == 0
    assert seq % M_CHUNK == 0 and ctx_len % M_CHUNK == 0 and d % LANES == 0
    assert LANES == 2 * HEAD_DIM and ATT_KV == LANES

    mod_rows = -(-(batch + 1) // SUBLANES) * SUBLANES
    cvec = jnp.zeros((mod_rows, d), F32).at[:batch].set(c).at[batch].set(c_ctx)
    mod = _modulation(cvec, w_ada, b_ada)
    rope = _rope_table(geom)
    xs = _assemble(geom, ctx.reshape(geom.n_ctx, d), x.reshape(batch * seq, d))
    g_fin = g_final.reshape(1, d)

    for l in range(depth):
        last = l == depth - 1
        w = _pack_layer(l, d, w_in, w_att_out, w_conv_pw, w_mlstm_out, w_out, w_ff_gate, w_ff_up, w_ff_down)
        mod3 = mod[l].reshape(mod_rows, 1, 6 * d)
        q, k, vt, u, qm, km, vmt, om, gates_t, br = _projection(
            geom, xs, mod3, g_norm_mix[l].reshape(1, d), rope, b_mgate[l], w_mconv[l], w)
        att = _attention(geom, att_sink[l], q, k, vt)
        cact = _conformer_conv(geom, u, jnp.repeat(w_conv_dw[l], SUBLANES, axis=0), b_conv_dw[l].reshape(1, -1),
                               g_conv_ln[l].reshape(1, -1), b_conv_ln[l].reshape(1, -1))
        hf, hb = _mlstm_scan(geom, qm, km, vmt, gates_t)
        xs = _mix_ffn(geom, xs, mod3, att, cact, hf, hb, om, br, g_mlstm_norm[l].reshape(1, -1),
                      g_norm_ffn[l].reshape(1, d), g_fin, w, last)

    return xs.reshape(batch, seq, d)
```

```python
import functools

import jax
import jax.numpy as jnp
from jax import lax
from jax.experimental import pallas as pl
from jax.experimental.pallas import tpu as pltpu

F32 = jnp.float32
BF16 = jnp.bfloat16

GRID_W = 64
N_Q_HEADS = 8
N_KV_HEADS = 2
HEAD_DIM = 64
WINDOW = 128
ATT_BLOCK = 128
ROPE_BASE = 10000.0
ATT_Q = N_Q_HEADS * HEAD_DIM
ATT_KV = N_KV_HEADS * HEAD_DIM
CONV_DIM = 512
CONV_WIDTH = 31
M_HEADS = 4
M_HEAD_DIM = 128
M_WIDTH = M_HEADS * M_HEAD_DIM
M_SHORT_CONV = 3
N_GATE_COLS = 4 * M_HEADS
EPS = 1e-6
NEG_INF = -1e30
LOG2E = 1.4426950408889634

LANES = 128
SUBLANES = 8
MXU_TILE = 256
M_CHUNK = 256
M_AUG = 16
ATT_STEP = 256
CONV_HALO = 16
KV_W = N_KV_HEADS * LANES
VMEM_LIMIT = 52 * 1024 * 1024


def _cparams(*sem):
    return pltpu.CompilerParams(dimension_semantics=sem, vmem_limit_bytes=VMEM_LIMIT)


def _resident(shape):
    nd = len(shape)
    return pl.BlockSpec(shape, lambda *_: (0,) * nd, pipeline_mode=pl.Buffered(1))


def _sigmoid(x):
    return 1.0 / (1.0 + jnp.exp(-x))


def _silu(x):
    return x * _sigmoid(x)


def _log_sigmoid(x):
    return jnp.minimum(x, 0.0) - jnp.log(1.0 + jnp.exp(-jnp.abs(x)))


def _rms(x):
    return x * lax.rsqrt(jnp.mean(x * x, axis=-1, keepdims=True) + EPS)


def _norm_mod(x, g, sc, sh):
    return (_rms(x) * g) * (1.0 + sc) + sh


def _dot(a, b):
    return jnp.dot(a, b, preferred_element_type=F32)


_NT = (((1,), (1,)), ((), ()))


def _mod_body(c_ref, w_ref, b_ref, o_ref):
    h = _silu(c_ref[...]).astype(BF16)
    o_ref[0] = _dot(h, w_ref[0].astype(BF16)) + b_ref[0]


def _modulation(cvec, w_ada, b_ada):
    depth, d, n6 = w_ada.shape
    rows = cvec.shape[0]
    tn = n6 // 4
    return pl.pallas_call(
        _mod_body,
        grid=(depth, n6 // tn),
        in_specs=[pl.BlockSpec((rows, d), lambda l, j: (0, 0)),
                  pl.BlockSpec((1, d, tn), lambda l, j: (l, 0, j)),
                  pl.BlockSpec((1, 1, tn), lambda l, j: (l, 0, j))],
        out_specs=pl.BlockSpec((1, rows, tn), lambda l, j: (l, 0, j)),
        out_shape=jax.ShapeDtypeStruct((depth, rows, n6), F32),
        compiler_params=_cparams("arbitrary", "arbitrary"),
        name="ada_mod",
    )(cvec, w_ada, b_ada.reshape(depth, 1, n6))


class _Geom:
    def __init__(self, batch, seq, ctx_len):
        self.B, self.S, self.Lc = batch, seq, ctx_len
        self.n_ctx = batch * ctx_len
        self.N = self.n_ctx + batch * seq

    def mod_row(self, tile, tm):
        n_ctx_tiles = self.n_ctx // tm
        return jnp.where(tile < n_ctx_tiles, self.B, (tile - n_ctx_tiles) // (self.S // tm))

    def seq_tile(self, tile, tm):
        n_ctx_tiles = self.n_ctx // tm
        tc, ts = self.Lc // tm, self.S // tm
        return jnp.where(tile < n_ctx_tiles, tile % tc, tc + (tile - n_ctx_tiles) % ts)

    def seq_edges(self, tile, tm):
        n_ctx_tiles = self.n_ctx // tm
        tc, ts = self.Lc // tm, self.S // tm
        pos = jnp.where(tile < n_ctx_tiles, tile % tc, (tile - n_ctx_tiles) % ts)
        last = jnp.where(tile < n_ctx_tiles, tc - 1, ts - 1)
        return pos == 0, pos == last


def _row_tile(geom, want):
    tm = want
    while geom.n_ctx % tm or geom.S % tm:
        tm //= 2
    return tm


def _halo_specs(geom, tm, halo, width):
    per = tm // halo
    last = geom.N // halo - 1
    return [pl.BlockSpec((halo, width), lambda i: (jnp.maximum(i * per - 1, 0), 0)),
            pl.BlockSpec((tm, width), lambda i: (i, 0)),
            pl.BlockSpec((halo, width), lambda i: (jnp.minimum((i + 1) * per, last), 0))]


def _assemble_body(n_ctx_tiles, ctx_ref, x_ref, o_ref):
    o_ref[...] = jnp.where(pl.program_id(0) < n_ctx_tiles, ctx_ref[...], x_ref[...])


def _assemble(geom, ctx2, x2):
    d = x2.shape[1]
    tm = _row_tile(geom, 1024)
    nct = geom.n_ctx // tm
    return pl.pallas_call(
        functools.partial(_assemble_body, nct),
        grid=(geom.N // tm,),
        in_specs=[pl.BlockSpec((tm, d), lambda i: (jnp.minimum(i, nct - 1), 0)),
                  pl.BlockSpec((tm, d), lambda i: (jnp.maximum(i - nct, 0), 0))],
        out_specs=pl.BlockSpec((tm, d), lambda i: (i, 0)),
        out_shape=jax.ShapeDtypeStruct((geom.N, d), F32),
        compiler_params=_cparams("arbitrary"),
        name="assemble_tokens",
    )(ctx2, x2)


def _rope(x, cos, sin_a, sin_b):
    return x * cos + pltpu.roll(x, LANES - 16, 1) * sin_a + pltpu.roll(x, 16, 1) * sin_b


def _proj_body(geom, tm, xp_ref, x_ref, xn_ref, mod_ref, g_ref, rope_ref, bgt_ref, wc_ref,
               wdw_ref, bdw_ref, gln_ref, bln_ref,
               wqkv_ref, wglu_ref, wqkm_ref, wvgt_ref, wom_ref, wbr_ref,
               q_ref, k_ref, vt_ref, cact_ref, qm_ref, km_ref, vmt_ref, om_ref, gatet_ref, br_ref,
               pad_ref, sh_ref):
    d = x_ref.shape[1]
    mod = mod_ref[0]
    g, sc, sh = g_ref[...], mod[:, d:2 * d], mod[:, 0:d]
    h = _norm_mod(x_ref[...], g, sc, sh).astype(BF16)
    cos, sa, sb = rope_ref[:, 0:LANES], rope_ref[:, LANES:2 * LANES], rope_ref[:, 2 * LANES:3 * LANES]

    halo = CONV_HALO
    first, last = geom.seq_edges(pl.program_id(0), tm)
    h_halo = _norm_mod(jnp.concatenate([xp_ref[...], xn_ref[...]], axis=0), g, sc, sh).astype(BF16)
    h_ext = jnp.concatenate([h, h_halo], axis=0)

    glu = _dot(h_ext, wglu_ref[...])
    u = glu[:, :CONV_DIM] * _sigmoid(glu[:, CONV_DIM:])
    sh_ref[0, 0:halo, :] = jnp.where(first, 0.0, u[tm:tm + halo])
    sh_ref[0, halo:halo + tm, :] = u[0:tm]
    sh_ref[0, halo + tm:2 * halo + tm, :] = jnp.where(last, 0.0, u[tm + halo:tm + 2 * halo])
    span = tm + 2 * halo - SUBLANES
    for s_ in range(1, SUBLANES):
        sh_ref[s_, 0:span, :] = sh_ref[0, s_:s_ + span, :]
    rows = 64
    for r0 in range(0, tm, rows):
        acc = None
        for j in range(CONV_WIDTH):
            a_, s_ = divmod(halo + j - CONV_WIDTH // 2, SUBLANES)
            off = r0 + a_ * SUBLANES
            win = sh_ref[s_, off:off + rows, :].reshape(rows // SUBLANES, SUBLANES, -1)
            t = win * wdw_ref[j * SUBLANES:(j + 1) * SUBLANES, :][None]
            acc = t if acc is None else acc + t
        yv = acc.reshape(rows, -1) + bdw_ref[...]
        mu = jnp.mean(yv, axis=-1, keepdims=True)
        yc = yv - mu
        var = jnp.mean(yc * yc, axis=-1, keepdims=True)
        z = yc * lax.rsqrt(var + EPS) * gln_ref[...] + bln_ref[...]
        cact_ref[r0:r0 + rows, :] = _silu(z).astype(BF16)

    qkm = _dot(h_ext, wqkm_ref[...])
    pad_ref[0:halo, :] = jnp.where(first, 0.0, qkm[tm:tm + halo])
    pad_ref[halo:halo + tm, :] = qkm[0:tm]
    pad_ref[halo + tm:2 * halo + tm, :] = jnp.where(last, 0.0, qkm[tm + halo:tm + 2 * halo])
    acc = None
    for j in range(M_SHORT_CONV):
        off = halo + j - M_SHORT_CONV // 2
        t = pad_ref[off:off + tm, :] * wc_ref[j:j + 1, :]
        acc = t if acc is None else acc + t
    y = _silu(acc)
    qm_ref[...] = y[:, :M_WIDTH].astype(BF16)
    km_ref[...] = (y[:, M_WIDTH:] * M_HEAD_DIM ** -0.5).astype(BF16)

    qkv = _dot(h, wqkv_ref[...])
    scale = HEAD_DIM ** -0.5 * LOG2E
    for j in range(ATT_Q // LANES):
        sl = slice(j * LANES, (j + 1) * LANES)
        q_ref[:, sl] = _rope(qkv[:, sl] * scale, cos, sa, sb).astype(BF16)
    lo = lax.broadcasted_iota(jnp.int32, (tm, LANES), 1) < HEAD_DIM
    keys = _rope(qkv[:, ATT_Q:ATT_Q + ATT_KV], cos, sa, sb)
    swapped = pltpu.roll(keys, HEAD_DIM, 1)
    k_ref[:, 0:LANES] = jnp.where(lo, keys, swapped).astype(BF16)
    k_ref[:, LANES:2 * LANES] = jnp.where(lo, swapped, keys).astype(BF16)

    vgt = lax.dot_general(wvgt_ref[...], h, _NT, preferred_element_type=F32)
    vmt_ref[...] = vgt[:M_WIDTH].astype(BF16)
    vt_ref[...] = vgt[M_WIDTH:M_WIDTH + KV_W].astype(BF16)
    gatet_ref[...] = vgt[M_WIDTH + KV_W:] + bgt_ref[...]
    om_ref[...] = _dot(h, wom_ref[...]).astype(BF16)
    br_ref[...] = _dot(h, wbr_ref[...]).astype(BF16)


def _projection(geom, x, mod3, g, rope, b_gate, w_mconv, conv, w):
    n, d = x.shape
    tm = _row_tile(geom, 256)
    row = lambda i: (i, 0)
    col = lambda i: (0, i)
    mod_map = lambda i: (geom.mod_row(i, tm), 0, 0)
    rope_map = lambda i: (geom.seq_tile(i, tm), 0)
    outs = ((ATT_Q, BF16, False), (KV_W, BF16, False), (KV_W, BF16, True), (CONV_DIM, BF16, False),
            (M_WIDTH, BF16, False), (M_WIDTH, BF16, False), (M_WIDTH, BF16, True), (M_WIDTH, BF16, False),
            (N_GATE_COLS, F32, True), (3 * d, BF16, False))
    weights = (w["qkv"], w["glu"], w["qkm"], w["vgt"], w["om"], w["br"])
    return pl.pallas_call(
        functools.partial(_proj_body, geom, tm),
        grid=(n // tm,),
        in_specs=_halo_specs(geom, tm, CONV_HALO, d)
                 + [pl.BlockSpec((1, 1, mod3.shape[2]), mod_map),
                    _resident((1, d)),
                    pl.BlockSpec((tm, rope.shape[1]), rope_map),
                    _resident((N_GATE_COLS, 1)), _resident(w_mconv.shape)]
                 + [_resident(a.shape) for a in conv]
                 + [_resident(a.shape) for a in weights],
        out_specs=[pl.BlockSpec((c, tm), col) if fm else pl.BlockSpec((tm, c), row) for c, _, fm in outs],
        out_shape=[jax.ShapeDtypeStruct((c, n) if fm else (n, c), t) for c, t, fm in outs],
        scratch_shapes=[pltpu.VMEM((tm + 2 * CONV_HALO, 2 * M_WIDTH), F32),
                        pltpu.VMEM((SUBLANES, tm + 2 * CONV_HALO, CONV_DIM), F32)],
        compiler_params=_cparams("parallel"),
        name="in_proj",
    )(x, x, x, mod3, g, rope, b_gate.reshape(N_GATE_COLS, 1), w_mconv, *conv, *weights)


def _att_body(geom, sink_ref, q_ref, kp_ref, kc_ref, kn_ref, kx_ref, vp_ref, vc_ref, vn_ref, vx_ref, o_ref):
    blk = ATT_BLOCK
    nsub = ATT_STEP // blk
    step = pl.program_id(1) - geom.Lc // ATT_STEP
    group = N_Q_HEADS // N_KV_HEADS
    kk = lax.broadcasted_iota(jnp.int32, (3 * blk, blk), 0)
    qq = lax.broadcasted_iota(jnp.int32, (3 * blk, blk), 1)
    lo = lax.broadcasted_iota(jnp.int32, (blk, LANES), 1) < HEAD_DIM
    zero = jnp.zeros((blk, LANES), BF16)
    k_loc = jnp.concatenate([kp_ref[...], kc_ref[...], kn_ref[...]], axis=0)
    vt_loc = jnp.concatenate([vp_ref[...], vc_ref[...], vn_ref[...]], axis=1)
    k_ctx, vt_ctx = kx_ref[...], vx_ref[...]
    chains = []
    for sub in range(nsub):
        n = step * nsub + sub
        key_pos = (n - 1) * blk + kk
        q_pos = n * blk + qq
        valid = (jnp.abs(key_pos - q_pos) <= WINDOW) & (key_pos >= 0) & (key_pos < geom.S) & (n >= 0)
        bias = jnp.where(valid, 0.0, NEG_INF)
        bias = jnp.concatenate([bias] * group, axis=1)
        rows = slice(sub * blk, (sub + 1) * blk)
        win = slice(sub * blk, (sub + 3) * blk)
        for hk in range(N_KV_HEADS):
            ks = slice(hk * LANES, (hk + 1) * LANES)
            tiles = [q_ref[rows, (hk * group + 2 * p) * HEAD_DIM:(hk * group + 2 * p + 2) * HEAD_DIM]
                     for p in range(group // 2)]
            qs = jnp.concatenate([jnp.where(lo if half == 0 else ~lo, t, zero)
                                  for t in tiles for half in (0, 1)], axis=0)
            s = jnp.concatenate(
                [lax.dot_general(k_loc[win, ks], qs, _NT, preferred_element_type=F32) + bias,
                 lax.dot_general(k_ctx[:, ks], qs, _NT, preferred_element_type=F32)], axis=0)
            chains.append((rows, win, hk, ks, s))

    stage2 = []
    for rows, win, hk, ks, s in chains:
        sk = jnp.concatenate(
            [jnp.full((1, blk), sink_ref[hk * group + j] * LOG2E, F32) for j in range(group)], axis=1)
        m = jnp.maximum(jnp.max(s, axis=0, keepdims=True), sk)
        e = jnp.exp2(s - m)
        den = jnp.sum(e, axis=0, keepdims=True) + jnp.exp2(sk - m)
        stage2.append((rows, win, hk, ks, e.astype(BF16), den))

    for rows, win, hk, ks, e, den in stage2:
        vt = jnp.concatenate([vt_loc[ks, win], vt_ctx[ks, :]], axis=1)
        o = (_dot(vt, e) * (1.0 / den)).T
        for p in range(group // 2):
            pair = jnp.where(lo, o[2 * p * blk:(2 * p + 1) * blk], o[(2 * p + 1) * blk:(2 * p + 2) * blk])
            c0 = (hk * group + 2 * p) * HEAD_DIM
            o_ref[rows, c0:c0 + LANES] = pair.astype(BF16)


def _attention(geom, sink, q, k, vt):
    blk, stp = ATT_BLOCK, ATT_STEP
    nsub = stp // blk
    ncs, nss = geom.Lc // stp, geom.S // stp
    nsb = geom.S // blk
    base_s, base_b = geom.n_ctx // stp, geom.n_ctx // blk

    def q_blk(b, i):
        return jnp.where(i < ncs, b * ncs + i, base_s + b * nss + i - ncs)

    def edge_blk(off):
        return lambda b, i: base_b + b * nsb + jnp.clip((i - ncs) * nsub + off, 0, nsb - 1)

    cur_blk = lambda b, i: base_s + b * nss + jnp.clip(i - ncs, 0, nss - 1)
    ctx_blk = lambda b, i: b
    blocks = ((blk, edge_blk(-1)), (stp, cur_blk), (blk, edge_blk(nsub)), (geom.Lc, ctx_blk))
    rows = lambda f: (lambda b, i: (f(b, i), 0))
    cols = lambda f: (lambda b, i: (0, f(b, i)))
    return pl.pallas_call(
        functools.partial(_att_body, geom),
        grid=(geom.B, ncs + nss),
        in_specs=[pl.BlockSpec(memory_space=pltpu.SMEM),
                  pl.BlockSpec((stp, ATT_Q), rows(q_blk))]
                 + [pl.BlockSpec((size, KV_W), rows(f)) for size, f in blocks]
                 + [pl.BlockSpec((KV_W, size), cols(f)) for size, f in blocks],
        out_specs=pl.BlockSpec((stp, ATT_Q), rows(q_blk)),
        out_shape=jax.ShapeDtypeStruct(q.shape, BF16),
        compiler_params=_cparams("parallel", "parallel"),
        name="window_attention",
    )(sink, q, k, k, k, k, vt, vt, vt, vt)


def _split3(x):
    hi = x.astype(BF16)
    r1 = x - hi.astype(F32)
    mid = r1.astype(BF16)
    lo = (r1 - mid.astype(F32)).astype(BF16)
    return hi, mid, lo


def _mlstm_body(qf, kf, vf, gtf, qb, kb, vb, gtb, of, ob, st_ref, m_ref):
    @pl.when(pl.program_id(1) == 0)
    def _():
        st_ref[...] = jnp.zeros_like(st_ref)
        m_ref[...] = jnp.full_like(m_ref, NEG_INF)

    L = M_CHUNK
    r = lax.broadcasted_iota(jnp.int32, (L, L), 0)
    c = lax.broadcasted_iota(jnp.int32, (L, L), 1)
    ones = jnp.ones((M_AUG, L), BF16)
    refs = ((qf, kf, vf, gtf, of), (qb, kb, vb, gtb, ob))
    prep = []
    for direction in range(2):
        keep = (r <= c) if direction == 0 else (r >= c)
        tri = jnp.where(keep, 1.0, 0.0).astype(BF16)
        gates_t = refs[direction][3][...]
        hi, mid, lo = _split3(_log_sigmoid(gates_t))
        brow_all = _dot(hi, tri) + _dot(mid, tri) + _dot(lo, tri)
        u0 = direction * M_HEADS
        c_rows = gates_t[u0:u0 + M_HEADS] - brow_all[2 * M_HEADS + u0:2 * M_HEADS + u0 + M_HEADS]
        c_cols = jnp.concatenate([c_rows, jnp.zeros((LANES - M_HEADS, L), F32)], axis=0).T
        prep.append((keep, gates_t, brow_all, c_cols))

    units = []
    for direction in range(2):
        keep, gates_t, brow_all, c_cols = prep[direction]
        q_ref, k_ref = refs[direction][0], refs[direction][1]
        for h in range(M_HEADS):
            u = direction * M_HEADS + h
            hs = slice(h * M_HEAD_DIM, (h + 1) * M_HEAD_DIM)
            q, k = q_ref[:, hs], k_ref[:, hs]
            b_row = brow_all[2 * M_HEADS + u:2 * M_HEADS + u + 1, :]
            i_row = gates_t[u:u + 1, :]
            m_prev = m_ref[u:u + 1, 0:1]
            state = st_ref[u]
            dmat = jnp.where(keep, b_row + c_cols[:, h:h + 1], NEG_INF)
            a_row = b_row + m_prev
            mt = jnp.maximum(a_row, jnp.max(dmat, axis=0, keepdims=True))
            big = lax.dot_general(jnp.concatenate([k, state.astype(BF16)], axis=0), q, _NT,
                                  preferred_element_type=F32)
            units.append((direction, u, hs, k, b_row, i_row, m_prev, state, dmat, a_row, mt, big))

    units2 = []
    for (direction, u, hs, k, b_row, i_row, m_prev, state, dmat, a_row, mt, big) in units:
        vt_aug = jnp.concatenate([refs[direction][2][hs, :], ones], axis=0)
        w_row = jnp.exp(a_row - mt)
        smat = (big[:L] * jnp.exp(dmat - mt)).astype(BF16)
        tot = w_row * big[L:] + _dot(vt_aug, smat)
        den = tot[M_HEAD_DIM:M_HEAD_DIM + 1, :]
        scale = 1.0 / jnp.maximum(jnp.abs(den), jnp.exp(-mt))
        refs[direction][4][:, hs] = (tot[:M_HEAD_DIM] * scale).T
        units2.append((direction, u, k, b_row, i_row, m_prev, state, vt_aug))

    for (direction, u, k, b_row, i_row, m_prev, state, vt_aug) in units2:
        end = L - 1 if direction == 0 else 0
        total = b_row[:, end:end + 1]
        g_row = total - b_row + i_row
        m_new = jnp.maximum(total + m_prev, jnp.max(g_row, axis=1, keepdims=True))
        decay = jnp.exp(total + m_prev - m_new)
        vw = (vt_aug.astype(F32) * jnp.exp(g_row - m_new)).astype(BF16)
        st_ref[u] = decay * state + _dot(vw, k)
        m_ref[u:u + 1, :] = jnp.broadcast_to(m_new, (1, LANES))


def _mlstm_scan(geom, q, k, vt, gates_t):
    L = M_CHUNK
    ncc, nsc = geom.Lc // L, geom.S // L
    base = geom.n_ctx // L

    def fwd(b, j):
        return jnp.where(j < ncc, b * ncc + j, base + b * nsc + j - ncc)

    def bwd(b, j):
        return jnp.where(j < ncc, b * ncc + ncc - 1 - j, base + b * nsc + nsc - 1 - (j - ncc))

    def specs(m):
        rows = lambda b, j: (m(b, j), 0)
        cols = lambda b, j: (0, m(b, j))
        return [pl.BlockSpec((L, M_WIDTH), rows), pl.BlockSpec((L, M_WIDTH), rows),
                pl.BlockSpec((M_WIDTH, L), cols), pl.BlockSpec((N_GATE_COLS, L), cols)]

    n = q.shape[0]
    return pl.pallas_call(
        _mlstm_body,
        grid=(geom.B, ncc + nsc),
        in_specs=specs(fwd) + specs(bwd),
        out_specs=[pl.BlockSpec((L, M_WIDTH), lambda b, j: (fwd(b, j), 0)),
                   pl.BlockSpec((L, M_WIDTH), lambda b, j: (bwd(b, j), 0))],
        out_shape=[jax.ShapeDtypeStruct((n, M_WIDTH), F32)] * 2,
        scratch_shapes=[pltpu.VMEM((2 * M_HEADS, M_HEAD_DIM + M_AUG, M_HEAD_DIM), F32),
                        pltpu.VMEM((2 * M_HEADS, LANES), F32)],
        compiler_params=_cparams("parallel", "arbitrary"),
        name="mlstm_scan",
    )(q, k, vt, gates_t, q, k, vt, gates_t)


def _mix_ffn_body(final, x_ref, mod_ref, att_ref, cact_ref, hf_ref, hb_ref, om_ref, br_ref, gm_ref,
                  g_ref, gf_ref, wa_ref, wp_ref, wm_ref, wo_ref, wg_ref, wu_ref, wd_ref, o_ref):
    d = x_ref.shape[1]
    mod = mod_ref[0]
    hm = _sigmoid(om_ref[...].astype(F32)) * (hf_ref[...] + hb_ref[...])
    parts = []
    for h in range(M_HEADS):
        t = hm[:, h * M_HEAD_DIM:(h + 1) * M_HEAD_DIM]
        mu = jnp.mean(t, axis=-1, keepdims=True)
        tc = t - mu
        var = jnp.mean(tc * tc, axis=-1, keepdims=True)
        parts.append(tc * lax.rsqrt(var + EPS))
    hn = (jnp.concatenate(parts, axis=-1) * gm_ref[...]).astype(BF16)
    ya = _dot(att_ref[...], wa_ref[...])
    yb = _dot(cact_ref[...], wp_ref[...])
    yc = _dot(hn, wm_ref[...])
    merged = (_sigmoid(br_ref[:, 0:d].astype(F32)) * ya + _sigmoid(br_ref[:, d:2 * d].astype(F32)) * yb
              + _sigmoid(br_ref[:, 2 * d:3 * d].astype(F32)) * yc)
    x = x_ref[...] + mod[:, 2 * d:3 * d] * _dot(merged.astype(BF16), wo_ref[...])
    h = _norm_mod(x, g_ref[...], mod[:, 4 * d:5 * d], mod[:, 3 * d:4 * d]).astype(BF16)
    ff = wg_ref.shape[1]
    half = -(-ff // (2 * MXU_TILE)) * MXU_TILE
    acc = None
    for sl in (slice(0, half), slice(half, ff)):
        act = (_silu(_dot(h, wg_ref[:, sl])) * _dot(h, wu_ref[:, sl])).astype(BF16)
        t = _dot(act, wd_ref[sl, :])
        acc = t if acc is None else acc + t
    y = x + mod[:, 5 * d:6 * d] * acc
    o_ref[...] = _rms(y) * gf_ref[...] if final else y


def _mix_ffn(geom, x, mod3, att, cact, hf, hb, om, br, g_mnorm, g_ffn, g_final, w, latent_only):
    n, d = x.shape
    tm = _row_tile(geom, 256)
    skip = geom.n_ctx // tm if latent_only else 0
    row = lambda i: (i + skip, 0)
    mod_map = lambda i: (geom.mod_row(i + skip, tm), 0, 0)
    weights = (w["att_out"], w["pw"], w["mout"], w["out"], w["ff_gate"], w["ff_up"], w["ff_down"])
    return pl.pallas_call(
        functools.partial(_mix_ffn_body, latent_only),
        grid=(n // tm - skip,),
        in_specs=[pl.BlockSpec((tm, d), row),
                  pl.BlockSpec((1, 1, mod3.shape[2]), mod_map),
                  pl.BlockSpec((tm, ATT_Q), row), pl.BlockSpec((tm, CONV_DIM), row),
                  pl.BlockSpec((tm, M_WIDTH), row), pl.BlockSpec((tm, M_WIDTH), row),
                  pl.BlockSpec((tm, M_WIDTH), row), pl.BlockSpec((tm, 3 * d), row),
                  _resident((1, M_WIDTH)), _resident((1, d)), _resident((1, d))]
                 + [_resident(a.shape) for a in weights],
        out_specs=pl.BlockSpec((tm, d), lambda i: (i, 0)),
        out_shape=jax.ShapeDtypeStruct((n - skip * tm, d), F32),
        compiler_params=_cparams("parallel"),
        name="mix_ffn",
    )(x, mod3, att, cact, hf, hb, om, br, g_mnorm, g_ffn, g_final, *weights)


def _rope_table(geom):
    p = jnp.arange(geom.S)
    n_freq = HEAD_DIM // 4
    inv_freq = ROPE_BASE ** (-jnp.arange(n_freq, dtype=F32) / n_freq)
    ang_r = (p // GRID_W).astype(F32)[:, None] * inv_freq
    ang_c = (p % GRID_W).astype(F32)[:, None] * inv_freq
    ang = jnp.concatenate([ang_r, ang_r, ang_c, ang_c], axis=-1)
    reps = LANES // HEAD_DIM
    cos = jnp.tile(jnp.cos(ang), (1, reps))
    sin = jnp.tile(jnp.sin(ang), (1, reps))
    first_half = (jnp.arange(LANES) % (HEAD_DIM // 2)) < HEAD_DIM // 4
    lat = jnp.concatenate([cos, jnp.where(first_half, -sin, 0.0), jnp.where(first_half, 0.0, sin)], axis=1)
    ident = jnp.concatenate([jnp.ones((geom.Lc, LANES), F32), jnp.zeros((geom.Lc, 2 * LANES), F32)], axis=1)
    return jnp.concatenate([ident, lat], axis=0)


def _pack_layer(l, d, w_in, w_att_out, w_conv_pw, w_mlstm_out, w_out, w_ff_gate, w_ff_up, w_ff_down):
    sizes = (ATT_Q + 2 * ATT_KV, 2 * CONV_DIM, 2 * M_WIDTH, M_WIDTH, M_WIDTH, N_GATE_COLS, 3 * d)
    names = ("qkv", "glu", "qkm", "vm", "om", "gate", "br")
    w = {}
    start = 0
    for name, size in zip(names, sizes):
        w[name] = w_in[l, :, start:start + size].astype(BF16)
        start += size
    v_cols = w["qkv"][:, ATT_Q + ATT_KV:].reshape(d, N_KV_HEADS, 1, HEAD_DIM)
    v_dup = jnp.broadcast_to(v_cols, (d, N_KV_HEADS, 2, HEAD_DIM)).reshape(d, KV_W)
    w["vgt"] = jnp.concatenate([w.pop("vm"), v_dup, w["gate"]], axis=1).T
    w["qkv"] = w["qkv"][:, :ATT_Q + ATT_KV]
    del w["gate"]
    w["att_out"] = w_att_out[l].astype(BF16)
    w["pw"] = w_conv_pw[l].astype(BF16)
    w["mout"] = w_mlstm_out[l].astype(BF16)
    w["out"] = w_out[l].astype(BF16)
    w["ff_gate"] = w_ff_gate[l].astype(BF16)
    w["ff_up"] = w_ff_up[l].astype(BF16)
    w["ff_down"] = w_ff_down[l].astype(BF16)
    return w


def kernel(x, c, ctx, c_ctx, w_ada, b_ada, g_norm_mix, g_norm_ffn, w_in, b_mgate, att_sink, w_att_out, w_conv_dw, b_conv_dw, g_conv_ln, b_conv_ln, w_conv_pw, w_mconv, g_mlstm_norm, w_mlstm_out, w_out, w_ff_gate, w_ff_up, w_ff_down, g_final):
    batch, seq, d = x.shape
    ctx_len = ctx.shape[1]
    depth = w_ada.shape[0]
    geom = _Geom(batch, seq, ctx_len)
    assert seq % ATT_STEP == 0 and ctx_len % ATT_STEP == 0 and seq % GRID_W == 0
    assert seq % M_CHUNK == 0 and ctx_len % M_CHUNK == 0 and d % LANES == 0
    assert LANES == 2 * HEAD_DIM and ATT_KV == LANES

    mod_rows = -(-(batch + 1) // SUBLANES) * SUBLANES
    cvec = jnp.zeros((mod_rows, d), F32).at[:batch].set(c).at[batch].set(c_ctx)
    mod = _modulation(cvec, w_ada, b_ada)
    rope = _rope_table(geom)
    xs = _assemble(geom, ctx.reshape(geom.n_ctx, d), x.reshape(batch * seq, d))
    g_fin = g_final.reshape(1, d)

    for l in range(depth):
        last = l == depth - 1
        w = _pack_layer(l, d, w_in, w_att_out, w_conv_pw, w_mlstm_out, w_out, w_ff_gate, w_ff_up, w_ff_down)
        mod3 = mod[l].reshape(mod_rows, 1, 6 * d)
        conv = (jnp.repeat(w_conv_dw[l], SUBLANES, axis=0), b_conv_dw[l].reshape(1, -1),
                g_conv_ln[l].reshape(1, -1), b_conv_ln[l].reshape(1, -1))
        q, k, vt, cact, qm, km, vmt, om, gates_t, br = _projection(
            geom, xs, mod3, g_norm_mix[l].reshape(1, d), rope, b_mgate[l], w_mconv[l], conv, w)
        att = _attention(geom, att_sink[l], q, k, vt)
        hf, hb = _mlstm_scan(geom, qm, km, vmt, gates_t)
        xs = _mix_ffn(geom, xs, mod3, att, cact, hf, hb, om, br, g_mlstm_norm[l].reshape(1, -1),
                      g_norm_ffn[l].reshape(1, d), g_fin, w, last)

    return xs.reshape(batch, seq, d)
```

```python
import functools

import jax
import jax.numpy as jnp
from jax import lax
from jax.experimental import pallas as pl
from jax.experimental.pallas import tpu as pltpu

F32 = jnp.float32
BF16 = jnp.bfloat16

GRID_W = 64
N_Q_HEADS = 8
N_KV_HEADS = 2
HEAD_DIM = 64
WINDOW = 128
ATT_BLOCK = 128
ROPE_BASE = 10000.0
ATT_Q = N_Q_HEADS * HEAD_DIM
ATT_KV = N_KV_HEADS * HEAD_DIM
CONV_DIM = 512
CONV_WIDTH = 31
M_HEADS = 4
M_HEAD_DIM = 128
M_WIDTH = M_HEADS * M_HEAD_DIM
M_SHORT_CONV = 3
N_GATE_COLS = 4 * M_HEADS
EPS = 1e-6
NEG_INF = -1e30
LOG2E = 1.4426950408889634

LANES = 128
SUBLANES = 8
MXU_TILE = 256
M_CHUNK = 256
M_AUG = 16
ATT_STEP = 256
CONV_HALO = 16
KV_W = N_KV_HEADS * LANES
VMEM_LIMIT = 52 * 1024 * 1024


def _cparams(*sem):
    return pltpu.CompilerParams(dimension_semantics=sem, vmem_limit_bytes=VMEM_LIMIT)


def _resident(shape):
    nd = len(shape)
    return pl.BlockSpec(shape, lambda *_: (0,) * nd, pipeline_mode=pl.Buffered(1))


def _layer_resident(arr, l):
    nd = arr.ndim
    return pl.BlockSpec((None,) + arr.shape[1:], lambda *_: (l,) + (0,) * (nd - 1), pipeline_mode=pl.Buffered(1))


def _sigmoid(x):
    return 1.0 / (1.0 + jnp.exp(-x))


def _silu(x):
    return x * _sigmoid(x)


def _log_sigmoid(x):
    return jnp.minimum(x, 0.0) - jnp.log(1.0 + jnp.exp(-jnp.abs(x)))


def _rms(x):
    return x * lax.rsqrt(jnp.mean(x * x, axis=-1, keepdims=True) + EPS)


def _norm_mod(x, g, sc, sh):
    return (_rms(x) * g) * (1.0 + sc) + sh


def _dot(a, b):
    return jnp.dot(a, b, preferred_element_type=F32)


_NT = (((1,), (1,)), ((), ()))


def _mod_body(c_ref, w_ref, b_ref, o_ref):
    h = _silu(c_ref[...]).astype(BF16)
    o_ref[0] = _dot(h, w_ref[0].astype(BF16)) + b_ref[0]


def _modulation(cvec, w_ada, b_ada):
    depth, d, n6 = w_ada.shape
    rows = cvec.shape[0]
    tn = n6 // 4
    return pl.pallas_call(
        _mod_body,
        grid=(depth, n6 // tn),
        in_specs=[pl.BlockSpec((rows, d), lambda l, j: (0, 0)),
                  pl.BlockSpec((1, d, tn), lambda l, j: (l, 0, j)),
                  pl.BlockSpec((1, 1, tn), lambda l, j: (l, 0, j))],
        out_specs=pl.BlockSpec((1, rows, tn), lambda l, j: (l, 0, j)),
        out_shape=jax.ShapeDtypeStruct((depth, rows, n6), F32),
        compiler_params=_cparams("arbitrary", "arbitrary"),
        name="ada_mod",
    )(cvec, w_ada, b_ada.reshape(depth, 1, n6))


class _Geom:
    def __init__(self, batch, seq, ctx_len):
        self.B, self.S, self.Lc = batch, seq, ctx_len
        self.n_ctx = batch * ctx_len
        self.N = self.n_ctx + batch * seq

    def mod_row(self, tile, tm):
        n_ctx_tiles = self.n_ctx // tm
        return jnp.where(tile < n_ctx_tiles, self.B, (tile - n_ctx_tiles) // (self.S // tm))

    def seq_tile(self, tile, tm):
        n_ctx_tiles = self.n_ctx // tm
        tc, ts = self.Lc // tm, self.S // tm
        return jnp.where(tile < n_ctx_tiles, tile % tc, tc + (tile - n_ctx_tiles) % ts)

    def seq_edges(self, tile, tm):
        n_ctx_tiles = self.n_ctx // tm
        tc, ts = self.Lc // tm, self.S // tm
        pos = jnp.where(tile < n_ctx_tiles, tile % tc, (tile - n_ctx_tiles) % ts)
        last = jnp.where(tile < n_ctx_tiles, tc - 1, ts - 1)
        return pos == 0, pos == last


def _row_tile(geom, want):
    tm = want
    while geom.n_ctx % tm or geom.S % tm:
        tm //= 2
    return tm


def _halo_specs(geom, tm, halo, width):
    per = tm // halo
    last = geom.N // halo - 1
    return [pl.BlockSpec((halo, width), lambda i: (jnp.maximum(i * per - 1, 0), 0)),
            pl.BlockSpec((tm, width), lambda i: (i, 0)),
            pl.BlockSpec((halo, width), lambda i: (jnp.minimum((i + 1) * per, last), 0))]


def _assemble_body(n_ctx_tiles, ctx_ref, x_ref, o_ref):
    o_ref[...] = jnp.where(pl.program_id(0) < n_ctx_tiles, ctx_ref[...], x_ref[...])


def _assemble(geom, ctx2, x2):
    d = x2.shape[1]
    tm = _row_tile(geom, 1024)
    nct = geom.n_ctx // tm
    return pl.pallas_call(
        functools.partial(_assemble_body, nct),
        grid=(geom.N // tm,),
        in_specs=[pl.BlockSpec((tm, d), lambda i: (jnp.minimum(i, nct - 1), 0)),
                  pl.BlockSpec((tm, d), lambda i: (jnp.maximum(i - nct, 0), 0))],
        out_specs=pl.BlockSpec((tm, d), lambda i: (i, 0)),
        out_shape=jax.ShapeDtypeStruct((geom.N, d), F32),
        compiler_params=_cparams("arbitrary"),
        name="assemble_tokens",
    )(ctx2, x2)


def _rope(x, cos, sin_a, sin_b):
    return x * cos + pltpu.roll(x, LANES - 16, 1) * sin_a + pltpu.roll(x, 16, 1) * sin_b


def _proj_body(geom, tm, xp_ref, x_ref, xn_ref, mod_ref, g_ref, rope_ref, bgt_ref, wc_ref,
               wdw_ref, bdw_ref, gln_ref, bln_ref,
               wtok_ref, wfeat_ref,
               q_ref, k_ref, vt_ref, cact_ref, qm_ref, km_ref, vmt_ref, om_ref, gatet_ref, br_ref,
               pad_ref, sh_ref):
    d = x_ref.shape[1]
    mod = mod_ref[0]
    g, sc, sh = g_ref[...], mod[:, d:2 * d], mod[:, 0:d]
    h = _norm_mod(x_ref[...], g, sc, sh).astype(BF16)
    cos, sa, sb = rope_ref[:, 0:LANES], rope_ref[:, LANES:2 * LANES], rope_ref[:, 2 * LANES:3 * LANES]

    halo = CONV_HALO
    first, last = geom.seq_edges(pl.program_id(0), tm)
    h_halo = _norm_mod(jnp.concatenate([xp_ref[...], xn_ref[...]], axis=0), g, sc, sh).astype(BF16)
    h_ext = jnp.concatenate([h, h_halo], axis=0)

    glu = _dot(h_ext, wtok_ref[:, _TOK["glu"]])
    u = glu[:, :CONV_DIM] * _sigmoid(glu[:, CONV_DIM:])
    sh_ref[0, 0:halo, :] = jnp.where(first, 0.0, u[tm:tm + halo])
    sh_ref[0, halo:halo + tm, :] = u[0:tm]
    sh_ref[0, halo + tm:2 * halo + tm, :] = jnp.where(last, 0.0, u[tm + halo:tm + 2 * halo])
    span = tm + 2 * halo - SUBLANES
    for s_ in range(1, SUBLANES):
        sh_ref[s_, 0:span, :] = sh_ref[0, s_:s_ + span, :]
    rows = 64
    for r0 in range(0, tm, rows):
        acc = None
        for j in range(CONV_WIDTH):
            a_, s_ = divmod(halo + j - CONV_WIDTH // 2, SUBLANES)
            off = r0 + a_ * SUBLANES
            win = sh_ref[s_, off:off + rows, :].reshape(rows // SUBLANES, SUBLANES, -1)
            t = win * wdw_ref[j * SUBLANES:(j + 1) * SUBLANES, :][None]
            acc = t if acc is None else acc + t
        yv = acc.reshape(rows, -1) + bdw_ref[...]
        mu = jnp.mean(yv, axis=-1, keepdims=True)
        yc = yv - mu
        var = jnp.mean(yc * yc, axis=-1, keepdims=True)
        z = yc * lax.rsqrt(var + EPS) * gln_ref[...] + bln_ref[...]
        cact_ref[r0:r0 + rows, :] = _silu(z).astype(BF16)

    qkm = _dot(h_ext, wtok_ref[:, _TOK["qkm"]])
    pad_ref[0:halo, :] = jnp.where(first, 0.0, qkm[tm:tm + halo])
    pad_ref[halo:halo + tm, :] = qkm[0:tm]
    pad_ref[halo + tm:2 * halo + tm, :] = jnp.where(last, 0.0, qkm[tm + halo:tm + 2 * halo])
    acc = None
    for j in range(M_SHORT_CONV):
        off = halo + j - M_SHORT_CONV // 2
        t = pad_ref[off:off + tm, :] * wc_ref[j:j + 1, :]
        acc = t if acc is None else acc + t
    y = _silu(acc)
    qm_ref[...] = y[:, :M_WIDTH].astype(BF16)
    km_ref[...] = (y[:, M_WIDTH:] * M_HEAD_DIM ** -0.5).astype(BF16)

    qkv = _dot(h, wtok_ref[:, _TOK["qk"]])
    scale = HEAD_DIM ** -0.5 * LOG2E
    for j in range(ATT_Q // LANES):
        sl = slice(j * LANES, (j + 1) * LANES)
        q_ref[:, sl] = _rope(qkv[:, sl] * scale, cos, sa, sb).astype(BF16)
    lo = lax.broadcasted_iota(jnp.int32, (tm, LANES), 1) < HEAD_DIM
    keys = _rope(qkv[:, ATT_Q:ATT_Q + ATT_KV], cos, sa, sb)
    swapped = pltpu.roll(keys, HEAD_DIM, 1)
    k_ref[:, 0:LANES] = jnp.where(lo, keys, swapped).astype(BF16)
    k_ref[:, LANES:2 * LANES] = jnp.where(lo, swapped, keys).astype(BF16)

    vgt = lax.dot_general(wfeat_ref[...], h, _NT, preferred_element_type=F32)
    vmt_ref[...] = vgt[:M_WIDTH].astype(BF16)
    vt_ref[...] = vgt[M_WIDTH:M_WIDTH + KV_W].astype(BF16)
    gatet_ref[...] = vgt[M_WIDTH + KV_W:] + bgt_ref[...]
    om_ref[...] = _dot(h, wtok_ref[:, _TOK["om"]]).astype(BF16)
    br_ref[...] = _dot(h, wtok_ref[:, _TOK["br"]]).astype(BF16)


def _projection(geom, l, x, mod3, g, rope, b_gate, w_mconv, conv, w):
    n, d = x.shape
    tm = _row_tile(geom, 256)
    row = lambda i: (i, 0)
    col = lambda i: (0, i)
    mod_map = lambda i: (geom.mod_row(i, tm), 0, 0)
    rope_map = lambda i: (geom.seq_tile(i, tm), 0)
    outs = ((ATT_Q, BF16, False), (KV_W, BF16, False), (KV_W, BF16, True), (CONV_DIM, BF16, False),
            (M_WIDTH, BF16, False), (M_WIDTH, BF16, False), (M_WIDTH, BF16, True), (M_WIDTH, BF16, False),
            (N_GATE_COLS, F32, True), (3 * d, BF16, False))
    weights = (w["tok"], w["feat"])
    return pl.pallas_call(
        functools.partial(_proj_body, geom, tm),
        grid=(n // tm,),
        in_specs=_halo_specs(geom, tm, CONV_HALO, d)
                 + [pl.BlockSpec((1, 1, mod3.shape[2]), mod_map),
                    _resident((1, d)),
                    pl.BlockSpec((tm, rope.shape[1]), rope_map),
                    _resident((N_GATE_COLS, 1)), _resident(w_mconv.shape)]
                 + [_resident(a.shape) for a in conv]
                 + [_layer_resident(a, l) for a in weights],
        out_specs=[pl.BlockSpec((c, tm), col) if fm else pl.BlockSpec((tm, c), row) for c, _, fm in outs],
        out_shape=[jax.ShapeDtypeStruct((c, n) if fm else (n, c), t) for c, t, fm in outs],
        scratch_shapes=[pltpu.VMEM((tm + 2 * CONV_HALO, 2 * M_WIDTH), F32),
                        pltpu.VMEM((SUBLANES, tm + 2 * CONV_HALO, CONV_DIM), F32)],
        compiler_params=_cparams("parallel"),
        name="in_proj",
    )(x, x, x, mod3, g, rope, b_gate.reshape(N_GATE_COLS, 1), w_mconv, *conv, *weights)


def _att_body(geom, sink_ref, q_ref, kp_ref, kc_ref, kn_ref, kx_ref, vp_ref, vc_ref, vn_ref, vx_ref, o_ref):
    blk = ATT_BLOCK
    nsub = ATT_STEP // blk
    step = pl.program_id(1) - geom.Lc // ATT_STEP
    group = N_Q_HEADS // N_KV_HEADS
    kk = lax.broadcasted_iota(jnp.int32, (3 * blk, blk), 0)
    qq = lax.broadcasted_iota(jnp.int32, (3 * blk, blk), 1)
    lo = lax.broadcasted_iota(jnp.int32, (blk, LANES), 1) < HEAD_DIM
    zero = jnp.zeros((blk, LANES), BF16)
    k_loc = jnp.concatenate([kp_ref[...], kc_ref[...], kn_ref[...]], axis=0)
    vt_loc = jnp.concatenate([vp_ref[...], vc_ref[...], vn_ref[...]], axis=1)
    k_ctx, vt_ctx = kx_ref[...], vx_ref[...]
    chains = []
    for sub in range(nsub):
        n = step * nsub + sub
        key_pos = (n - 1) * blk + kk
        q_pos = n * blk + qq
        valid = (jnp.abs(key_pos - q_pos) <= WINDOW) & (key_pos >= 0) & (key_pos < geom.S) & (n >= 0)
        bias = jnp.where(valid, 0.0, NEG_INF)
        bias = jnp.concatenate([bias] * group, axis=1)
        rows = slice(sub * blk, (sub + 1) * blk)
        win = slice(sub * blk, (sub + 3) * blk)
        for hk in range(N_KV_HEADS):
            ks = slice(hk * LANES, (hk + 1) * LANES)
            tiles = [q_ref[rows, (hk * group + 2 * p) * HEAD_DIM:(hk * group + 2 * p + 2) * HEAD_DIM]
                     for p in range(group // 2)]
            qs = jnp.concatenate([jnp.where(lo if half == 0 else ~lo, t, zero)
                                  for t in tiles for half in (0, 1)], axis=0)
            s = jnp.concatenate(
                [lax.dot_general(k_loc[win, ks], qs, _NT, preferred_element_type=F32) + bias,
                 lax.dot_general(k_ctx[:, ks], qs, _NT, preferred_element_type=F32)], axis=0)
            chains.append((rows, win, hk, ks, s))

    stage2 = []
    for rows, win, hk, ks, s in chains:
        sk = jnp.concatenate(
            [jnp.full((1, blk), sink_ref[hk * group + j] * LOG2E, F32) for j in range(group)], axis=1)
        m = jnp.maximum(jnp.max(s, axis=0, keepdims=True), sk)
        e = jnp.exp2(s - m)
        den = jnp.sum(e, axis=0, keepdims=True) + jnp.exp2(sk - m)
        stage2.append((rows, win, hk, ks, e.astype(BF16), den))

    for rows, win, hk, ks, e, den in stage2:
        vt = jnp.concatenate([vt_loc[ks, win], vt_ctx[ks, :]], axis=1)
        o = (_dot(vt, e) * (1.0 / den)).T
        for p in range(group // 2):
            pair = jnp.where(lo, o[2 * p * blk:(2 * p + 1) * blk], o[(2 * p + 1) * blk:(2 * p + 2) * blk])
            c0 = (hk * group + 2 * p) * HEAD_DIM
            o_ref[rows, c0:c0 + LANES] = pair.astype(BF16)


def _attention(geom, sink, q, k, vt):
    blk, stp = ATT_BLOCK, ATT_STEP
    nsub = stp // blk
    ncs, nss = geom.Lc // stp, geom.S // stp
    nsb = geom.S // blk
    base_s, base_b = geom.n_ctx // stp, geom.n_ctx // blk

    def q_blk(b, i):
        return jnp.where(i < ncs, b * ncs + i, base_s + b * nss + i - ncs)

    def edge_blk(off):
        return lambda b, i: base_b + b * nsb + jnp.clip((i - ncs) * nsub + off, 0, nsb - 1)

    cur_blk = lambda b, i: base_s + b * nss + jnp.clip(i - ncs, 0, nss - 1)
    ctx_blk = lambda b, i: b
    blocks = ((blk, edge_blk(-1)), (stp, cur_blk), (blk, edge_blk(nsub)), (geom.Lc, ctx_blk))
    rows = lambda f: (lambda b, i: (f(b, i), 0))
    cols = lambda f: (lambda b, i: (0, f(b, i)))
    return pl.pallas_call(
        functools.partial(_att_body, geom),
        grid=(geom.B, ncs + nss),
        in_specs=[pl.BlockSpec(memory_space=pltpu.SMEM),
                  pl.BlockSpec((stp, ATT_Q), rows(q_blk))]
                 + [pl.BlockSpec((size, KV_W), rows(f)) for size, f in blocks]
                 + [pl.BlockSpec((KV_W, size), cols(f)) for size, f in blocks],
        out_specs=pl.BlockSpec((stp, ATT_Q), rows(q_blk)),
        out_shape=jax.ShapeDtypeStruct(q.shape, BF16),
        compiler_params=_cparams("parallel", "parallel"),
        name="window_attention",
    )(sink, q, k, k, k, k, vt, vt, vt, vt)


def _split3(x):
    hi = x.astype(BF16)
    r1 = x - hi.astype(F32)
    mid = r1.astype(BF16)
    lo = (r1 - mid.astype(F32)).astype(BF16)
    return hi, mid, lo


def _mlstm_body(qf, kf, vf, gtf, qb, kb, vb, gtb, of, ob, st_ref, m_ref):
    @pl.when(pl.program_id(1) == 0)
    def _():
        st_ref[...] = jnp.zeros_like(st_ref)
        m_ref[...] = jnp.full_like(m_ref, NEG_INF)

    L = M_CHUNK
    r = lax.broadcasted_iota(jnp.int32, (L, L), 0)
    c = lax.broadcasted_iota(jnp.int32, (L, L), 1)
    ones = jnp.ones((M_AUG, L), BF16)
    refs = ((qf, kf, vf, gtf, of), (qb, kb, vb, gtb, ob))
    prep = []
    for direction in range(2):
        keep = (r <= c) if direction == 0 else (r >= c)
        tri = jnp.where(keep, 1.0, 0.0).astype(BF16)
        gates_t = refs[direction][3][...]
        hi, mid, lo = _split3(_log_sigmoid(gates_t))
        brow_all = _dot(hi, tri) + _dot(mid, tri) + _dot(lo, tri)
        u0 = direction * M_HEADS
        c_rows = gates_t[u0:u0 + M_HEADS] - brow_all[2 * M_HEADS + u0:2 * M_HEADS + u0 + M_HEADS]
        c_cols = jnp.concatenate([c_rows, jnp.zeros((LANES - M_HEADS, L), F32)], axis=0).T
        prep.append((keep, gates_t, brow_all, c_cols))

    units = []
    for direction in range(2):
        keep, gates_t, brow_all, c_cols = prep[direction]
        q_ref, k_ref = refs[direction][0], refs[direction][1]
        for h in range(M_HEADS):
            u = direction * M_HEADS + h
            hs = slice(h * M_HEAD_DIM, (h + 1) * M_HEAD_DIM)
            q, k = q_ref[:, hs], k_ref[:, hs]
            b_row = brow_all[2 * M_HEADS + u:2 * M_HEADS + u + 1, :]
            i_row = gates_t[u:u + 1, :]
            m_prev = m_ref[u:u + 1, 0:1]
            state = st_ref[u]
            dmat = jnp.where(keep, b_row + c_cols[:, h:h + 1], NEG_INF)
            a_row = b_row + m_prev
            mt = jnp.maximum(a_row, jnp.max(dmat, axis=0, keepdims=True))
            big = lax.dot_general(jnp.concatenate([k, state.astype(BF16)], axis=0), q, _NT,
                                  preferred_element_type=F32)
            units.append((direction, u, hs, k, b_row, i_row, m_prev, state, dmat, a_row, mt, big))

    units2 = []
    for (direction, u, hs, k, b_row, i_row, m_prev, state, dmat, a_row, mt, big) in units:
        vt_aug = jnp.concatenate([refs[direction][2][hs, :], ones], axis=0)
        w_row = jnp.exp(a_row - mt)
        smat = (big[:L] * jnp.exp(dmat - mt)).astype(BF16)
        tot = w_row * big[L:] + _dot(vt_aug, smat)
        den = tot[M_HEAD_DIM:M_HEAD_DIM + 1, :]
        scale = 1.0 / jnp.maximum(jnp.abs(den), jnp.exp(-mt))
        refs[direction][4][:, hs] = (tot[:M_HEAD_DIM] * scale).T
        units2.append((direction, u, k, b_row, i_row, m_prev, state, vt_aug))

    for (direction, u, k, b_row, i_row, m_prev, state, vt_aug) in units2:
        end = L - 1 if direction == 0 else 0
        total = b_row[:, end:end + 1]
        g_row = total - b_row + i_row
        m_new = jnp.maximum(total + m_prev, jnp.max(g_row, axis=1, keepdims=True))
        decay = jnp.exp(total + m_prev - m_new)
        vw = (vt_aug.astype(F32) * jnp.exp(g_row - m_new)).astype(BF16)
        st_ref[u] = decay * state + _dot(vw, k)
        m_ref[u:u + 1, :] = jnp.broadcast_to(m_new, (1, LANES))


def _mlstm_scan(geom, q, k, vt, gates_t):
    L = M_CHUNK
    ncc, nsc = geom.Lc // L, geom.S // L
    base = geom.n_ctx // L

    def fwd(b, j):
        return jnp.where(j < ncc, b * ncc + j, base + b * nsc + j - ncc)

    def bwd(b, j):
        return jnp.where(j < ncc, b * ncc + ncc - 1 - j, base + b * nsc + nsc - 1 - (j - ncc))

    def specs(m):
        rows = lambda b, j: (m(b, j), 0)
        cols = lambda b, j: (0, m(b, j))
        return [pl.BlockSpec((L, M_WIDTH), rows), pl.BlockSpec((L, M_WIDTH), rows),
                pl.BlockSpec((M_WIDTH, L), cols), pl.BlockSpec((N_GATE_COLS, L), cols)]

    n = q.shape[0]
    return pl.pallas_call(
        _mlstm_body,
        grid=(geom.B, ncc + nsc),
        in_specs=specs(fwd) + specs(bwd),
        out_specs=[pl.BlockSpec((L, M_WIDTH), lambda b, j: (fwd(b, j), 0)),
                   pl.BlockSpec((L, M_WIDTH), lambda b, j: (bwd(b, j), 0))],
        out_shape=[jax.ShapeDtypeStruct((n, M_WIDTH), F32)] * 2,
        scratch_shapes=[pltpu.VMEM((2 * M_HEADS, M_HEAD_DIM + M_AUG, M_HEAD_DIM), F32),
                        pltpu.VMEM((2 * M_HEADS, LANES), F32)],
        compiler_params=_cparams("parallel", "arbitrary"),
        name="mlstm_scan",
    )(q, k, vt, gates_t, q, k, vt, gates_t)


def _mix_ffn_body(final, x_ref, mod_ref, att_ref, cact_ref, hf_ref, hb_ref, om_ref, br_ref, gm_ref,
                  g_ref, gf_ref, wa_ref, wp_ref, wm_ref, wo_ref, wg_ref, wu_ref, wd_ref, o_ref):
    d = x_ref.shape[1]
    mod = mod_ref[0]
    hm = _sigmoid(om_ref[...].astype(F32)) * (hf_ref[...] + hb_ref[...])
    parts = []
    for h in range(M_HEADS):
        t = hm[:, h * M_HEAD_DIM:(h + 1) * M_HEAD_DIM]
        mu = jnp.mean(t, axis=-1, keepdims=True)
        tc = t - mu
        var = jnp.mean(tc * tc, axis=-1, keepdims=True)
        parts.append(tc * lax.rsqrt(var + EPS))
    hn = (jnp.concatenate(parts, axis=-1) * gm_ref[...]).astype(BF16)
    ya = _dot(att_ref[...], wa_ref[...])
    yb = _dot(cact_ref[...], wp_ref[...])
    yc = _dot(hn, wm_ref[...])
    merged = (_sigmoid(br_ref[:, 0:d].astype(F32)) * ya + _sigmoid(br_ref[:, d:2 * d].astype(F32)) * yb
              + _sigmoid(br_ref[:, 2 * d:3 * d].astype(F32)) * yc)
    x = x_ref[...] + mod[:, 2 * d:3 * d] * _dot(merged.astype(BF16), wo_ref[...])
    h = _norm_mod(x, g_ref[...], mod[:, 4 * d:5 * d], mod[:, 3 * d:4 * d]).astype(BF16)
    ff = wg_ref.shape[1]
    half = -(-ff // (2 * MXU_TILE)) * MXU_TILE
    acc = None
    for sl in (slice(0, half), slice(half, ff)):
        act = (_silu(_dot(h, wg_ref[:, sl])) * _dot(h, wu_ref[:, sl])).astype(BF16)
        t = _dot(act, wd_ref[sl, :])
        acc = t if acc is None else acc + t
    y = x + mod[:, 5 * d:6 * d] * acc
    o_ref[...] = _rms(y) * gf_ref[...] if final else y


def _mix_ffn(geom, l, x, mod3, att, cact, hf, hb, om, br, g_mnorm, g_ffn, g_final, w, latent_only):
    n, d = x.shape
    tm = _row_tile(geom, 256)
    skip = geom.n_ctx // tm if latent_only else 0
    row = lambda i: (i + skip, 0)
    mod_map = lambda i: (geom.mod_row(i + skip, tm), 0, 0)
    weights = (w["att_out"], w["pw"], w["mout"], w["out"], w["ff_gate"], w["ff_up"], w["ff_down"])
    return pl.pallas_call(
        functools.partial(_mix_ffn_body, latent_only),
        grid=(n // tm - skip,),
        in_specs=[pl.BlockSpec((tm, d), row),
                  pl.BlockSpec((1, 1, mod3.shape[2]), mod_map),
                  pl.BlockSpec((tm, ATT_Q), row), pl.BlockSpec((tm, CONV_DIM), row),
                  pl.BlockSpec((tm, M_WIDTH), row), pl.BlockSpec((tm, M_WIDTH), row),
                  pl.BlockSpec((tm, M_WIDTH), row), pl.BlockSpec((tm, 3 * d), row),
                  _resident((1, M_WIDTH)), _resident((1, d)), _resident((1, d))]
                 + [_layer_resident(a, l) for a in weights],
        out_specs=pl.BlockSpec((tm, d), lambda i: (i, 0)),
        out_shape=jax.ShapeDtypeStruct((n - skip * tm, d), F32),
        compiler_params=_cparams("parallel"),
        name="mix_ffn",
    )(x, mod3, att, cact, hf, hb, om, br, g_mnorm, g_ffn, g_final, *weights)


def _rope_table(geom):
    p = jnp.arange(geom.S)
    n_freq = HEAD_DIM // 4
    inv_freq = ROPE_BASE ** (-jnp.arange(n_freq, dtype=F32) / n_freq)
    ang_r = (p // GRID_W).astype(F32)[:, None] * inv_freq
    ang_c = (p % GRID_W).astype(F32)[:, None] * inv_freq
    ang = jnp.concatenate([ang_r, ang_r, ang_c, ang_c], axis=-1)
    reps = LANES // HEAD_DIM
    cos = jnp.tile(jnp.cos(ang), (1, reps))
    sin = jnp.tile(jnp.sin(ang), (1, reps))
    first_half = (jnp.arange(LANES) % (HEAD_DIM // 2)) < HEAD_DIM // 4
    lat = jnp.concatenate([cos, jnp.where(first_half, -sin, 0.0), jnp.where(first_half, 0.0, sin)], axis=1)
    ident = jnp.concatenate([jnp.ones((geom.Lc, LANES), F32), jnp.zeros((geom.Lc, 2 * LANES), F32)], axis=1)
    return jnp.concatenate([ident, lat], axis=0)


_TOK_SIZES = (("glu", 2 * CONV_DIM), ("qkm", 2 * M_WIDTH), ("qk", ATT_Q + ATT_KV), ("om", M_WIDTH))
_TOK = {}
_off = 0
for _name, _size in _TOK_SIZES:
    _TOK[_name] = slice(_off, _off + _size)
    _off += _size
_TOK["br"] = slice(_off, None)
assert all(sl.start % LANES == 0 for sl in _TOK.values())


def _pack_weights(d, w_in, w_att_out, w_conv_pw, w_mlstm_out, w_out, w_ff_gate, w_ff_up, w_ff_down):
    wb = w_in.astype(BF16)
    sizes = (ATT_Q + ATT_KV, ATT_KV, 2 * CONV_DIM, 2 * M_WIDTH, M_WIDTH, M_WIDTH, N_GATE_COLS, 3 * d)
    names = ("qk", "v", "glu", "qkm", "vm", "om", "gate", "br")
    seg = {}
    start = 0
    for name, size in zip(names, sizes):
        seg[name] = wb[:, :, start:start + size]
        start += size
    depth = w_in.shape[0]
    v_cols = seg["v"].reshape(depth, d, N_KV_HEADS, 1, HEAD_DIM)
    v_dup = jnp.broadcast_to(v_cols, (depth, d, N_KV_HEADS, 2, HEAD_DIM)).reshape(depth, d, KV_W)
    return {
        "tok": jnp.concatenate([seg[name] for name, _ in _TOK_SIZES] + [seg["br"]], axis=2),
        "feat": jnp.concatenate([seg["vm"], v_dup, seg["gate"]], axis=2).transpose(0, 2, 1),
        "att_out": w_att_out.astype(BF16), "pw": w_conv_pw.astype(BF16), "mout": w_mlstm_out.astype(BF16),
        "out": w_out.astype(BF16), "ff_gate": w_ff_gate.astype(BF16), "ff_up": w_ff_up.astype(BF16),
        "ff_down": w_ff_down.astype(BF16),
    }


def kernel(x, c, ctx, c_ctx, w_ada, b_ada, g_norm_mix, g_norm_ffn, w_in, b_mgate, att_sink, w_att_out, w_conv_dw, b_conv_dw, g_conv_ln, b_conv_ln, w_conv_pw, w_mconv, g_mlstm_norm, w_mlstm_out, w_out, w_ff_gate, w_ff_up, w_ff_down, g_final):
    batch, seq, d = x.shape
    ctx_len = ctx.shape[1]
    depth = w_ada.shape[0]
    geom = _Geom(batch, seq, ctx_len)
    assert seq % ATT_STEP == 0 and ctx_len % ATT_STEP == 0 and seq % GRID_W == 0
    assert seq % M_CHUNK == 0 and ctx_len % M_CHUNK == 0 and d % LANES == 0
    assert LANES == 2 * HEAD_DIM and ATT_KV == LANES

    mod_rows = -(-(batch + 1) // SUBLANES) * SUBLANES
    cvec = jnp.zeros((mod_rows, d), F32).at[:batch].set(c).at[batch].set(c_ctx)
    mod = _modulation(cvec, w_ada, b_ada)
    rope = _rope_table(geom)
    xs = _assemble(geom, ctx.reshape(geom.n_ctx, d), x.reshape(batch * seq, d))
    g_fin = g_final.reshape(1, d)
    w = _pack_weights(d, w_in, w_att_out, w_conv_pw, w_mlstm_out, w_out, w_ff_gate, w_ff_up, w_ff_down)

    for l in range(depth):
        last = l == depth - 1
        mod3 = mod[l].reshape(mod_rows, 1, 6 * d)
        conv = (jnp.repeat(w_conv_dw[l], SUBLANES, axis=0), b_conv_dw[l].reshape(1, -1),
                g_conv_ln[l].reshape(1, -1), b_conv_ln[l].reshape(1, -1))
        q, k, vt, cact, qm, km, vmt, om, gates_t, br = _projection(
            geom, l, xs, mod3, g_norm_mix[l].reshape(1, d), rope, b_mgate[l], w_mconv[l], conv, w)
        att = _attention(geom, att_sink[l], q, k, vt)
        hf, hb = _mlstm_scan(geom, qm, km, vmt, gates_t)
        xs = _mix_ffn(geom, l, xs, mod3, att, cact, hf, hb, om, br, g_mlstm_norm[l].reshape(1, -1),
                      g_norm_ffn[l].reshape(1, d), g_fin, w, last)

    return xs.reshape(batch, seq, d)
```

```python
import functools

import jax
import jax.numpy as jnp
from jax import lax
from jax.experimental import pallas as pl
from jax.experimental.pallas import tpu as pltpu

F32 = jnp.float32
BF16 = jnp.bfloat16

GRID_W = 64
N_Q_HEADS = 8
N_KV_HEADS = 2
HEAD_DIM = 64
WINDOW = 128
ATT_BLOCK = 128
ROPE_BASE = 10000.0
ATT_Q = N_Q_HEADS * HEAD_DIM
ATT_KV = N_KV_HEADS * HEAD_DIM
CONV_DIM = 512
CONV_WIDTH = 31
M_HEADS = 4
M_HEAD_DIM = 128
M_WIDTH = M_HEADS * M_HEAD_DIM
M_SHORT_CONV = 3
N_GATE_COLS = 4 * M_HEADS
EPS = 1e-6
NEG_INF = -1e30
LOG2E = 1.4426950408889634

LANES = 128
SUBLANES = 8
MXU_TILE = 256
M_CHUNK = 256
M_AUG = 16
ATT_STEP = 256
CONV_HALO = 16
KV_W = N_KV_HEADS * LANES
VMEM_LIMIT = 52 * 1024 * 1024


def _cparams(*sem):
    return pltpu.CompilerParams(dimension_semantics=sem, vmem_limit_bytes=VMEM_LIMIT)


def _resident(shape):
    nd = len(shape)
    return pl.BlockSpec(shape, lambda *_: (0,) * nd, pipeline_mode=pl.Buffered(1))


def _layer_resident(arr, l):
    nd = arr.ndim
    return pl.BlockSpec((None,) + arr.shape[1:], lambda *_: (l,) + (0,) * (nd - 1), pipeline_mode=pl.Buffered(1))


def _sigmoid(x):
    return 1.0 / (1.0 + jnp.exp(-x))


def _silu(x):
    return x * _sigmoid(x)


def _log_sigmoid(x):
    return jnp.minimum(x, 0.0) - jnp.log(1.0 + jnp.exp(-jnp.abs(x)))


def _rms(x):
    return x * lax.rsqrt(jnp.mean(x * x, axis=-1, keepdims=True) + EPS)


def _norm_mod(x, g, sc, sh):
    return (_rms(x) * g) * (1.0 + sc) + sh


def _dot(a, b):
    return jnp.dot(a, b, preferred_element_type=F32)


_NT = (((1,), (1,)), ((), ()))


def _mod_body(c_ref, w_ref, b_ref, o_ref):
    h = _silu(c_ref[...]).astype(BF16)
    o_ref[0] = _dot(h, w_ref[0].astype(BF16)) + b_ref[0]


def _modulation(cvec, w_ada, b_ada):
    depth, d, n6 = w_ada.shape
    rows = cvec.shape[0]
    tn = n6 // 4
    return pl.pallas_call(
        _mod_body,
        grid=(depth, n6 // tn),
        in_specs=[pl.BlockSpec((rows, d), lambda l, j: (0, 0)),
                  pl.BlockSpec((1, d, tn), lambda l, j: (l, 0, j)),
                  pl.BlockSpec((1, 1, tn), lambda l, j: (l, 0, j))],
        out_specs=pl.BlockSpec((1, rows, tn), lambda l, j: (l, 0, j)),
        out_shape=jax.ShapeDtypeStruct((depth, rows, n6), F32),
        compiler_params=_cparams("arbitrary", "arbitrary"),
        name="ada_mod",
    )(cvec, w_ada, b_ada.reshape(depth, 1, n6))


class _Geom:
    def __init__(self, batch, seq, ctx_len):
        self.B, self.S, self.Lc = batch, seq, ctx_len
        self.n_ctx = batch * ctx_len
        self.N = self.n_ctx + batch * seq

    def mod_row(self, tile, tm):
        n_ctx_tiles = self.n_ctx // tm
        return jnp.where(tile < n_ctx_tiles, self.B, (tile - n_ctx_tiles) // (self.S // tm))

    def seq_tile(self, tile, tm):
        n_ctx_tiles = self.n_ctx // tm
        tc, ts = self.Lc // tm, self.S // tm
        return jnp.where(tile < n_ctx_tiles, tile % tc, tc + (tile - n_ctx_tiles) % ts)

    def seq_edges(self, tile, tm):
        n_ctx_tiles = self.n_ctx // tm
        tc, ts = self.Lc // tm, self.S // tm
        pos = jnp.where(tile < n_ctx_tiles, tile % tc, (tile - n_ctx_tiles) % ts)
        last = jnp.where(tile < n_ctx_tiles, tc - 1, ts - 1)
        return pos == 0, pos == last


def _row_tile(geom, want):
    tm = want
    while geom.n_ctx % tm or geom.S % tm:
        tm //= 2
    return tm


def _halo_specs(geom, tm, halo, width):
    per = tm // halo
    last = geom.N // halo - 1
    return [pl.BlockSpec((halo, width), lambda i: (jnp.maximum(i * per - 1, 0), 0)),
            pl.BlockSpec((tm, width), lambda i: (i, 0)),
            pl.BlockSpec((halo, width), lambda i: (jnp.minimum((i + 1) * per, last), 0))]


def _assemble_body(n_ctx_tiles, ctx_ref, x_ref, o_ref):
    o_ref[...] = jnp.where(pl.program_id(0) < n_ctx_tiles, ctx_ref[...], x_ref[...])


def _assemble(geom, ctx2, x2):
    d = x2.shape[1]
    tm = _row_tile(geom, 1024)
    nct = geom.n_ctx // tm
    return pl.pallas_call(
        functools.partial(_assemble_body, nct),
        grid=(geom.N // tm,),
        in_specs=[pl.BlockSpec((tm, d), lambda i: (jnp.minimum(i, nct - 1), 0)),
                  pl.BlockSpec((tm, d), lambda i: (jnp.maximum(i - nct, 0), 0))],
        out_specs=pl.BlockSpec((tm, d), lambda i: (i, 0)),
        out_shape=jax.ShapeDtypeStruct((geom.N, d), F32),
        compiler_params=_cparams("arbitrary"),
        name="assemble_tokens",
    )(ctx2, x2)


def _rope(x, cos, sin_a, sin_b):
    return x * cos + pltpu.roll(x, LANES - 16, 1) * sin_a + pltpu.roll(x, 16, 1) * sin_b


def _proj_body(geom, tm, xp_ref, x_ref, xn_ref, mod_ref, g_ref, rope_ref, bgt_ref, wc_ref,
               wdw_ref, bdw_ref, gln_ref, bln_ref,
               wtok_ref, wbr_ref, wfeat_ref,
               q_ref, k_ref, vt_ref, cact_ref, qm_ref, km_ref, vmt_ref, om_ref, gatet_ref, br_ref,
               pad_ref, sh_ref):
    d = x_ref.shape[1]
    mod = mod_ref[0]
    g, sc, sh = g_ref[...], mod[:, d:2 * d], mod[:, 0:d]
    h = _norm_mod(x_ref[...], g, sc, sh).astype(BF16)
    cos, sa, sb = rope_ref[:, 0:LANES], rope_ref[:, LANES:2 * LANES], rope_ref[:, 2 * LANES:3 * LANES]

    halo = CONV_HALO
    first, last = geom.seq_edges(pl.program_id(0), tm)
    h_halo = _norm_mod(jnp.concatenate([xp_ref[...], xn_ref[...]], axis=0), g, sc, sh).astype(BF16)
    h_ext = jnp.concatenate([h, h_halo], axis=0)

    glu = _dot(h_ext, wtok_ref[:, _TOK["glu"]])
    u = glu[:, :CONV_DIM] * _sigmoid(glu[:, CONV_DIM:])
    sh_ref[0, 0:halo, :] = jnp.where(first, 0.0, u[tm:tm + halo])
    sh_ref[0, halo:halo + tm, :] = u[0:tm]
    sh_ref[0, halo + tm:2 * halo + tm, :] = jnp.where(last, 0.0, u[tm + halo:tm + 2 * halo])
    span = tm + 2 * halo - SUBLANES
    for s_ in range(1, SUBLANES):
        sh_ref[s_, 0:span, :] = sh_ref[0, s_:s_ + span, :]
    rows = 64
    for r0 in range(0, tm, rows):
        acc = None
        for j in range(CONV_WIDTH):
            a_, s_ = divmod(halo + j - CONV_WIDTH // 2, SUBLANES)
            off = r0 + a_ * SUBLANES
            win = sh_ref[s_, off:off + rows, :].reshape(rows // SUBLANES, SUBLANES, -1)
            t = win * wdw_ref[j * SUBLANES:(j + 1) * SUBLANES, :][None]
            acc = t if acc is None else acc + t
        yv = acc.reshape(rows, -1) + bdw_ref[...]
        mu = jnp.mean(yv, axis=-1, keepdims=True)
        yc = yv - mu
        var = jnp.mean(yc * yc, axis=-1, keepdims=True)
        z = yc * lax.rsqrt(var + EPS) * gln_ref[...] + bln_ref[...]
        cact_ref[r0:r0 + rows, :] = _silu(z).astype(BF16)

    qkm = _dot(h_ext, wtok_ref[:, _TOK["qkm"]])
    pad_ref[0:halo, :] = jnp.where(first, 0.0, qkm[tm:tm + halo])
    pad_ref[halo:halo + tm, :] = qkm[0:tm]
    pad_ref[halo + tm:2 * halo + tm, :] = jnp.where(last, 0.0, qkm[tm + halo:tm + 2 * halo])
    acc = None
    for j in range(M_SHORT_CONV):
        off = halo + j - M_SHORT_CONV // 2
        t = pad_ref[off:off + tm, :] * wc_ref[j:j + 1, :]
        acc = t if acc is None else acc + t
    y = _silu(acc)
    qm_ref[...] = y[:, :M_WIDTH].astype(BF16)
    km_ref[...] = (y[:, M_WIDTH:] * M_HEAD_DIM ** -0.5).astype(BF16)

    qkv = _dot(h, wtok_ref[:, _TOK["qk"]])
    scale = HEAD_DIM ** -0.5 * LOG2E
    for j in range(ATT_Q // LANES):
        sl = slice(j * LANES, (j + 1) * LANES)
        q_ref[:, sl] = _rope(qkv[:, sl] * scale, cos, sa, sb).astype(BF16)
    lo = lax.broadcasted_iota(jnp.int32, (tm, LANES), 1) < HEAD_DIM
    keys = _rope(qkv[:, ATT_Q:ATT_Q + ATT_KV], cos, sa, sb)
    swapped = pltpu.roll(keys, HEAD_DIM, 1)
    k_ref[:, 0:LANES] = jnp.where(lo, keys, swapped).astype(BF16)
    k_ref[:, LANES:2 * LANES] = jnp.where(lo, swapped, keys).astype(BF16)

    vgt = lax.dot_general(wfeat_ref[...], h, _NT, preferred_element_type=F32)
    vmt_ref[...] = vgt[:M_WIDTH].astype(BF16)
    vt_ref[...] = vgt[M_WIDTH:M_WIDTH + ATT_KV].astype(BF16)
    gatet_ref[...] = vgt[M_WIDTH + ATT_KV:] + bgt_ref[...]
    om_ref[...] = _dot(h, wtok_ref[:, _TOK["om"]]).astype(BF16)
    br_ref[...] = _dot(h, wbr_ref[...]).astype(BF16)


def _projection(geom, l, x, mod3, g, rope, b_gate, w_mconv, conv, w):
    n, d = x.shape
    tm = _row_tile(geom, 256)
    row = lambda i: (i, 0)
    col = lambda i: (0, i)
    mod_map = lambda i: (geom.mod_row(i, tm), 0, 0)
    rope_map = lambda i: (geom.seq_tile(i, tm), 0)
    outs = ((ATT_Q, BF16, False), (KV_W, BF16, False), (ATT_KV, BF16, True), (CONV_DIM, BF16, False),
            (M_WIDTH, BF16, False), (M_WIDTH, BF16, False), (M_WIDTH, BF16, True), (M_WIDTH, BF16, False),
            (N_GATE_COLS, F32, True), (3 * d, BF16, False))
    tok = w["tok"]
    tok_spec = pl.BlockSpec((None, d, _TOK_W), lambda i: (l, 0, 0), pipeline_mode=pl.Buffered(1))
    return pl.pallas_call(
        functools.partial(_proj_body, geom, tm),
        grid=(n // tm,),
        in_specs=_halo_specs(geom, tm, CONV_HALO, d)
                 + [pl.BlockSpec((1, 1, mod3.shape[2]), mod_map),
                    _resident((1, d)),
                    pl.BlockSpec((tm, rope.shape[1]), rope_map),
                    _resident((N_GATE_COLS, 1)), _resident(w_mconv.shape)]
                 + [_resident(a.shape) for a in conv]
                 + [tok_spec, _layer_resident(w["br"], l), _layer_resident(w["feat"], l)],
        out_specs=[pl.BlockSpec((c, tm), col) if fm else pl.BlockSpec((tm, c), row) for c, _, fm in outs],
        out_shape=[jax.ShapeDtypeStruct((c, n) if fm else (n, c), t) for c, t, fm in outs],
        scratch_shapes=[pltpu.VMEM((tm + 2 * CONV_HALO, 2 * M_WIDTH), F32),
                        pltpu.VMEM((SUBLANES, tm + 2 * CONV_HALO, CONV_DIM), F32)],
        compiler_params=_cparams("parallel"),
        name="in_proj",
    )(x, x, x, mod3, g, rope, b_gate.reshape(N_GATE_COLS, 1), w_mconv, *conv, tok, w["br"], w["feat"])


def _att_body(geom, sink_ref, q_ref, kp_ref, kc_ref, kn_ref, kx_ref, vp_ref, vc_ref, vn_ref, vx_ref, o_ref):
    blk = ATT_BLOCK
    nsub = ATT_STEP // blk
    step = pl.program_id(1) - geom.Lc // ATT_STEP
    group = N_Q_HEADS // N_KV_HEADS
    kk = lax.broadcasted_iota(jnp.int32, (3 * blk, blk), 0)
    qq = lax.broadcasted_iota(jnp.int32, (3 * blk, blk), 1)
    lo = lax.broadcasted_iota(jnp.int32, (blk, LANES), 1) < HEAD_DIM
    zero = jnp.zeros((blk, LANES), BF16)
    k_loc = jnp.concatenate([kp_ref[...], kc_ref[...], kn_ref[...]], axis=0)
    vt_loc = jnp.concatenate([vp_ref[...], vc_ref[...], vn_ref[...]], axis=1)
    k_ctx, vt_ctx = kx_ref[...], vx_ref[...]
    chains = []
    for sub in range(nsub):
        n = step * nsub + sub
        key_pos = (n - 1) * blk + kk
        q_pos = n * blk + qq
        valid = (jnp.abs(key_pos - q_pos) <= WINDOW) & (key_pos >= 0) & (key_pos < geom.S) & (n >= 0)
        bias = jnp.where(valid, 0.0, NEG_INF)
        bias = jnp.concatenate([bias] * group, axis=1)
        rows = slice(sub * blk, (sub + 1) * blk)
        win = slice(sub * blk, (sub + 3) * blk)
        for hk in range(N_KV_HEADS):
            ks = slice(hk * LANES, (hk + 1) * LANES)
            tiles = [q_ref[rows, (hk * group + 2 * p) * HEAD_DIM:(hk * group + 2 * p + 2) * HEAD_DIM]
                     for p in range(group // 2)]
            qs = jnp.concatenate([jnp.where(lo if half == 0 else ~lo, t, zero)
                                  for t in tiles for half in (0, 1)], axis=0)
            s = jnp.concatenate(
                [lax.dot_general(k_loc[win, ks], qs, _NT, preferred_element_type=F32) + bias,
                 lax.dot_general(k_ctx[:, ks], qs, _NT, preferred_element_type=F32)], axis=0)
            chains.append((rows, win, hk, ks, s))

    stage2 = []
    for rows, win, hk, ks, s in chains:
        sk = jnp.concatenate(
            [jnp.full((1, blk), sink_ref[hk * group + j] * LOG2E, F32) for j in range(group)], axis=1)
        m = jnp.maximum(jnp.max(s, axis=0, keepdims=True), sk)
        e = jnp.exp2(s - m)
        den = jnp.sum(e, axis=0, keepdims=True) + jnp.exp2(sk - m)
        stage2.append((rows, win, hk, ks, e.astype(BF16), den))

    for rows, win, hk, ks, e, den in stage2:
        vs = slice(hk * HEAD_DIM, (hk + 1) * HEAD_DIM)
        vt = jnp.concatenate([vt_loc[vs, win], vt_ctx[vs, :]], axis=1)
        vt = jnp.concatenate([vt, vt], axis=0)
        o = (_dot(vt, e) * (1.0 / den)).T
        for p in range(group // 2):
            pair = jnp.where(lo, o[2 * p * blk:(2 * p + 1) * blk], o[(2 * p + 1) * blk:(2 * p + 2) * blk])
            c0 = (hk * group + 2 * p) * HEAD_DIM
            o_ref[rows, c0:c0 + LANES] = pair.astype(BF16)


def _attention(geom, sink, q, k, vt):
    blk, stp = ATT_BLOCK, ATT_STEP
    nsub = stp // blk
    ncs, nss = geom.Lc // stp, geom.S // stp
    nsb = geom.S // blk
    base_s, base_b = geom.n_ctx // stp, geom.n_ctx // blk

    def q_blk(b, i):
        return jnp.where(i < ncs, b * ncs + i, base_s + b * nss + i - ncs)

    def edge_blk(off):
        return lambda b, i: base_b + b * nsb + jnp.clip((i - ncs) * nsub + off, 0, nsb - 1)

    cur_blk = lambda b, i: base_s + b * nss + jnp.clip(i - ncs, 0, nss - 1)
    ctx_blk = lambda b, i: b
    blocks = ((blk, edge_blk(-1)), (stp, cur_blk), (blk, edge_blk(nsub)), (geom.Lc, ctx_blk))
    rows = lambda f: (lambda b, i: (f(b, i), 0))
    cols = lambda f: (lambda b, i: (0, f(b, i)))
    return pl.pallas_call(
        functools.partial(_att_body, geom),
        grid=(geom.B, ncs + nss),
        in_specs=[pl.BlockSpec(memory_space=pltpu.SMEM),
                  pl.BlockSpec((stp, ATT_Q), rows(q_blk))]
                 + [pl.BlockSpec((size, KV_W), rows(f)) for size, f in blocks]
                 + [pl.BlockSpec((ATT_KV, size), cols(f)) for size, f in blocks],
        out_specs=pl.BlockSpec((stp, ATT_Q), rows(q_blk)),
        out_shape=jax.ShapeDtypeStruct(q.shape, BF16),
        compiler_params=_cparams("parallel", "parallel"),
        name="window_attention",
    )(sink, q, k, k, k, k, vt, vt, vt, vt)


def _split3(x):
    hi = x.astype(BF16)
    r1 = x - hi.astype(F32)
    mid = r1.astype(BF16)
    lo = (r1 - mid.astype(F32)).astype(BF16)
    return hi, mid, lo


def _mlstm_body(qf, kf, vf, gtf, qb, kb, vb, gtb, of, ob, st_ref, m_ref):
    @pl.when(pl.program_id(1) == 0)
    def _():
        st_ref[...] = jnp.zeros_like(st_ref)
        m_ref[...] = jnp.full_like(m_ref, NEG_INF)

    L = M_CHUNK
    r = lax.broadcasted_iota(jnp.int32, (L, L), 0)
    c = lax.broadcasted_iota(jnp.int32, (L, L), 1)
    ones = jnp.ones((M_AUG, L), BF16)
    refs = ((qf, kf, vf, gtf, of), (qb, kb, vb, gtb, ob))
    prep = []
    for direction in range(2):
        keep = (r <= c) if direction == 0 else (r >= c)
        tri = jnp.where(keep, 1.0, 0.0).astype(BF16)
        gates_t = refs[direction][3][...]
        hi, mid, lo = _split3(_log_sigmoid(gates_t))
        brow_all = _dot(hi, tri) + _dot(mid, tri) + _dot(lo, tri)
        u0 = direction * M_HEADS
        c_rows = gates_t[u0:u0 + M_HEADS] - brow_all[2 * M_HEADS + u0:2 * M_HEADS + u0 + M_HEADS]
        c_cols = jnp.concatenate([c_rows, jnp.zeros((LANES - M_HEADS, L), F32)], axis=0).T
        prep.append((keep, gates_t, brow_all, c_cols))

    units = []
    for direction in range(2):
        keep, gates_t, brow_all, c_cols = prep[direction]
        q_ref, k_ref = refs[direction][0], refs[direction][1]
        for h in range(M_HEADS):
            u = direction * M_HEADS + h
            hs = slice(h * M_HEAD_DIM, (h + 1) * M_HEAD_DIM)
            q, k = q_ref[:, hs], k_ref[:, hs]
            b_row = brow_all[2 * M_HEADS + u:2 * M_HEADS + u + 1, :]
            i_row = gates_t[u:u + 1, :]
            m_prev = m_ref[u:u + 1, 0:1]
            state = st_ref[u]
            dmat = jnp.where(keep, b_row + c_cols[:, h:h + 1], NEG_INF)
            a_row = b_row + m_prev
            mt = jnp.maximum(a_row, jnp.max(dmat, axis=0, keepdims=True))
            big = lax.dot_general(jnp.concatenate([k, state.astype(BF16)], axis=0), q, _NT,
                                  preferred_element_type=F32)
            units.append((direction, u, hs, k, b_row, i_row, m_prev, state, dmat, a_row, mt, big))

    units2 = []
    for (direction, u, hs, k, b_row, i_row, m_prev, state, dmat, a_row, mt, big) in units:
        vt_aug = jnp.concatenate([refs[direction][2][hs, :], ones], axis=0)
        w_row = jnp.exp(a_row - mt)
        smat = (big[:L] * jnp.exp(dmat - mt)).astype(BF16)
        tot = w_row * big[L:] + _dot(vt_aug, smat)
        den = tot[M_HEAD_DIM:M_HEAD_DIM + 1, :]
        scale = 1.0 / jnp.maximum(jnp.abs(den), jnp.exp(-mt))
        refs[direction][4][:, hs] = (tot[:M_HEAD_DIM] * scale).T
        units2.append((direction, u, k, b_row, i_row, m_prev, state, vt_aug))

    for (direction, u, k, b_row, i_row, m_prev, state, vt_aug) in units2:
        end = L - 1 if direction == 0 else 0
        total = b_row[:, end:end + 1]
        g_row = total - b_row + i_row
        m_new = jnp.maximum(total + m_prev, jnp.max(g_row, axis=1, keepdims=True))
        decay = jnp.exp(total + m_prev - m_new)
        vw = (vt_aug.astype(F32) * jnp.exp(g_row - m_new)).astype(BF16)
        st_ref[u] = decay * state + _dot(vw, k)
        m_ref[u:u + 1, :] = jnp.broadcast_to(m_new, (1, LANES))


def _mlstm_scan(geom, q, k, vt, gates_t):
    L = M_CHUNK
    ncc, nsc = geom.Lc // L, geom.S // L
    base = geom.n_ctx // L

    def fwd(b, j):
        return jnp.where(j < ncc, b * ncc + j, base + b * nsc + j - ncc)

    def bwd(b, j):
        return jnp.where(j < ncc, b * ncc + ncc - 1 - j, base + b * nsc + nsc - 1 - (j - ncc))

    def specs(m):
        rows = lambda b, j: (m(b, j), 0)
        cols = lambda b, j: (0, m(b, j))
        return [pl.BlockSpec((L, M_WIDTH), rows), pl.BlockSpec((L, M_WIDTH), rows),
                pl.BlockSpec((M_WIDTH, L), cols), pl.BlockSpec((N_GATE_COLS, L), cols)]

    n = q.shape[0]
    return pl.pallas_call(
        _mlstm_body,
        grid=(geom.B, ncc + nsc),
        in_specs=specs(fwd) + specs(bwd),
        out_specs=[pl.BlockSpec((L, M_WIDTH), lambda b, j: (fwd(b, j), 0)),
                   pl.BlockSpec((L, M_WIDTH), lambda b, j: (bwd(b, j), 0))],
        out_shape=[jax.ShapeDtypeStruct((n, M_WIDTH), F32)] * 2,
        scratch_shapes=[pltpu.VMEM((2 * M_HEADS, M_HEAD_DIM + M_AUG, M_HEAD_DIM), F32),
                        pltpu.VMEM((2 * M_HEADS, LANES), F32)],
        compiler_params=_cparams("parallel", "arbitrary"),
        name="mlstm_scan",
    )(q, k, vt, gates_t, q, k, vt, gates_t)


def _mix_ffn_body(final, x_ref, mod_ref, att_ref, cact_ref, hf_ref, hb_ref, om_ref, br_ref, gm_ref,
                  g_ref, gf_ref, wa_ref, wp_ref, wm_ref, wo_ref, wg_ref, wu_ref, wd_ref, o_ref):
    d = x_ref.shape[1]
    mod = mod_ref[0]
    hm = _sigmoid(om_ref[...].astype(F32)) * (hf_ref[...] + hb_ref[...])
    parts = []
    for h in range(M_HEADS):
        t = hm[:, h * M_HEAD_DIM:(h + 1) * M_HEAD_DIM]
        mu = jnp.mean(t, axis=-1, keepdims=True)
        tc = t - mu
        var = jnp.mean(tc * tc, axis=-1, keepdims=True)
        parts.append(tc * lax.rsqrt(var + EPS))
    hn = (jnp.concatenate(parts, axis=-1) * gm_ref[...]).astype(BF16)
    ya = _dot(att_ref[...], wa_ref[...])
    yb = _dot(cact_ref[...], wp_ref[...])
    yc = _dot(hn, wm_ref[...])
    merged = (_sigmoid(br_ref[:, 0:d].astype(F32)) * ya + _sigmoid(br_ref[:, d:2 * d].astype(F32)) * yb
              + _sigmoid(br_ref[:, 2 * d:3 * d].astype(F32)) * yc)
    x = x_ref[...] + mod[:, 2 * d:3 * d] * _dot(merged.astype(BF16), wo_ref[...])
    h = _norm_mod(x, g_ref[...], mod[:, 4 * d:5 * d], mod[:, 3 * d:4 * d]).astype(BF16)
    ff = wg_ref.shape[1]
    half = -(-ff // (2 * MXU_TILE)) * MXU_TILE
    acc = None
    for sl in (slice(0, half), slice(half, ff)):
        act = (_silu(_dot(h, wg_ref[:, sl])) * _dot(h, wu_ref[:, sl])).astype(BF16)
        t = _dot(act, wd_ref[sl, :])
        acc = t if acc is None else acc + t
    y = x + mod[:, 5 * d:6 * d] * acc
    o_ref[...] = _rms(y) * gf_ref[...] if final else y


def _mix_ffn(geom, l, x, mod3, att, cact, hf, hb, om, br, g_mnorm, g_ffn, g_final, w, latent_only):
    n, d = x.shape
    tm = _row_tile(geom, 256)
    skip = geom.n_ctx // tm if latent_only else 0
    row = lambda i: (i + skip, 0)
    mod_map = lambda i: (geom.mod_row(i + skip, tm), 0, 0)
    weights = (w["att_out"], w["pw"], w["mout"], w["out"], w["ff_gate"], w["ff_up"], w["ff_down"])
    return pl.pallas_call(
        functools.partial(_mix_ffn_body, latent_only),
        grid=(n // tm - skip,),
        in_specs=[pl.BlockSpec((tm, d), row),
                  pl.BlockSpec((1, 1, mod3.shape[2]), mod_map),
                  pl.BlockSpec((tm, ATT_Q), row), pl.BlockSpec((tm, CONV_DIM), row),
                  pl.BlockSpec((tm, M_WIDTH), row), pl.BlockSpec((tm, M_WIDTH), row),
                  pl.BlockSpec((tm, M_WIDTH), row), pl.BlockSpec((tm, 3 * d), row),
                  _resident((1, M_WIDTH)), _resident((1, d)), _resident((1, d))]
                 + [_layer_resident(a, l) for a in weights],
        out_specs=pl.BlockSpec((tm, d), lambda i: (i, 0)),
        out_shape=jax.ShapeDtypeStruct((n - skip * tm, d), F32),
        compiler_params=_cparams("parallel"),
        name="mix_ffn",
    )(x, mod3, att, cact, hf, hb, om, br, g_mnorm, g_ffn, g_final, *weights)


def _rope_table(geom):
    p = jnp.arange(geom.S)
    n_freq = HEAD_DIM // 4
    inv_freq = ROPE_BASE ** (-jnp.arange(n_freq, dtype=F32) / n_freq)
    ang_r = (p // GRID_W).astype(F32)[:, None] * inv_freq
    ang_c = (p % GRID_W).astype(F32)[:, None] * inv_freq
    ang = jnp.concatenate([ang_r, ang_r, ang_c, ang_c], axis=-1)
    reps = LANES // HEAD_DIM
    cos = jnp.tile(jnp.cos(ang), (1, reps))
    sin = jnp.tile(jnp.sin(ang), (1, reps))
    first_half = (jnp.arange(LANES) % (HEAD_DIM // 2)) < HEAD_DIM // 4
    lat = jnp.concatenate([cos, jnp.where(first_half, -sin, 0.0), jnp.where(first_half, 0.0, sin)], axis=1)
    ident = jnp.concatenate([jnp.ones((geom.Lc, LANES), F32), jnp.zeros((geom.Lc, 2 * LANES), F32)], axis=1)
    return jnp.concatenate([ident, lat], axis=0)


_TOK = {"qk": slice(0, ATT_Q + ATT_KV)}
_off = ATT_Q + 2 * ATT_KV
for _name, _size in (("glu", 2 * CONV_DIM), ("qkm", 2 * M_WIDTH), ("vm", M_WIDTH), ("om", M_WIDTH)):
    _TOK[_name] = slice(_off, _off + _size)
    _off += _size
_TOK_W = _off
assert all(sl.start % LANES == 0 for sl in _TOK.values()) and _TOK_W % LANES == 0


def _pack_weights(d, w_in, w_att_out, w_conv_pw, w_mlstm_out, w_out, w_ff_gate, w_ff_up, w_ff_down):
    wb = w_in.astype(BF16)
    v0, g0 = ATT_Q + ATT_KV, _TOK_W
    feat = jnp.concatenate([wb[:, :, _TOK["vm"]], wb[:, :, v0:v0 + ATT_KV], wb[:, :, g0:g0 + N_GATE_COLS]], axis=2)
    return {
        "tok": wb, "br": wb[:, :, g0 + N_GATE_COLS:], "feat": feat.transpose(0, 2, 1),
        "att_out": w_att_out.astype(BF16), "pw": w_conv_pw.astype(BF16), "mout": w_mlstm_out.astype(BF16),
        "out": w_out.astype(BF16), "ff_gate": w_ff_gate.astype(BF16), "ff_up": w_ff_up.astype(BF16),
        "ff_down": w_ff_down.astype(BF16),
    }


def kernel(x, c, ctx, c_ctx, w_ada, b_ada, g_norm_mix, g_norm_ffn, w_in, b_mgate, att_sink, w_att_out, w_conv_dw, b_conv_dw, g_conv_ln, b_conv_ln, w_conv_pw, w_mconv, g_mlstm_norm, w_mlstm_out, w_out, w_ff_gate, w_ff_up, w_ff_down, g_final):
    batch, seq, d = x.shape
    ctx_len = ctx.shape[1]
    depth = w_ada.shape[0]
    geom = _Geom(batch, seq, ctx_len)
    assert seq % ATT_STEP == 0 and ctx_len % ATT_STEP == 0 and seq % GRID_W == 0
    assert seq % M_CHUNK == 0 and ctx_len % M_CHUNK == 0 and d % LANES == 0
    assert LANES == 2 * HEAD_DIM and ATT_KV == LANES

    mod_rows = -(-(batch + 1) // SUBLANES) * SUBLANES
    cvec = jnp.zeros((mod_rows, d), F32).at[:batch].set(c).at[batch].set(c_ctx)
    mod = _modulation(cvec, w_ada, b_ada)
    rope = _rope_table(geom)
    xs = _assemble(geom, ctx.reshape(geom.n_ctx, d), x.reshape(batch * seq, d))
    g_fin = g_final.reshape(1, d)
    w = _pack_weights(d, w_in, w_att_out, w_conv_pw, w_mlstm_out, w_out, w_ff_gate, w_ff_up, w_ff_down)

    for l in range(depth):
        last = l == depth - 1
        mod3 = mod[l].reshape(mod_rows, 1, 6 * d)
        conv = (jnp.repeat(w_conv_dw[l], SUBLANES, axis=0), b_conv_dw[l].reshape(1, -1),
                g_conv_ln[l].reshape(1, -1), b_conv_ln[l].reshape(1, -1))
        q, k, vt, cact, qm, km, vmt, om, gates_t, br = _projection(
            geom, l, xs, mod3, g_norm_mix[l].reshape(1, d), rope, b_mgate[l], w_mconv[l], conv, w)
        att = _attention(geom, att_sink[l], q, k, vt)
        hf, hb = _mlstm_scan(geom, qm, km, vmt, gates_t)
        xs = _mix_ffn(geom, l, xs, mod3, att, cact, hf, hb, om, br, g_mlstm_norm[l].reshape(1, -1),
                      g_norm_ffn[l].reshape(1, d), g_fin, w, last)

    return xs.reshape(batch, seq, d)
```

```python
import functools

import jax
import jax.numpy as jnp
from jax import lax
from jax.experimental import pallas as pl
from jax.experimental.pallas import tpu as pltpu

F32 = jnp.float32
BF16 = jnp.bfloat16

GRID_W = 64
N_Q_HEADS = 8
N_KV_HEADS = 2
HEAD_DIM = 64
WINDOW = 128
ATT_BLOCK = 128
ROPE_BASE = 10000.0
ATT_Q = N_Q_HEADS * HEAD_DIM
ATT_KV = N_KV_HEADS * HEAD_DIM
CONV_DIM = 512
CONV_WIDTH = 31
M_HEADS = 4
M_HEAD_DIM = 128
M_WIDTH = M_HEADS * M_HEAD_DIM
M_SHORT_CONV = 3
N_GATE_COLS = 4 * M_HEADS
EPS = 1e-6
NEG_INF = -1e30
LOG2E = 1.4426950408889634

LANES = 128
SUBLANES = 8
MXU_TILE = 256
M_CHUNK = 256
M_AUG = 16
ATT_STEP = 256
CONV_HALO = 16
KV_W = N_KV_HEADS * LANES
VMEM_LIMIT = 52 * 1024 * 1024


def _cparams(*sem):
    return pltpu.CompilerParams(dimension_semantics=sem, vmem_limit_bytes=VMEM_LIMIT)


def _resident(shape):
    nd = len(shape)
    return pl.BlockSpec(shape, lambda *_: (0,) * nd, pipeline_mode=pl.Buffered(1))


def _layer_resident(arr, l):
    nd = arr.ndim
    return pl.BlockSpec((None,) + arr.shape[1:], lambda *_: (l,) + (0,) * (nd - 1), pipeline_mode=pl.Buffered(1))


def _sigmoid(x):
    return 1.0 / (1.0 + jnp.exp(-x))


def _silu(x):
    return x * _sigmoid(x)


def _log_sigmoid(x):
    return jnp.minimum(x, 0.0) - jnp.log(1.0 + jnp.exp(-jnp.abs(x)))


def _rms(x):
    return x * lax.rsqrt(jnp.mean(x * x, axis=-1, keepdims=True) + EPS)


def _norm_mod(x, g, sc, sh):
    return (_rms(x) * g) * (1.0 + sc) + sh


def _dot(a, b):
    return jnp.dot(a, b, preferred_element_type=F32)


_NT = (((1,), (1,)), ((), ()))


def _mod_body(c_ref, w_ref, b_ref, o_ref):
    h = _silu(c_ref[...]).astype(BF16)
    o_ref[0] = _dot(h, w_ref[0].astype(BF16)) + b_ref[0]


def _modulation(cvec, w_ada, b_ada):
    depth, d, n6 = w_ada.shape
    rows = cvec.shape[0]
    tn = n6 // 4
    return pl.pallas_call(
        _mod_body,
        grid=(depth, n6 // tn),
        in_specs=[pl.BlockSpec((rows, d), lambda l, j: (0, 0)),
                  pl.BlockSpec((1, d, tn), lambda l, j: (l, 0, j)),
                  pl.BlockSpec((1, 1, tn), lambda l, j: (l, 0, j))],
        out_specs=pl.BlockSpec((1, rows, tn), lambda l, j: (l, 0, j)),
        out_shape=jax.ShapeDtypeStruct((depth, rows, n6), F32),
        compiler_params=_cparams("arbitrary", "arbitrary"),
        name="ada_mod",
    )(cvec, w_ada, b_ada.reshape(depth, 1, n6))


class _Geom:
    def __init__(self, batch, seq, ctx_len):
        self.B, self.S, self.Lc = batch, seq, ctx_len
        self.n_ctx = batch * ctx_len
        self.N = self.n_ctx + batch * seq

    def mod_row(self, tile, tm):
        n_ctx_tiles = self.n_ctx // tm
        return jnp.where(tile < n_ctx_tiles, self.B, (tile - n_ctx_tiles) // (self.S // tm))

    def seq_tile(self, tile, tm):
        n_ctx_tiles = self.n_ctx // tm
        tc, ts = self.Lc // tm, self.S // tm
        return jnp.where(tile < n_ctx_tiles, tile % tc, tc + (tile - n_ctx_tiles) % ts)

    def seq_edges(self, tile, tm):
        n_ctx_tiles = self.n_ctx // tm
        tc, ts = self.Lc // tm, self.S // tm
        pos = jnp.where(tile < n_ctx_tiles, tile % tc, (tile - n_ctx_tiles) % ts)
        last = jnp.where(tile < n_ctx_tiles, tc - 1, ts - 1)
        return pos == 0, pos == last


def _row_tile(geom, want):
    tm = want
    while geom.n_ctx % tm or geom.S % tm:
        tm //= 2
    return tm


def _halo_specs(geom, tm, halo, width):
    per = tm // halo
    last = geom.N // halo - 1
    return [pl.BlockSpec((halo, width), lambda i: (jnp.maximum(i * per - 1, 0), 0)),
            pl.BlockSpec((tm, width), lambda i: (i, 0)),
            pl.BlockSpec((halo, width), lambda i: (jnp.minimum((i + 1) * per, last), 0))]


def _assemble_body(n_ctx_tiles, ctx_ref, x_ref, o_ref):
    o_ref[...] = jnp.where(pl.program_id(0) < n_ctx_tiles, ctx_ref[...], x_ref[...])


def _assemble(geom, ctx2, x2):
    d = x2.shape[1]
    tm = _row_tile(geom, 1024)
    nct = geom.n_ctx // tm
    return pl.pallas_call(
        functools.partial(_assemble_body, nct),
        grid=(geom.N // tm,),
        in_specs=[pl.BlockSpec((tm, d), lambda i: (jnp.minimum(i, nct - 1), 0)),
                  pl.BlockSpec((tm, d), lambda i: (jnp.maximum(i - nct, 0), 0))],
        out_specs=pl.BlockSpec((tm, d), lambda i: (i, 0)),
        out_shape=jax.ShapeDtypeStruct((geom.N, d), F32),
        compiler_params=_cparams("arbitrary"),
        name="assemble_tokens",
    )(ctx2, x2)


def _rope(x, cos, sin_a, sin_b):
    return x * cos + pltpu.roll(x, LANES - 16, 1) * sin_a + pltpu.roll(x, 16, 1) * sin_b


def _proj_body(geom, tm, xp_ref, x_ref, xn_ref, mod_ref, g_ref, rope_ref, bgt_ref, wc_ref,
               wdw_ref, bdw_ref, gln_ref, bln_ref,
               wtok_ref, wbr_ref, wfeat_ref,
               q_ref, k_ref, vt_ref, cact_ref, qm_ref, km_ref, vmt_ref, om_ref, gatet_ref, br_ref,
               pad_ref, sh_ref):
    d = x_ref.shape[1]
    mod = mod_ref[0]
    g, sc, sh = g_ref[...], mod[:, d:2 * d], mod[:, 0:d]
    h = _norm_mod(x_ref[...], g, sc, sh).astype(BF16)
    cos, sa, sb = rope_ref[:, 0:LANES], rope_ref[:, LANES:2 * LANES], rope_ref[:, 2 * LANES:3 * LANES]

    halo = CONV_HALO
    first, last = geom.seq_edges(pl.program_id(0), tm)
    h_halo = _norm_mod(jnp.concatenate([xp_ref[...], xn_ref[...]], axis=0), g, sc, sh).astype(BF16)
    h_ext = jnp.concatenate([h, h_halo], axis=0)

    glu = _dot(h_ext, wtok_ref[:, _TOK["glu"]])
    u = glu[:, :CONV_DIM] * _sigmoid(glu[:, CONV_DIM:])
    sh_ref[0, 0:halo, :] = jnp.where(first, 0.0, u[tm:tm + halo])
    sh_ref[0, halo:halo + tm, :] = u[0:tm]
    sh_ref[0, halo + tm:2 * halo + tm, :] = jnp.where(last, 0.0, u[tm + halo:tm + 2 * halo])
    span = tm + 2 * halo - SUBLANES
    for s_ in range(1, SUBLANES):
        sh_ref[s_, 0:span, :] = sh_ref[0, s_:s_ + span, :]
    rows = 32

    def conv_rows(r0, zero):
        accs = [zero, None]
        for j in range(CONV_WIDTH):
            a_, s_ = divmod(halo + j - CONV_WIDTH // 2, SUBLANES)
            off = r0 + a_ * SUBLANES
            win = sh_ref[s_, off:off + rows, :].reshape(rows // SUBLANES, SUBLANES, -1)
            t = win * wdw_ref[j * SUBLANES:(j + 1) * SUBLANES, :][None]
            accs[j % 2] = t if accs[j % 2] is None else accs[j % 2] + t
        acc = accs[0] + accs[1]
        yv = acc.reshape(rows, -1) + bdw_ref[...]
        mu = jnp.mean(yv, axis=-1, keepdims=True)
        yc = yv - mu
        var = jnp.mean(yc * yc, axis=-1, keepdims=True)
        z = yc * lax.rsqrt(var + EPS) * gln_ref[...] + bln_ref[...]
        cact_ref[r0:r0 + rows, :] = _silu(z).astype(BF16)

    def late_zero(part):
        bits = pltpu.bitcast(part[0:rows, 0:CONV_DIM], jnp.uint32)
        return pltpu.bitcast((bits >> 16) >> 16, F32).reshape(rows // SUBLANES, SUBLANES, -1)

    n_groups = tm // rows
    br_cols = 2 * MXU_TILE
    n_br = wbr_ref.shape[1] // br_cols
    assert n_br + 2 == n_groups
    qkm = _dot(h_ext, wtok_ref[:, _TOK["qkm"]])
    zero = None
    qkv = om = None
    for gi in range(n_groups):
        conv_rows(gi * rows, zero)
        if gi < n_br:
            cs = slice(gi * br_cols, (gi + 1) * br_cols)
            part = _dot(h, wbr_ref[:, cs])
            br_ref[:, cs] = part.astype(BF16)
        elif gi == n_br:
            part = qkv = _dot(h, wtok_ref[:, _TOK["qk"]])
        else:
            part = om = _dot(h, wtok_ref[:, _TOK["om"]])
        zero = late_zero(part)
    om_ref[...] = om.astype(BF16)

    pad_ref[0:halo, :] = jnp.where(first, 0.0, qkm[tm:tm + halo])
    pad_ref[halo:halo + tm, :] = qkm[0:tm]
    pad_ref[halo + tm:2 * halo + tm, :] = jnp.where(last, 0.0, qkm[tm + halo:tm + 2 * halo])
    vgt = lax.dot_general(wfeat_ref[...], h, _NT, preferred_element_type=F32)
    acc = None
    for j in range(M_SHORT_CONV):
        off = halo + j - M_SHORT_CONV // 2
        t = pad_ref[off:off + tm, :] * wc_ref[j:j + 1, :]
        acc = t if acc is None else acc + t
    y = _silu(acc)
    qm_ref[...] = y[:, :M_WIDTH].astype(BF16)
    km_ref[...] = (y[:, M_WIDTH:] * M_HEAD_DIM ** -0.5).astype(BF16)

    scale = HEAD_DIM ** -0.5 * LOG2E
    for j in range(ATT_Q // LANES):
        sl = slice(j * LANES, (j + 1) * LANES)
        q_ref[:, sl] = _rope(qkv[:, sl] * scale, cos, sa, sb).astype(BF16)
    lo = lax.broadcasted_iota(jnp.int32, (tm, LANES), 1) < HEAD_DIM
    keys = _rope(qkv[:, ATT_Q:ATT_Q + ATT_KV], cos, sa, sb)
    swapped = pltpu.roll(keys, HEAD_DIM, 1)
    k_ref[:, 0:LANES] = jnp.where(lo, keys, swapped).astype(BF16)
    k_ref[:, LANES:2 * LANES] = jnp.where(lo, swapped, keys).astype(BF16)
    vmt_ref[...] = vgt[:M_WIDTH].astype(BF16)
    vt_ref[...] = vgt[M_WIDTH:M_WIDTH + ATT_KV].astype(BF16)
    gatet_ref[...] = vgt[M_WIDTH + ATT_KV:] + bgt_ref[...]


def _projection(geom, l, x, mod3, g, rope, b_gate, w_mconv, conv, w):
    n, d = x.shape
    tm = _row_tile(geom, 256)
    row = lambda i: (i, 0)
    col = lambda i: (0, i)
    mod_map = lambda i: (geom.mod_row(i, tm), 0, 0)
    rope_map = lambda i: (geom.seq_tile(i, tm), 0)
    outs = ((ATT_Q, BF16, False), (KV_W, BF16, False), (ATT_KV, BF16, True), (CONV_DIM, BF16, False),
            (M_WIDTH, BF16, False), (M_WIDTH, BF16, False), (M_WIDTH, BF16, True), (M_WIDTH, BF16, False),
            (N_GATE_COLS, F32, True), (3 * d, BF16, False))
    tok = w["tok"]
    tok_spec = pl.BlockSpec((None, d, _TOK_W), lambda i: (l, 0, 0), pipeline_mode=pl.Buffered(1))
    return pl.pallas_call(
        functools.partial(_proj_body, geom, tm),
        grid=(n // tm,),
        in_specs=_halo_specs(geom, tm, CONV_HALO, d)
                 + [pl.BlockSpec((1, 1, mod3.shape[2]), mod_map),
                    _resident((1, d)),
                    pl.BlockSpec((tm, rope.shape[1]), rope_map),
                    _resident((N_GATE_COLS, 1)), _resident(w_mconv.shape)]
                 + [_resident(a.shape) for a in conv]
                 + [tok_spec, _layer_resident(w["br"], l), _layer_resident(w["feat"], l)],
        out_specs=[pl.BlockSpec((c, tm), col) if fm else pl.BlockSpec((tm, c), row) for c, _, fm in outs],
        out_shape=[jax.ShapeDtypeStruct((c, n) if fm else (n, c), t) for c, t, fm in outs],
        scratch_shapes=[pltpu.VMEM((tm + 2 * CONV_HALO, 2 * M_WIDTH), F32),
                        pltpu.VMEM((SUBLANES, tm + 2 * CONV_HALO, CONV_DIM), F32)],
        compiler_params=_cparams("parallel"),
        name="in_proj",
    )(x, x, x, mod3, g, rope, b_gate.reshape(N_GATE_COLS, 1), w_mconv, *conv, tok, w["br"], w["feat"])


def _att_body(geom, sink_ref, q_ref, kp_ref, kc_ref, kn_ref, kx_ref, vp_ref, vc_ref, vn_ref, vx_ref, o_ref):
    blk = ATT_BLOCK
    nsub = ATT_STEP // blk
    step = pl.program_id(1) - geom.Lc // ATT_STEP
    group = N_Q_HEADS // N_KV_HEADS
    kk = lax.broadcasted_iota(jnp.int32, (3 * blk, blk), 0)
    qq = lax.broadcasted_iota(jnp.int32, (3 * blk, blk), 1)
    lo = lax.broadcasted_iota(jnp.int32, (blk, LANES), 1) < HEAD_DIM
    zero = jnp.zeros((blk, LANES), BF16)
    k_loc = jnp.concatenate([kp_ref[...], kc_ref[...], kn_ref[...]], axis=0)
    vt_loc = jnp.concatenate([vp_ref[...], vc_ref[...], vn_ref[...]], axis=1)
    k_ctx, vt_ctx = kx_ref[...], vx_ref[...]
    chains = []
    for sub in range(nsub):
        n = step * nsub + sub
        key_pos = (n - 1) * blk + kk
        q_pos = n * blk + qq
        valid = (jnp.abs(key_pos - q_pos) <= WINDOW) & (key_pos >= 0) & (key_pos < geom.S) & (n >= 0)
        bias = jnp.where(valid, 0.0, NEG_INF)
        bias = jnp.concatenate([bias] * group, axis=1)
        rows = slice(sub * blk, (sub + 1) * blk)
        win = slice(sub * blk, (sub + 3) * blk)
        for hk in range(N_KV_HEADS):
            ks = slice(hk * LANES, (hk + 1) * LANES)
            tiles = [q_ref[rows, (hk * group + 2 * p) * HEAD_DIM:(hk * group + 2 * p + 2) * HEAD_DIM]
                     for p in range(group // 2)]
            qs = jnp.concatenate([jnp.where(lo if half == 0 else ~lo, t, zero)
                                  for t in tiles for half in (0, 1)], axis=0)
            s = jnp.concatenate(
                [lax.dot_general(k_loc[win, ks], qs, _NT, preferred_element_type=F32) + bias,
                 lax.dot_general(k_ctx[:, ks], qs, _NT, preferred_element_type=F32)], axis=0)
            chains.append((rows, win, hk, ks, s))

    stage2 = []
    for rows, win, hk, ks, s in chains:
        sk = jnp.concatenate(
            [jnp.full((1, blk), sink_ref[hk * group + j] * LOG2E, F32) for j in range(group)], axis=1)
        m = jnp.maximum(jnp.max(s, axis=0, keepdims=True), sk)
        e = jnp.exp2(s - m)
        den = jnp.sum(e, axis=0, keepdims=True) + jnp.exp2(sk - m)
        stage2.append((rows, win, hk, ks, e.astype(BF16), den))

    for rows, win, hk, ks, e, den in stage2:
        vs = slice(hk * HEAD_DIM, (hk + 1) * HEAD_DIM)
        vt = jnp.concatenate([vt_loc[vs, win], vt_ctx[vs, :]], axis=1)
        vt = jnp.concatenate([vt, vt], axis=0)
        o = (_dot(vt, e) * (1.0 / den)).T
        for p in range(group // 2):
            pair = jnp.where(lo, o[2 * p * blk:(2 * p + 1) * blk], o[(2 * p + 1) * blk:(2 * p + 2) * blk])
            c0 = (hk * group + 2 * p) * HEAD_DIM
            o_ref[rows, c0:c0 + LANES] = pair.astype(BF16)


def _attention(geom, sink, q, k, vt):
    blk, stp = ATT_BLOCK, ATT_STEP
    nsub = stp // blk
    ncs, nss = geom.Lc // stp, geom.S // stp
    nsb = geom.S // blk
    base_s, base_b = geom.n_ctx // stp, geom.n_ctx // blk

    def q_blk(b, i):
        return jnp.where(i < ncs, b * ncs + i, base_s + b * nss + i - ncs)

    def edge_blk(off):
        return lambda b, i: base_b + b * nsb + jnp.clip((i - ncs) * nsub + off, 0, nsb - 1)

    cur_blk = lambda b, i: base_s + b * nss + jnp.clip(i - ncs, 0, nss - 1)
    ctx_blk = lambda b, i: b
    blocks = ((blk, edge_blk(-1)), (stp, cur_blk), (blk, edge_blk(nsub)), (geom.Lc, ctx_blk))
    rows = lambda f: (lambda b, i: (f(b, i), 0))
    cols = lambda f: (lambda b, i: (0, f(b, i)))
    return pl.pallas_call(
        functools.partial(_att_body, geom),
        grid=(geom.B, ncs + nss),
        in_specs=[pl.BlockSpec(memory_space=pltpu.SMEM),
                  pl.BlockSpec((stp, ATT_Q), rows(q_blk))]
                 + [pl.BlockSpec((size, KV_W), rows(f)) for size, f in blocks]
                 + [pl.BlockSpec((ATT_KV, size), cols(f)) for size, f in blocks],
        out_specs=pl.BlockSpec((stp, ATT_Q), rows(q_blk)),
        out_shape=jax.ShapeDtypeStruct(q.shape, BF16),
        compiler_params=_cparams("parallel", "parallel"),
        name="window_attention",
    )(sink, q, k, k, k, k, vt, vt, vt, vt)


def _split3(x):
    hi = x.astype(BF16)
    r1 = x - hi.astype(F32)
    mid = r1.astype(BF16)
    lo = (r1 - mid.astype(F32)).astype(BF16)
    return hi, mid, lo


def _mlstm_body(qf, kf, vf, gtf, qb, kb, vb, gtb, of, ob, st_ref, m_ref):
    @pl.when(pl.program_id(1) == 0)
    def _():
        st_ref[...] = jnp.zeros_like(st_ref)
        m_ref[...] = jnp.full_like(m_ref, NEG_INF)

    L = M_CHUNK
    r = lax.broadcasted_iota(jnp.int32, (L, L), 0)
    c = lax.broadcasted_iota(jnp.int32, (L, L), 1)
    ones = jnp.ones((M_AUG, L), BF16)
    refs = ((qf, kf, vf, gtf, of), (qb, kb, vb, gtb, ob))
    prep = []
    for direction in range(2):
        keep = (r <= c) if direction == 0 else (r >= c)
        tri = jnp.where(keep, 1.0, 0.0).astype(BF16)
        gates_t = refs[direction][3][...]
        hi, mid, lo = _split3(_log_sigmoid(gates_t))
        brow_all = _dot(hi, tri) + _dot(mid, tri) + _dot(lo, tri)
        u0 = direction * M_HEADS
        c_rows = gates_t[u0:u0 + M_HEADS] - brow_all[2 * M_HEADS + u0:2 * M_HEADS + u0 + M_HEADS]
        c_cols = jnp.concatenate([c_rows, jnp.zeros((LANES - M_HEADS, L), F32)], axis=0).T
        prep.append((keep, gates_t, brow_all, c_cols))

    units = []
    for direction in range(2):
        keep, gates_t, brow_all, c_cols = prep[direction]
        q_ref, k_ref = refs[direction][0], refs[direction][1]
        for h in range(M_HEADS):
            u = direction * M_HEADS + h
            hs = slice(h * M_HEAD_DIM, (h + 1) * M_HEAD_DIM)
            q, k = q_ref[:, hs], k_ref[:, hs]
            b_row = brow_all[2 * M_HEADS + u:2 * M_HEADS + u + 1, :]
            i_row = gates_t[u:u + 1, :]
            m_prev = m_ref[u:u + 1, 0:1]
            state = st_ref[u]
            dmat = jnp.where(keep, b_row + c_cols[:, h:h + 1], NEG_INF)
            a_row = b_row + m_prev
            mt = jnp.maximum(a_row, jnp.max(dmat, axis=0, keepdims=True))
            big = lax.dot_general(jnp.concatenate([k, state.astype(BF16)], axis=0), q, _NT,
                                  preferred_element_type=F32)
            units.append((direction, u, hs, k, b_row, i_row, m_prev, state, dmat, a_row, mt, big))

    units2 = []
    for (direction, u, hs, k, b_row, i_row, m_prev, state, dmat, a_row, mt, big) in units:
        vt_aug = jnp.concatenate([refs[direction][2][hs, :], ones], axis=0)
        w_row = jnp.exp(a_row - mt)
        smat = (big[:L] * jnp.exp(dmat - mt)).astype(BF16)
        tot = w_row * big[L:] + _dot(vt_aug, smat)
        den = tot[M_HEAD_DIM:M_HEAD_DIM + 1, :]
        scale = 1.0 / jnp.maximum(jnp.abs(den), jnp.exp(-mt))
        refs[direction][4][:, hs] = (tot[:M_HEAD_DIM] * scale).T
        units2.append((direction, u, k, b_row, i_row, m_prev, state, vt_aug))

    for (direction, u, k, b_row, i_row, m_prev, state, vt_aug) in units2:
        end = L - 1 if direction == 0 else 0
        total = b_row[:, end:end + 1]
        g_row = total - b_row + i_row
        m_new = jnp.maximum(total + m_prev, jnp.max(g_row, axis=1, keepdims=True))
        decay = jnp.exp(total + m_prev - m_new)
        vw = (vt_aug.astype(F32) * jnp.exp(g_row - m_new)).astype(BF16)
        st_ref[u] = decay * state + _dot(vw, k)
        m_ref[u:u + 1, :] = jnp.broadcast_to(m_new, (1, LANES))


def _mlstm_scan(geom, q, k, vt, gates_t):
    L = M_CHUNK
    ncc, nsc = geom.Lc // L, geom.S // L
    base = geom.n_ctx // L

    def fwd(b, j):
        return jnp.where(j < ncc, b * ncc + j, base + b * nsc + j - ncc)

    def bwd(b, j):
        return jnp.where(j < ncc, b * ncc + ncc - 1 - j, base + b * nsc + nsc - 1 - (j - ncc))

    def specs(m):
        rows = lambda b, j: (m(b, j), 0)
        cols = lambda b, j: (0, m(b, j))
        return [pl.BlockSpec((L, M_WIDTH), rows), pl.BlockSpec((L, M_WIDTH), rows),
                pl.BlockSpec((M_WIDTH, L), cols), pl.BlockSpec((N_GATE_COLS, L), cols)]

    n = q.shape[0]
    return pl.pallas_call(
        _mlstm_body,
        grid=(geom.B, ncc + nsc),
        in_specs=specs(fwd) + specs(bwd),
        out_specs=[pl.BlockSpec((L, M_WIDTH), lambda b, j: (fwd(b, j), 0)),
                   pl.BlockSpec((L, M_WIDTH), lambda b, j: (bwd(b, j), 0))],
        out_shape=[jax.ShapeDtypeStruct((n, M_WIDTH), F32)] * 2,
        scratch_shapes=[pltpu.VMEM((2 * M_HEADS, M_HEAD_DIM + M_AUG, M_HEAD_DIM), F32),
                        pltpu.VMEM((2 * M_HEADS, LANES), F32)],
        compiler_params=_cparams("parallel", "arbitrary"),
        name="mlstm_scan",
    )(q, k, vt, gates_t, q, k, vt, gates_t)


def _mix_ffn_body(final, x_ref, mod_ref, att_ref, cact_ref, hf_ref, hb_ref, om_ref, br_ref, gm_ref,
                  g_ref, gf_ref, wa_ref, wp_ref, wm_ref, wo_ref, wg_ref, wu_ref, wd_ref, o_ref):
    d = x_ref.shape[1]
    mod = mod_ref[0]
    hm = _sigmoid(om_ref[...].astype(F32)) * (hf_ref[...] + hb_ref[...])
    parts = []
    for h in range(M_HEADS):
        t = hm[:, h * M_HEAD_DIM:(h + 1) * M_HEAD_DIM]
        mu = jnp.mean(t, axis=-1, keepdims=True)
        tc = t - mu
        var = jnp.mean(tc * tc, axis=-1, keepdims=True)
        parts.append(tc * lax.rsqrt(var + EPS))
    hn = (jnp.concatenate(parts, axis=-1) * gm_ref[...]).astype(BF16)
    ya = _dot(att_ref[...], wa_ref[...])
    yb = _dot(cact_ref[...], wp_ref[...])
    yc = _dot(hn, wm_ref[...])
    merged = (_sigmoid(br_ref[:, 0:d].astype(F32)) * ya + _sigmoid(br_ref[:, d:2 * d].astype(F32)) * yb
              + _sigmoid(br_ref[:, 2 * d:3 * d].astype(F32)) * yc)
    x = x_ref[...] + mod[:, 2 * d:3 * d] * _dot(merged.astype(BF16), wo_ref[...])
    h = _norm_mod(x, g_ref[...], mod[:, 4 * d:5 * d], mod[:, 3 * d:4 * d]).astype(BF16)
    ff = wg_ref.shape[1]
    half = -(-ff // (2 * MXU_TILE)) * MXU_TILE
    acc = None
    for sl in (slice(0, half), slice(half, ff)):
        act = (_silu(_dot(h, wg_ref[:, sl])) * _dot(h, wu_ref[:, sl])).astype(BF16)
        t = _dot(act, wd_ref[sl, :])
        acc = t if acc is None else acc + t
    y = x + mod[:, 5 * d:6 * d] * acc
    o_ref[...] = _rms(y) * gf_ref[...] if final else y


def _mix_ffn(geom, l, x, mod3, att, cact, hf, hb, om, br, g_mnorm, g_ffn, g_final, w, latent_only):
    n, d = x.shape
    tm = _row_tile(geom, 256)
    skip = geom.n_ctx // tm if latent_only else 0
    row = lambda i: (i + skip, 0)
    mod_map = lambda i: (geom.mod_row(i + skip, tm), 0, 0)
    weights = (w["att_out"], w["pw"], w["mout"], w["out"], w["ff_gate"], w["ff_up"], w["ff_down"])
    return pl.pallas_call(
        functools.partial(_mix_ffn_body, latent_only),
        grid=(n // tm - skip,),
        in_specs=[pl.BlockSpec((tm, d), row),
                  pl.BlockSpec((1, 1, mod3.shape[2]), mod_map),
                  pl.BlockSpec((tm, ATT_Q), row), pl.BlockSpec((tm, CONV_DIM), row),
                  pl.BlockSpec((tm, M_WIDTH), row), pl.BlockSpec((tm, M_WIDTH), row),
                  pl.BlockSpec((tm, M_WIDTH), row), pl.BlockSpec((tm, 3 * d), row),
                  _resident((1, M_WIDTH)), _resident((1, d)), _resident((1, d))]
                 + [_layer_resident(a, l) for a in weights],
        out_specs=pl.BlockSpec((tm, d), lambda i: (i, 0)),
        out_shape=jax.ShapeDtypeStruct((n - skip * tm, d), F32),
        compiler_params=_cparams("parallel"),
        name="mix_ffn",
    )(x, mod3, att, cact, hf, hb, om, br, g_mnorm, g_ffn, g_final, *weights)


def _rope_table(geom):
    p = jnp.arange(geom.S)
    n_freq = HEAD_DIM // 4
    inv_freq = ROPE_BASE ** (-jnp.arange(n_freq, dtype=F32) / n_freq)
    ang_r = (p // GRID_W).astype(F32)[:, None] * inv_freq
    ang_c = (p % GRID_W).astype(F32)[:, None] * inv_freq
    ang = jnp.concatenate([ang_r, ang_r, ang_c, ang_c], axis=-1)
    reps = LANES // HEAD_DIM
    cos = jnp.tile(jnp.cos(ang), (1, reps))
    sin = jnp.tile(jnp.sin(ang), (1, reps))
    first_half = (jnp.arange(LANES) % (HEAD_DIM // 2)) < HEAD_DIM // 4
    lat = jnp.concatenate([cos, jnp.where(first_half, -sin, 0.0), jnp.where(first_half, 0.0, sin)], axis=1)
    ident = jnp.concatenate([jnp.ones((geom.Lc, LANES), F32), jnp.zeros((geom.Lc, 2 * LANES), F32)], axis=1)
    return jnp.concatenate([ident, lat], axis=0)


_TOK = {"qk": slice(0, ATT_Q + ATT_KV)}
_off = ATT_Q + 2 * ATT_KV
for _name, _size in (("glu", 2 * CONV_DIM), ("qkm", 2 * M_WIDTH), ("vm", M_WIDTH), ("om", M_WIDTH)):
    _TOK[_name] = slice(_off, _off + _size)
    _off += _size
_TOK_W = _off
assert all(sl.start % LANES == 0 for sl in _TOK.values()) and _TOK_W % LANES == 0


def _pack_weights(d, w_in, w_att_out, w_conv_pw, w_mlstm_out, w_out, w_ff_gate, w_ff_up, w_ff_down):
    wb = w_in.astype(BF16)
    v0, g0 = ATT_Q + ATT_KV, _TOK_W
    feat = jnp.concatenate([wb[:, :, _TOK["vm"]], wb[:, :, v0:v0 + ATT_KV], wb[:, :, g0:g0 + N_GATE_COLS]], axis=2)
    return {
        "tok": wb, "br": wb[:, :, g0 + N_GATE_COLS:], "feat": feat.transpose(0, 2, 1),
        "att_out": w_att_out.astype(BF16), "pw": w_conv_pw.astype(BF16), "mout": w_mlstm_out.astype(BF16),
        "out": w_out.astype(BF16), "ff_gate": w_ff_gate.astype(BF16), "ff_up": w_ff_up.astype(BF16),
        "ff_down": w_ff_down.astype(BF16),
    }


def kernel(x, c, ctx, c_ctx, w_ada, b_ada, g_norm_mix, g_norm_ffn, w_in, b_mgate, att_sink, w_att_out, w_conv_dw, b_conv_dw, g_conv_ln, b_conv_ln, w_conv_pw, w_mconv, g_mlstm_norm, w_mlstm_out, w_out, w_ff_gate, w_ff_up, w_ff_down, g_final):
    batch, seq, d = x.shape
    ctx_len = ctx.shape[1]
    depth = w_ada.shape[0]
    geom = _Geom(batch, seq, ctx_len)
    assert seq % ATT_STEP == 0 and ctx_len % ATT_STEP == 0 and seq % GRID_W == 0
    assert seq % M_CHUNK == 0 and ctx_len % M_CHUNK == 0 and d % LANES == 0
    assert LANES == 2 * HEAD_DIM and ATT_KV == LANES

    mod_rows = -(-(batch + 1) // SUBLANES) * SUBLANES
    cvec = jnp.zeros((mod_rows, d), F32).at[:batch].set(c).at[batch].set(c_ctx)
    mod = _modulation(cvec, w_ada, b_ada)
    rope = _rope_table(geom)
    xs = _assemble(geom, ctx.reshape(geom.n_ctx, d), x.reshape(batch * seq, d))
    g_fin = g_final.reshape(1, d)
    w = _pack_weights(d, w_in, w_att_out, w_conv_pw, w_mlstm_out, w_out, w_ff_gate, w_ff_up, w_ff_down)

    for l in range(depth):
        last = l == depth - 1
        mod3 = mod[l].reshape(mod_rows, 1, 6 * d)
        conv = (jnp.repeat(w_conv_dw[l], SUBLANES, axis=0), b_conv_dw[l].reshape(1, -1),
                g_conv_ln[l].reshape(1, -1), b_conv_ln[l].reshape(1, -1))
        q, k, vt, cact, qm, km, vmt, om, gates_t, br = _projection(
            geom, l, xs, mod3, g_norm_mix[l].reshape(1, d), rope, b_mgate[l], w_mconv[l], conv, w)
        att = _attention(geom, att_sink[l], q, k, vt)
        hf, hb = _mlstm_scan(geom, qm, km, vmt, gates_t)
        xs = _mix_ffn(geom, l, xs, mod3, att, cact, hf, hb, om, br, g_mlstm_norm[l].reshape(1, -1),
                      g_norm_ffn[l].reshape(1, d), g_fin, w, last)

    return xs.reshape(batch, seq, d)
```

```python
import functools

import jax
import jax.numpy as jnp
from jax import lax
from jax.experimental import pallas as pl
from jax.experimental.pallas import tpu as pltpu

F32 = jnp.float32
BF16 = jnp.bfloat16

GRID_W = 64
N_Q_HEADS = 8
N_KV_HEADS = 2
HEAD_DIM = 64
WINDOW = 128
ATT_BLOCK = 128
ROPE_BASE = 10000.0
ATT_Q = N_Q_HEADS * HEAD_DIM
ATT_KV = N_KV_HEADS * HEAD_DIM
CONV_DIM = 512
CONV_WIDTH = 31
M_HEADS = 4
M_HEAD_DIM = 128
M_WIDTH = M_HEADS * M_HEAD_DIM
M_SHORT_CONV = 3
N_GATE_COLS = 4 * M_HEADS
EPS = 1e-6
NEG_INF = -1e30
LOG2E = 1.4426950408889634

LANES = 128
SUBLANES = 8
MXU_TILE = 256
M_CHUNK = 256
M_AUG = 16
ATT_STEP = 256
CONV_HALO = 16
KV_W = N_KV_HEADS * LANES
VMEM_LIMIT = 52 * 1024 * 1024


def _cparams(*sem):
    return pltpu.CompilerParams(dimension_semantics=sem, vmem_limit_bytes=VMEM_LIMIT)


def _resident(shape):
    nd = len(shape)
    return pl.BlockSpec(shape, lambda *_: (0,) * nd, pipeline_mode=pl.Buffered(1))


def _layer_resident(arr, l):
    nd = arr.ndim
    return pl.BlockSpec((None,) + arr.shape[1:], lambda *_: (l,) + (0,) * (nd - 1), pipeline_mode=pl.Buffered(1))


def _sigmoid(x):
    return 1.0 / (1.0 + jnp.exp(-x))


def _silu(x):
    return x * _sigmoid(x)


def _log_sigmoid(x):
    return jnp.minimum(x, 0.0) - jnp.log(1.0 + jnp.exp(-jnp.abs(x)))


def _rms(x):
    return x * lax.rsqrt(jnp.mean(x * x, axis=-1, keepdims=True) + EPS)


def _norm_mod(x, g, sc, sh):
    return (_rms(x) * g) * (1.0 + sc) + sh


_NT = (((1,), (1,)), ((), ()))


def _ntdot(a, b):
    return lax.dot_general(a, b, _NT, preferred_element_type=F32)


def _dot(a, b):
    return jnp.dot(a, b, preferred_element_type=F32)


def _mod_body(c_ref, w_ref, b_ref, o_ref):
    h = _silu(c_ref[...]).astype(BF16)
    o_ref[0] = _dot(h, w_ref[0].astype(BF16)) + b_ref[0]


def _modulation(cvec, w_ada, b_ada):
    depth, d, n6 = w_ada.shape
    rows = cvec.shape[0]
    tn = n6 // 4
    return pl.pallas_call(
        _mod_body,
        grid=(depth, n6 // tn),
        in_specs=[pl.BlockSpec((rows, d), lambda l, j: (0, 0)),
                  pl.BlockSpec((1, d, tn), lambda l, j: (l, 0, j)),
                  pl.BlockSpec((1, 1, tn), lambda l, j: (l, 0, j))],
        out_specs=pl.BlockSpec((1, rows, tn), lambda l, j: (l, 0, j)),
        out_shape=jax.ShapeDtypeStruct((depth, rows, n6), F32),
        compiler_params=_cparams("arbitrary", "arbitrary"),
        name="ada_mod",
    )(cvec, w_ada, b_ada.reshape(depth, 1, n6))


class _Geom:
    def __init__(self, batch, seq, ctx_len):
        self.B, self.S, self.Lc = batch, seq, ctx_len
        self.n_ctx = batch * ctx_len
        self.N = self.n_ctx + batch * seq

    def mod_row(self, tile, tm):
        n_ctx_tiles = self.n_ctx // tm
        return jnp.where(tile < n_ctx_tiles, self.B, (tile - n_ctx_tiles) // (self.S // tm))

    def seq_tile(self, tile, tm):
        n_ctx_tiles = self.n_ctx // tm
        tc, ts = self.Lc // tm, self.S // tm
        return jnp.where(tile < n_ctx_tiles, tile % tc, tc + (tile - n_ctx_tiles) % ts)

    def seq_edges(self, tile, tm):
        n_ctx_tiles = self.n_ctx // tm
        tc, ts = self.Lc // tm, self.S // tm
        pos = jnp.where(tile < n_ctx_tiles, tile % tc, (tile - n_ctx_tiles) % ts)
        last = jnp.where(tile < n_ctx_tiles, tc - 1, ts - 1)
        return pos == 0, pos == last


def _row_tile(geom, want):
    tm = want
    while geom.n_ctx % tm or geom.S % tm:
        tm //= 2
    return tm


def _halo_specs(geom, tm, halo, width):
    per = tm // halo
    last = geom.N // halo - 1
    return [pl.BlockSpec((halo, width), lambda i: (jnp.maximum(i * per - 1, 0), 0)),
            pl.BlockSpec((tm, width), lambda i: (i, 0)),
            pl.BlockSpec((halo, width), lambda i: (jnp.minimum((i + 1) * per, last), 0))]


def _assemble_body(n_ctx_tiles, ctx_ref, x_ref, o_ref):
    o_ref[...] = jnp.where(pl.program_id(0) < n_ctx_tiles, ctx_ref[...], x_ref[...])


def _assemble(geom, ctx2, x2):
    d = x2.shape[1]
    tm = _row_tile(geom, 1024)
    nct = geom.n_ctx // tm
    return pl.pallas_call(
        functools.partial(_assemble_body, nct),
        grid=(geom.N // tm,),
        in_specs=[pl.BlockSpec((tm, d), lambda i: (jnp.minimum(i, nct - 1), 0)),
                  pl.BlockSpec((tm, d), lambda i: (jnp.maximum(i - nct, 0), 0))],
        out_specs=pl.BlockSpec((tm, d), lambda i: (i, 0)),
        out_shape=jax.ShapeDtypeStruct((geom.N, d), F32),
        compiler_params=_cparams("arbitrary"),
        name="assemble_tokens",
    )(ctx2, x2)


def _rope(x, cos, sin_a, sin_b):
    return x * cos + pltpu.roll(x, LANES - 16, 1) * sin_a + pltpu.roll(x, 16, 1) * sin_b


def _proj_body(geom, tm, xp_ref, x_ref, xn_ref, mod_ref, g_ref, rope_ref, bgt_ref, wc_ref,
               wdw_ref, bdw_ref, gln_ref, bln_ref,
               wt_ref,
               q_ref, k_ref, vt_ref, cact_ref, qm_ref, km_ref, vmt_ref, om_ref, gatet_ref, br_ref,
               pad_ref, sh_ref):
    d = x_ref.shape[1]
    mod = mod_ref[0]
    g, sc, sh = g_ref[...], mod[:, d:2 * d], mod[:, 0:d]
    h = _norm_mod(x_ref[...], g, sc, sh).astype(BF16)
    cos, sa, sb = rope_ref[:, 0:LANES], rope_ref[:, LANES:2 * LANES], rope_ref[:, 2 * LANES:3 * LANES]

    halo = CONV_HALO
    first, last = geom.seq_edges(pl.program_id(0), tm)
    h_halo = _norm_mod(jnp.concatenate([xp_ref[...], xn_ref[...]], axis=0), g, sc, sh).astype(BF16)
    h_ext = jnp.concatenate([h, h_halo], axis=0)

    glu = _ntdot(h_ext, wt_ref[_TOK["glu"], :])
    u = glu[:, :CONV_DIM] * _sigmoid(glu[:, CONV_DIM:])
    sh_ref[0, 0:halo, :] = jnp.where(first, 0.0, u[tm:tm + halo])
    sh_ref[0, halo:halo + tm, :] = u[0:tm]
    sh_ref[0, halo + tm:2 * halo + tm, :] = jnp.where(last, 0.0, u[tm + halo:tm + 2 * halo])
    span = tm + 2 * halo - SUBLANES
    for s_ in range(1, SUBLANES):
        sh_ref[s_, 0:span, :] = sh_ref[0, s_:s_ + span, :]
    rows = 32

    def conv_rows(r0, zero):
        accs = [zero, None]
        for j in range(CONV_WIDTH):
            a_, s_ = divmod(halo + j - CONV_WIDTH // 2, SUBLANES)
            off = r0 + a_ * SUBLANES
            win = sh_ref[s_, off:off + rows, :].reshape(rows // SUBLANES, SUBLANES, -1)
            t = win * wdw_ref[j * SUBLANES:(j + 1) * SUBLANES, :][None]
            accs[j % 2] = t if accs[j % 2] is None else accs[j % 2] + t
        acc = accs[0] + accs[1]
        yv = acc.reshape(rows, -1) + bdw_ref[...]
        mu = jnp.mean(yv, axis=-1, keepdims=True)
        yc = yv - mu
        var = jnp.mean(yc * yc, axis=-1, keepdims=True)
        z = yc * lax.rsqrt(var + EPS) * gln_ref[...] + bln_ref[...]
        cact_ref[r0:r0 + rows, :] = _silu(z).astype(BF16)

    def late_zero(part):
        bits = pltpu.bitcast(part[0:rows, 0:CONV_DIM], jnp.uint32)
        return pltpu.bitcast((bits >> 16) >> 16, F32).reshape(rows // SUBLANES, SUBLANES, -1)

    n_groups = tm // rows
    br_cols = 2 * MXU_TILE
    br0 = _TOK_W + N_GATE_COLS
    n_br = (wt_ref.shape[0] - br0) // br_cols
    assert n_br + 2 == n_groups
    qkm = _ntdot(h_ext, wt_ref[_TOK["qkm"], :])
    zero = None
    qkv = om = None
    for gi in range(n_groups):
        conv_rows(gi * rows, zero)
        if gi < n_br:
            cs = slice(gi * br_cols, (gi + 1) * br_cols)
            part = _ntdot(h, wt_ref[br0 + gi * br_cols:br0 + (gi + 1) * br_cols, :])
            br_ref[:, cs] = part.astype(BF16)
        elif gi == n_br:
            part = qkv = _ntdot(h, wt_ref[_TOK["qk"], :])
        else:
            part = om = _ntdot(h, wt_ref[_TOK["om"], :])
        zero = late_zero(part)
    om_ref[...] = om.astype(BF16)

    pad_ref[0:halo, :] = jnp.where(first, 0.0, qkm[tm:tm + halo])
    pad_ref[halo:halo + tm, :] = qkm[0:tm]
    pad_ref[halo + tm:2 * halo + tm, :] = jnp.where(last, 0.0, qkm[tm + halo:tm + 2 * halo])
    v0 = ATT_Q + ATT_KV
    vmt = _ntdot(wt_ref[_TOK["vm"], :], h)
    vtt = _ntdot(wt_ref[v0:v0 + ATT_KV, :], h)
    gtt = _ntdot(wt_ref[_TOK_W:_TOK_W + N_GATE_COLS, :], h)
    acc = None
    for j in range(M_SHORT_CONV):
        off = halo + j - M_SHORT_CONV // 2
        t = pad_ref[off:off + tm, :] * wc_ref[j:j + 1, :]
        acc = t if acc is None else acc + t
    y = _silu(acc)
    qm_ref[...] = y[:, :M_WIDTH].astype(BF16)
    km_ref[...] = (y[:, M_WIDTH:] * M_HEAD_DIM ** -0.5).astype(BF16)

    scale = HEAD_DIM ** -0.5 * LOG2E
    for j in range(ATT_Q // LANES):
        sl = slice(j * LANES, (j + 1) * LANES)
        q_ref[:, sl] = _rope(qkv[:, sl] * scale, cos, sa, sb).astype(BF16)
    lo = lax.broadcasted_iota(jnp.int32, (tm, LANES), 1) < HEAD_DIM
    keys = _rope(qkv[:, ATT_Q:ATT_Q + ATT_KV], cos, sa, sb)
    swapped = pltpu.roll(keys, HEAD_DIM, 1)
    k_ref[:, 0:LANES] = jnp.where(lo, keys, swapped).astype(BF16)
    k_ref[:, LANES:2 * LANES] = jnp.where(lo, swapped, keys).astype(BF16)
    vmt_ref[...] = vmt.astype(BF16)
    vt_ref[...] = vtt.astype(BF16)
    gatet_ref[...] = gtt + bgt_ref[...]


def _projection(geom, l, x, mod3, g, rope, b_gate, w_mconv, conv, w):
    n, d = x.shape
    tm = _row_tile(geom, 256)
    row = lambda i: (i, 0)
    col = lambda i: (0, i)
    mod_map = lambda i: (geom.mod_row(i, tm), 0, 0)
    rope_map = lambda i: (geom.seq_tile(i, tm), 0)
    outs = ((ATT_Q, BF16, False), (KV_W, BF16, False), (ATT_KV, BF16, True), (CONV_DIM, BF16, False),
            (M_WIDTH, BF16, False), (M_WIDTH, BF16, False), (M_WIDTH, BF16, True), (M_WIDTH, BF16, False),
            (N_GATE_COLS, F32, True), (3 * d, BF16, False))
    return pl.pallas_call(
        functools.partial(_proj_body, geom, tm),
        grid=(n // tm,),
        in_specs=_halo_specs(geom, tm, CONV_HALO, d)
                 + [pl.BlockSpec((1, 1, mod3.shape[2]), mod_map),
                    _resident((1, d)),
                    pl.BlockSpec((tm, rope.shape[1]), rope_map),
                    _resident((N_GATE_COLS, 1)), _resident(w_mconv.shape)]
                 + [_resident(a.shape) for a in conv]
                 + [_layer_resident(w["in_t"], l)],
        out_specs=[pl.BlockSpec((c, tm), col) if fm else pl.BlockSpec((tm, c), row) for c, _, fm in outs],
        out_shape=[jax.ShapeDtypeStruct((c, n) if fm else (n, c), t) for c, t, fm in outs],
        scratch_shapes=[pltpu.VMEM((tm + 2 * CONV_HALO, 2 * M_WIDTH), F32),
                        pltpu.VMEM((SUBLANES, tm + 2 * CONV_HALO, CONV_DIM), F32)],
        compiler_params=_cparams("parallel"),
        name="in_proj",
    )(x, x, x, mod3, g, rope, b_gate.reshape(N_GATE_COLS, 1), w_mconv, *conv, w["in_t"])


def _att_body(geom, sink_ref, q_ref, kp_ref, kc_ref, kn_ref, kx_ref, vp_ref, vc_ref, vn_ref, vx_ref, o_ref):
    blk = ATT_BLOCK
    nsub = ATT_STEP // blk
    step = pl.program_id(1) - geom.Lc // ATT_STEP
    group = N_Q_HEADS // N_KV_HEADS
    kk = lax.broadcasted_iota(jnp.int32, (3 * blk, blk), 0)
    qq = lax.broadcasted_iota(jnp.int32, (3 * blk, blk), 1)
    lo = lax.broadcasted_iota(jnp.int32, (blk, LANES), 1) < HEAD_DIM
    zero = jnp.zeros((blk, LANES), BF16)
    k_loc = jnp.concatenate([kp_ref[...], kc_ref[...], kn_ref[...]], axis=0)
    vt_loc = jnp.concatenate([vp_ref[...], vc_ref[...], vn_ref[...]], axis=1)
    k_ctx, vt_ctx = kx_ref[...], vx_ref[...]
    chains = []
    for sub in range(nsub):
        n = step * nsub + sub
        key_pos = (n - 1) * blk + kk
        q_pos = n * blk + qq
        valid = (jnp.abs(key_pos - q_pos) <= WINDOW) & (key_pos >= 0) & (key_pos < geom.S) & (n >= 0)
        bias = jnp.where(valid, 0.0, NEG_INF)
        bias = jnp.concatenate([bias] * group, axis=1)
        rows = slice(sub * blk, (sub + 1) * blk)
        win = slice(sub * blk, (sub + 3) * blk)
        for hk in range(N_KV_HEADS):
            ks = slice(hk * LANES, (hk + 1) * LANES)
            tiles = [q_ref[rows, (hk * group + 2 * p) * HEAD_DIM:(hk * group + 2 * p + 2) * HEAD_DIM]
                     for p in range(group // 2)]
            qs = jnp.concatenate([jnp.where(lo if half == 0 else ~lo, t, zero)
                                  for t in tiles for half in (0, 1)], axis=0)
            s = jnp.concatenate(
                [lax.dot_general(k_loc[win, ks], qs, _NT, preferred_element_type=F32) + bias,
                 lax.dot_general(k_ctx[:, ks], qs, _NT, preferred_element_type=F32)], axis=0)
            chains.append((rows, win, hk, ks, s))

    stage2 = []
    for rows, win, hk, ks, s in chains:
        sk = jnp.concatenate(
            [jnp.full((1, blk), sink_ref[hk * group + j] * LOG2E, F32) for j in range(group)], axis=1)
        m = jnp.maximum(jnp.max(s, axis=0, keepdims=True), sk)
        e = jnp.exp2(s - m)
        den = jnp.sum(e, axis=0, keepdims=True) + jnp.exp2(sk - m)
        stage2.append((rows, win, hk, ks, e.astype(BF16), den))

    for rows, win, hk, ks, e, den in stage2:
        vs = slice(hk * HEAD_DIM, (hk + 1) * HEAD_DIM)
        vt = jnp.concatenate([vt_loc[vs, win], vt_ctx[vs, :]], axis=1)
        vt = jnp.concatenate([vt, vt], axis=0)
        o = (_dot(vt, e) * (1.0 / den)).T
        for p in range(group // 2):
            pair = jnp.where(lo, o[2 * p * blk:(2 * p + 1) * blk], o[(2 * p + 1) * blk:(2 * p + 2) * blk])
            c0 = (hk * group + 2 * p) * HEAD_DIM
            o_ref[rows, c0:c0 + LANES] = pair.astype(BF16)


def _attention(geom, sink, q, k, vt):
    blk, stp = ATT_BLOCK, ATT_STEP
    nsub = stp // blk
    ncs, nss = geom.Lc // stp, geom.S // stp
    nsb = geom.S // blk
    base_s, base_b = geom.n_ctx // stp, geom.n_ctx // blk

    def q_blk(b, i):
        return jnp.where(i < ncs, b * ncs + i, base_s + b * nss + i - ncs)

    def edge_blk(off):
        return lambda b, i: base_b + b * nsb + jnp.clip((i - ncs) * nsub + off, 0, nsb - 1)

    cur_blk = lambda b, i: base_s + b * nss + jnp.clip(i - ncs, 0, nss - 1)
    ctx_blk = lambda b, i: b
    blocks = ((blk, edge_blk(-1)), (stp, cur_blk), (blk, edge_blk(nsub)), (geom.Lc, ctx_blk))
    rows = lambda f: (lambda b, i: (f(b, i), 0))
    cols = lambda f: (lambda b, i: (0, f(b, i)))
    return pl.pallas_call(
        functools.partial(_att_body, geom),
        grid=(geom.B, ncs + nss),
        in_specs=[pl.BlockSpec(memory_space=pltpu.SMEM),
                  pl.BlockSpec((stp, ATT_Q), rows(q_blk))]
                 + [pl.BlockSpec((size, KV_W), rows(f)) for size, f in blocks]
                 + [pl.BlockSpec((ATT_KV, size), cols(f)) for size, f in blocks],
        out_specs=pl.BlockSpec((stp, ATT_Q), rows(q_blk)),
        out_shape=jax.ShapeDtypeStruct(q.shape, BF16),
        compiler_params=_cparams("parallel", "parallel"),
        name="window_attention",
    )(sink, q, k, k, k, k, vt, vt, vt, vt)


def _split3(x):
    hi = x.astype(BF16)
    r1 = x - hi.astype(F32)
    mid = r1.astype(BF16)
    lo = (r1 - mid.astype(F32)).astype(BF16)
    return hi, mid, lo


def _mlstm_body(qf, kf, vf, gtf, qb, kb, vb, gtb, of, ob, st_ref, m_ref):
    @pl.when(pl.program_id(1) == 0)
    def _():
        st_ref[...] = jnp.zeros_like(st_ref)
        m_ref[...] = jnp.full_like(m_ref, NEG_INF)

    L = M_CHUNK
    r = lax.broadcasted_iota(jnp.int32, (L, L), 0)
    c = lax.broadcasted_iota(jnp.int32, (L, L), 1)
    ones = jnp.ones((M_AUG, L), BF16)
    refs = ((qf, kf, vf, gtf, of), (qb, kb, vb, gtb, ob))
    prep = []
    for direction in range(2):
        keep = (r <= c) if direction == 0 else (r >= c)
        tri = jnp.where(keep, 1.0, 0.0).astype(BF16)
        gates_t = refs[direction][3][...]
        hi, mid, lo = _split3(_log_sigmoid(gates_t))
        brow_all = _dot(hi, tri) + _dot(mid, tri) + _dot(lo, tri)
        u0 = direction * M_HEADS
        c_rows = gates_t[u0:u0 + M_HEADS] - brow_all[2 * M_HEADS + u0:2 * M_HEADS + u0 + M_HEADS]
        c_cols = jnp.concatenate([c_rows, jnp.zeros((LANES - M_HEADS, L), F32)], axis=0).T
        prep.append((keep, gates_t, brow_all, c_cols))

    units = []
    for direction in range(2):
        keep, gates_t, brow_all, c_cols = prep[direction]
        q_ref, k_ref = refs[direction][0], refs[direction][1]
        for h in range(M_HEADS):
            u = direction * M_HEADS + h
            hs = slice(h * M_HEAD_DIM, (h + 1) * M_HEAD_DIM)
            q, k = q_ref[:, hs], k_ref[:, hs]
            b_row = brow_all[2 * M_HEADS + u:2 * M_HEADS + u + 1, :]
            i_row = gates_t[u:u + 1, :]
            m_prev = m_ref[u:u + 1, 0:1]
            state = st_ref[u]
            dmat = jnp.where(keep, b_row + c_cols[:, h:h + 1], NEG_INF)
            a_row = b_row + m_prev
            mt = jnp.maximum(a_row, jnp.max(dmat, axis=0, keepdims=True))
            big = lax.dot_general(jnp.concatenate([k, state.astype(BF16)], axis=0), q, _NT,
                                  preferred_element_type=F32)
            units.append((direction, u, hs, k, b_row, i_row, m_prev, state, dmat, a_row, mt, big))

    units2 = []
    for (direction, u, hs, k, b_row, i_row, m_prev, state, dmat, a_row, mt, big) in units:
        vt_aug = jnp.concatenate([refs[direction][2][hs, :], ones], axis=0)
        w_row = jnp.exp(a_row - mt)
        smat = (big[:L] * jnp.exp(dmat - mt)).astype(BF16)
        tot = w_row * big[L:] + _dot(vt_aug, smat)
        den = tot[M_HEAD_DIM:M_HEAD_DIM + 1, :]
        scale = 1.0 / jnp.maximum(jnp.abs(den), jnp.exp(-mt))
        refs[direction][4][:, hs] = (tot[:M_HEAD_DIM] * scale).T
        units2.append((direction, u, k, b_row, i_row, m_prev, state, vt_aug))

    for (direction, u, k, b_row, i_row, m_prev, state, vt_aug) in units2:
        end = L - 1 if direction == 0 else 0
        total = b_row[:, end:end + 1]
        g_row = total - b_row + i_row
        m_new = jnp.maximum(total + m_prev, jnp.max(g_row, axis=1, keepdims=True))
        decay = jnp.exp(total + m_prev - m_new)
        vw = (vt_aug.astype(F32) * jnp.exp(g_row - m_new)).astype(BF16)
        st_ref[u] = decay * state + _dot(vw, k)
        m_ref[u:u + 1, :] = jnp.broadcast_to(m_new, (1, LANES))


def _mlstm_scan(geom, q, k, vt, gates_t):
    L = M_CHUNK
    ncc, nsc = geom.Lc // L, geom.S // L
    base = geom.n_ctx // L

    def fwd(b, j):
        return jnp.where(j < ncc, b * ncc + j, base + b * nsc + j - ncc)

    def bwd(b, j):
        return jnp.where(j < ncc, b * ncc + ncc - 1 - j, base + b * nsc + nsc - 1 - (j - ncc))

    def specs(m):
        rows = lambda b, j: (m(b, j), 0)
        cols = lambda b, j: (0, m(b, j))
        return [pl.BlockSpec((L, M_WIDTH), rows), pl.BlockSpec((L, M_WIDTH), rows),
                pl.BlockSpec((M_WIDTH, L), cols), pl.BlockSpec((N_GATE_COLS, L), cols)]

    n = q.shape[0]
    return pl.pallas_call(
        _mlstm_body,
        grid=(geom.B, ncc + nsc),
        in_specs=specs(fwd) + specs(bwd),
        out_specs=[pl.BlockSpec((L, M_WIDTH), lambda b, j: (fwd(b, j), 0)),
                   pl.BlockSpec((L, M_WIDTH), lambda b, j: (bwd(b, j), 0))],
        out_shape=[jax.ShapeDtypeStruct((n, M_WIDTH), F32)] * 2,
        scratch_shapes=[pltpu.VMEM((2 * M_HEADS, M_HEAD_DIM + M_AUG, M_HEAD_DIM), F32),
                        pltpu.VMEM((2 * M_HEADS, LANES), F32)],
        compiler_params=_cparams("parallel", "arbitrary"),
        name="mlstm_scan",
    )(q, k, vt, gates_t, q, k, vt, gates_t)


def _mix_ffn_body(final, x_ref, mod_ref, att_ref, cact_ref, hf_ref, hb_ref, om_ref, br_ref, gm_ref,
                  g_ref, gf_ref, wa_ref, wp_ref, wm_ref, wo_ref, wg_ref, wu_ref, wd_ref, o_ref):
    d = x_ref.shape[1]
    mod = mod_ref[0]
    hm = _sigmoid(om_ref[...].astype(F32)) * (hf_ref[...] + hb_ref[...])
    parts = []
    for h in range(M_HEADS):
        t = hm[:, h * M_HEAD_DIM:(h + 1) * M_HEAD_DIM]
        mu = jnp.mean(t, axis=-1, keepdims=True)
        tc = t - mu
        var = jnp.mean(tc * tc, axis=-1, keepdims=True)
        parts.append(tc * lax.rsqrt(var + EPS))
    hn = (jnp.concatenate(parts, axis=-1) * gm_ref[...]).astype(BF16)
    ya = _dot(att_ref[...], wa_ref[...])
    yb = _dot(cact_ref[...], wp_ref[...])
    yc = _dot(hn, wm_ref[...])
    merged = (_sigmoid(br_ref[:, 0:d].astype(F32)) * ya + _sigmoid(br_ref[:, d:2 * d].astype(F32)) * yb
              + _sigmoid(br_ref[:, 2 * d:3 * d].astype(F32)) * yc)
    x = x_ref[...] + mod[:, 2 * d:3 * d] * _dot(merged.astype(BF16), wo_ref[...])
    h = _norm_mod(x, g_ref[...], mod[:, 4 * d:5 * d], mod[:, 3 * d:4 * d]).astype(BF16)
    ff = wg_ref.shape[1]
    half = -(-ff // (2 * MXU_TILE)) * MXU_TILE
    acc = None
    for sl in (slice(0, half), slice(half, ff)):
        act = (_silu(_dot(h, wg_ref[:, sl])) * _dot(h, wu_ref[:, sl])).astype(BF16)
        t = _dot(act, wd_ref[sl, :])
        acc = t if acc is None else acc + t
    y = x + mod[:, 5 * d:6 * d] * acc
    o_ref[...] = _rms(y) * gf_ref[...] if final else y


def _mix_ffn(geom, l, x, mod3, att, cact, hf, hb, om, br, g_mnorm, g_ffn, g_final, w, latent_only):
    n, d = x.shape
    tm = _row_tile(geom, 256)
    skip = geom.n_ctx // tm if latent_only else 0
    row = lambda i: (i + skip, 0)
    mod_map = lambda i: (geom.mod_row(i + skip, tm), 0, 0)
    weights = (w["att_out"], w["pw"], w["mout"], w["out"], w["ff_gate"], w["ff_up"], w["ff_down"])
    return pl.pallas_call(
        functools.partial(_mix_ffn_body, latent_only),
        grid=(n // tm - skip,),
        in_specs=[pl.BlockSpec((tm, d), row),
                  pl.BlockSpec((1, 1, mod3.shape[2]), mod_map),
                  pl.BlockSpec((tm, ATT_Q), row), pl.BlockSpec((tm, CONV_DIM), row),
                  pl.BlockSpec((tm, M_WIDTH), row), pl.BlockSpec((tm, M_WIDTH), row),
                  pl.BlockSpec((tm, M_WIDTH), row), pl.BlockSpec((tm, 3 * d), row),
                  _resident((1, M_WIDTH)), _resident((1, d)), _resident((1, d))]
                 + [_layer_resident(a, l) for a in weights],
        out_specs=pl.BlockSpec((tm, d), lambda i: (i, 0)),
        out_shape=jax.ShapeDtypeStruct((n - skip * tm, d), F32),
        compiler_params=_cparams("parallel"),
        name="mix_ffn",
    )(x, mod3, att, cact, hf, hb, om, br, g_mnorm, g_ffn, g_final, *weights)


def _rope_table(geom):
    p = jnp.arange(geom.S)
    n_freq = HEAD_DIM // 4
    inv_freq = ROPE_BASE ** (-jnp.arange(n_freq, dtype=F32) / n_freq)
    ang_r = (p // GRID_W).astype(F32)[:, None] * inv_freq
    ang_c = (p % GRID_W).astype(F32)[:, None] * inv_freq
    ang = jnp.concatenate([ang_r, ang_r, ang_c, ang_c], axis=-1)
    reps = LANES // HEAD_DIM
    cos = jnp.tile(jnp.cos(ang), (1, reps))
    sin = jnp.tile(jnp.sin(ang), (1, reps))
    first_half = (jnp.arange(LANES) % (HEAD_DIM // 2)) < HEAD_DIM // 4
    lat = jnp.concatenate([cos, jnp.where(first_half, -sin, 0.0), jnp.where(first_half, 0.0, sin)], axis=1)
    ident = jnp.concatenate([jnp.ones((geom.Lc, LANES), F32), jnp.zeros((geom.Lc, 2 * LANES), F32)], axis=1)
    return jnp.concatenate([ident, lat], axis=0)


_TOK = {"qk": slice(0, ATT_Q + ATT_KV)}
_off = ATT_Q + 2 * ATT_KV
for _name, _size in (("glu", 2 * CONV_DIM), ("qkm", 2 * M_WIDTH), ("vm", M_WIDTH), ("om", M_WIDTH)):
    _TOK[_name] = slice(_off, _off + _size)
    _off += _size
_TOK_W = _off
assert all(sl.start % LANES == 0 for sl in _TOK.values()) and _TOK_W % LANES == 0


def _pack_weights(d, w_in, w_att_out, w_conv_pw, w_mlstm_out, w_out, w_ff_gate, w_ff_up, w_ff_down):
    return {
        "in_t": jnp.swapaxes(w_in, 1, 2).astype(BF16),
        "att_out": w_att_out.astype(BF16), "pw": w_conv_pw.astype(BF16), "mout": w_mlstm_out.astype(BF16),
        "out": w_out.astype(BF16), "ff_gate": w_ff_gate.astype(BF16), "ff_up": w_ff_up.astype(BF16),
        "ff_down": w_ff_down.astype(BF16),
    }


def kernel(x, c, ctx, c_ctx, w_ada, b_ada, g_norm_mix, g_norm_ffn, w_in, b_mgate, att_sink, w_att_out, w_conv_dw, b_conv_dw, g_conv_ln, b_conv_ln, w_conv_pw, w_mconv, g_mlstm_norm, w_mlstm_out, w_out, w_ff_gate, w_ff_up, w_ff_down, g_final):
    batch, seq, d = x.shape
    ctx_len = ctx.shape[1]
    depth = w_ada.shape[0]
    geom = _Geom(batch, seq, ctx_len)
    assert seq % ATT_STEP == 0 and ctx_len % ATT_STEP == 0 and seq % GRID_W == 0
    assert seq % M_CHUNK == 0 and ctx_len % M_CHUNK == 0 and d % LANES == 0
    assert LANES == 2 * HEAD_DIM and ATT_KV == LANES

    mod_rows = -(-(batch + 1) // SUBLANES) * SUBLANES
    cvec = jnp.zeros((mod_rows, d), F32).at[:batch].set(c).at[batch].set(c_ctx)
    mod = _modulation(cvec, w_ada, b_ada)
    rope = _rope_table(geom)
    xs = _assemble(geom, ctx.reshape(geom.n_ctx, d), x.reshape(batch * seq, d))
    g_fin = g_final.reshape(1, d)
    w = _pack_weights(d, w_in, w_att_out, w_conv_pw, w_mlstm_out, w_out, w_ff_gate, w_ff_up, w_ff_down)

    for l in range(depth):
        last = l == depth - 1
        mod3 = mod[l].reshape(mod_rows, 1, 6 * d)
        conv = (jnp.repeat(w_conv_dw[l], SUBLANES, axis=0), b_conv_dw[l].reshape(1, -1),
                g_conv_ln[l].reshape(1, -1), b_conv_ln[l].reshape(1, -1))
        q, k, vt, cact, qm, km, vmt, om, gates_t, br = _projection(
            geom, l, xs, mod3, g_norm_mix[l].reshape(1, d), rope, b_mgate[l], w_mconv[l], conv, w)
        att = _attention(geom, att_sink[l], q, k, vt)
        hf, hb = _mlstm_scan(geom, qm, km, vmt, gates_t)
        xs = _mix_ffn(geom, l, xs, mod3, att, cact, hf, hb, om, br, g_mlstm_norm[l].reshape(1, -1),
                      g_norm_ffn[l].reshape(1, d), g_fin, w, last)

    return xs.reshape(batch, seq, d)
```

```python
import functools

import jax
import jax.numpy as jnp
from jax import lax
from jax.experimental import pallas as pl
from jax.experimental.pallas import tpu as pltpu

F32 = jnp.float32
BF16 = jnp.bfloat16

GRID_W = 64
N_Q_HEADS = 8
N_KV_HEADS = 2
HEAD_DIM = 64
WINDOW = 128
ATT_BLOCK = 128
ROPE_BASE = 10000.0
ATT_Q = N_Q_HEADS * HEAD_DIM
ATT_KV = N_KV_HEADS * HEAD_DIM
CONV_DIM = 512
CONV_WIDTH = 31
M_HEADS = 4
M_HEAD_DIM = 128
M_WIDTH = M_HEADS * M_HEAD_DIM
M_SHORT_CONV = 3
N_GATE_COLS = 4 * M_HEADS
EPS = 1e-6
NEG_INF = -1e30
LOG2E = 1.4426950408889634

LANES = 128
SUBLANES = 8
MXU_TILE = 256
M_CHUNK = 256
M_AUG = 16
ATT_STEP = 256
CONV_HALO = 16
KV_W = N_KV_HEADS * LANES
VMEM_LIMIT = 52 * 1024 * 1024


def _cparams(*sem):
    return pltpu.CompilerParams(dimension_semantics=sem, vmem_limit_bytes=VMEM_LIMIT)


def _resident(shape):
    nd = len(shape)
    return pl.BlockSpec(shape, lambda *_: (0,) * nd, pipeline_mode=pl.Buffered(1))


def _layer_resident(arr, l):
    nd = arr.ndim
    return pl.BlockSpec((None,) + arr.shape[1:], lambda *_: (l,) + (0,) * (nd - 1), pipeline_mode=pl.Buffered(1))


def _sigmoid(x):
    return 1.0 / (1.0 + jnp.exp(-x))


def _silu(x):
    return x * _sigmoid(x)


def _log_sigmoid(x):
    return jnp.minimum(x, 0.0) - jnp.log(1.0 + jnp.exp(-jnp.abs(x)))


def _rms(x):
    return x * lax.rsqrt(jnp.mean(x * x, axis=-1, keepdims=True) + EPS)


def _norm_mod(x, g, sc, sh):
    return (_rms(x) * g) * (1.0 + sc) + sh


_NT = (((1,), (1,)), ((), ()))


def _ntdot(a, b):
    return lax.dot_general(a, b, _NT, preferred_element_type=F32)


def _dot(a, b):
    return jnp.dot(a, b, preferred_element_type=F32)


def _mod_body(c_ref, w_ref, b_ref, o_ref):
    h = _silu(c_ref[...]).astype(BF16)
    o_ref[0] = _dot(h, w_ref[0].astype(BF16)) + b_ref[0]


def _modulation(cvec, w_ada, b_ada):
    depth, d, n6 = w_ada.shape
    rows = cvec.shape[0]
    tn = n6 // 4
    return pl.pallas_call(
        _mod_body,
        grid=(depth, n6 // tn),
        in_specs=[pl.BlockSpec((rows, d), lambda l, j: (0, 0)),
                  pl.BlockSpec((1, d, tn), lambda l, j: (l, 0, j)),
                  pl.BlockSpec((1, 1, tn), lambda l, j: (l, 0, j))],
        out_specs=pl.BlockSpec((1, rows, tn), lambda l, j: (l, 0, j)),
        out_shape=jax.ShapeDtypeStruct((depth, rows, n6), F32),
        compiler_params=_cparams("arbitrary", "arbitrary"),
        name="ada_mod",
    )(cvec, w_ada, b_ada.reshape(depth, 1, n6))


class _Geom:
    def __init__(self, batch, seq, ctx_len):
        self.B, self.S, self.Lc = batch, seq, ctx_len
        self.n_ctx = batch * ctx_len
        self.N = self.n_ctx + batch * seq

    def mod_row(self, tile, tm):
        n_ctx_tiles = self.n_ctx // tm
        return jnp.where(tile < n_ctx_tiles, self.B, (tile - n_ctx_tiles) // (self.S // tm))

    def seq_tile(self, tile, tm):
        n_ctx_tiles = self.n_ctx // tm
        tc, ts = self.Lc // tm, self.S // tm
        return jnp.where(tile < n_ctx_tiles, tile % tc, tc + (tile - n_ctx_tiles) % ts)

    def seq_edges(self, tile, tm):
        n_ctx_tiles = self.n_ctx // tm
        tc, ts = self.Lc // tm, self.S // tm
        pos = jnp.where(tile < n_ctx_tiles, tile % tc, (tile - n_ctx_tiles) % ts)
        last = jnp.where(tile < n_ctx_tiles, tc - 1, ts - 1)
        return pos == 0, pos == last


def _row_tile(geom, want, within_sequence=False):
    tm = want
    while geom.n_ctx % tm or geom.S % tm or (within_sequence and geom.Lc % tm):
        tm //= 2
    return tm


def _halo_specs(geom, tm, halo, width):
    per = tm // halo
    last = geom.N // halo - 1
    return [pl.BlockSpec((halo, width), lambda i: (jnp.maximum(i * per - 1, 0), 0)),
            pl.BlockSpec((tm, width), lambda i: (i, 0)),
            pl.BlockSpec((halo, width), lambda i: (jnp.minimum((i + 1) * per, last), 0))]


def _assemble_body(n_ctx_tiles, ctx_ref, x_ref, o_ref):
    o_ref[...] = jnp.where(pl.program_id(0) < n_ctx_tiles, ctx_ref[...], x_ref[...])


def _assemble(geom, ctx2, x2):
    d = x2.shape[1]
    tm = _row_tile(geom, 1024)
    nct = geom.n_ctx // tm
    return pl.pallas_call(
        functools.partial(_assemble_body, nct),
        grid=(geom.N // tm,),
        in_specs=[pl.BlockSpec((tm, d), lambda i: (jnp.minimum(i, nct - 1), 0)),
                  pl.BlockSpec((tm, d), lambda i: (jnp.maximum(i - nct, 0), 0))],
        out_specs=pl.BlockSpec((tm, d), lambda i: (i, 0)),
        out_shape=jax.ShapeDtypeStruct((geom.N, d), F32),
        compiler_params=_cparams("arbitrary"),
        name="assemble_tokens",
    )(ctx2, x2)


def _rope(x, cos, sin_a, sin_b):
    return x * cos + pltpu.roll(x, LANES - 16, 1) * sin_a + pltpu.roll(x, 16, 1) * sin_b


def _proj_body(geom, tm, xp_ref, x_ref, xn_ref, mod_ref, g_ref, rope_ref, bgt_ref, wc_ref,
               wdw_ref, bdw_ref, gln_ref, bln_ref,
               wt_ref,
               q_ref, k_ref, vt_ref, cact_ref, qm_ref, km_ref, vmt_ref, om_ref, gatet_ref, br_ref,
               pad_ref, sh_ref):
    d = x_ref.shape[1]
    mod = mod_ref[0]
    g, sc, sh = g_ref[...], mod[:, d:2 * d], mod[:, 0:d]
    h = _norm_mod(x_ref[...], g, sc, sh).astype(BF16)
    cos, sa, sb = rope_ref[:, 0:LANES], rope_ref[:, LANES:2 * LANES], rope_ref[:, 2 * LANES:3 * LANES]

    halo = CONV_HALO
    first, last = geom.seq_edges(pl.program_id(0), tm)
    h_halo = _norm_mod(jnp.concatenate([xp_ref[...], xn_ref[...]], axis=0), g, sc, sh).astype(BF16)
    h_ext = jnp.concatenate([h, h_halo], axis=0)

    glu = _ntdot(h_ext, wt_ref[_TOK["glu"], :])
    u = glu[:, :CONV_DIM] * _sigmoid(glu[:, CONV_DIM:])
    sh_ref[0, 0:halo, :] = jnp.where(first, 0.0, u[tm:tm + halo])
    sh_ref[0, halo:halo + tm, :] = u[0:tm]
    sh_ref[0, halo + tm:2 * halo + tm, :] = jnp.where(last, 0.0, u[tm + halo:tm + 2 * halo])
    span = tm + 2 * halo - SUBLANES
    for s_ in range(1, SUBLANES):
        sh_ref[s_, 0:span, :] = sh_ref[0, s_:s_ + span, :]
    rows = 32

    def conv_rows(r0, zero):
        accs = [zero, None]
        for j in range(CONV_WIDTH):
            a_, s_ = divmod(halo + j - CONV_WIDTH // 2, SUBLANES)
            off = r0 + a_ * SUBLANES
            win = sh_ref[s_, off:off + rows, :].reshape(rows // SUBLANES, SUBLANES, -1)
            t = win * wdw_ref[j * SUBLANES:(j + 1) * SUBLANES, :][None]
            accs[j % 2] = t if accs[j % 2] is None else accs[j % 2] + t
        acc = accs[0] + accs[1]
        yv = acc.reshape(rows, -1) + bdw_ref[...]
        mu = jnp.mean(yv, axis=-1, keepdims=True)
        yc = yv - mu
        var = jnp.mean(yc * yc, axis=-1, keepdims=True)
        z = yc * lax.rsqrt(var + EPS) * gln_ref[...] + bln_ref[...]
        cact_ref[r0:r0 + rows, :] = _silu(z).astype(BF16)

    def late_zero(part):
        bits = pltpu.bitcast(part[0:rows, 0:CONV_DIM], jnp.uint32)
        return pltpu.bitcast((bits >> 16) >> 16, F32).reshape(rows // SUBLANES, SUBLANES, -1)

    n_groups = tm // rows
    br_cols = 2 * MXU_TILE
    br0 = _TOK_W + N_GATE_COLS
    n_br = (wt_ref.shape[0] - br0) // br_cols
    assert n_br + 2 == n_groups
    qkm = _ntdot(h_ext, wt_ref[_TOK["qkm"], :])
    zero = None
    qkv = om = None
    for gi in range(n_groups):
        conv_rows(gi * rows, zero)
        if gi < n_br:
            cs = slice(gi * br_cols, (gi + 1) * br_cols)
            part = _ntdot(h, wt_ref[br0 + gi * br_cols:br0 + (gi + 1) * br_cols, :])
            br_ref[:, cs] = part.astype(BF16)
        elif gi == n_br:
            part = qkv = _ntdot(h, wt_ref[_TOK["qk"], :])
        else:
            part = om = _ntdot(h, wt_ref[_TOK["om"], :])
        zero = late_zero(part)
    om_ref[...] = om.astype(BF16)

    pad_ref[0:halo, :] = jnp.where(first, 0.0, qkm[tm:tm + halo])
    pad_ref[halo:halo + tm, :] = qkm[0:tm]
    pad_ref[halo + tm:2 * halo + tm, :] = jnp.where(last, 0.0, qkm[tm + halo:tm + 2 * halo])
    v0 = ATT_Q + ATT_KV
    vmt = _ntdot(wt_ref[_TOK["vm"], :], h)
    vtt = _ntdot(wt_ref[v0:v0 + ATT_KV, :], h)
    gtt = _ntdot(wt_ref[_TOK_W:_TOK_W + N_GATE_COLS, :], h)
    acc = None
    for j in range(M_SHORT_CONV):
        off = halo + j - M_SHORT_CONV // 2
        t = pad_ref[off:off + tm, :] * wc_ref[j:j + 1, :]
        acc = t if acc is None else acc + t
    y = _silu(acc)
    qm_ref[...] = y[:, :M_WIDTH].astype(BF16)
    km_ref[...] = (y[:, M_WIDTH:] * M_HEAD_DIM ** -0.5).astype(BF16)

    scale = HEAD_DIM ** -0.5 * LOG2E
    for j in range(ATT_Q // LANES):
        sl = slice(j * LANES, (j + 1) * LANES)
        q_ref[:, sl] = _rope(qkv[:, sl] * scale, cos, sa, sb).astype(BF16)
    lo = lax.broadcasted_iota(jnp.int32, (tm, LANES), 1) < HEAD_DIM
    keys = _rope(qkv[:, ATT_Q:ATT_Q + ATT_KV], cos, sa, sb)
    swapped = pltpu.roll(keys, HEAD_DIM, 1)
    k_ref[:, 0:LANES] = jnp.where(lo, keys, swapped).astype(BF16)
    k_ref[:, LANES:2 * LANES] = jnp.where(lo, swapped, keys).astype(BF16)
    vmt_ref[...] = vmt.astype(BF16)
    vt_ref[...] = vtt.astype(BF16)
    gatet_ref[...] = gtt + bgt_ref[...]


def _projection(geom, l, x, mod3, g, rope, b_gate, w_mconv, conv, w):
    n, d = x.shape
    tm = _row_tile(geom, 256, within_sequence=True)
    row = lambda i: (i, 0)
    col = lambda i: (0, i)
    mod_map = lambda i: (geom.mod_row(i, tm), 0, 0)
    rope_map = lambda i: (geom.seq_tile(i, tm), 0)
    outs = ((ATT_Q, BF16, False), (KV_W, BF16, False), (ATT_KV, BF16, True), (CONV_DIM, BF16, False),
            (M_WIDTH, BF16, False), (M_WIDTH, BF16, False), (M_WIDTH, BF16, True), (M_WIDTH, BF16, False),
            (N_GATE_COLS, F32, True), (3 * d, BF16, False))
    return pl.pallas_call(
        functools.partial(_proj_body, geom, tm),
        grid=(n // tm,),
        in_specs=_halo_specs(geom, tm, CONV_HALO, d)
                 + [pl.BlockSpec((1, 1, mod3.shape[2]), mod_map),
                    _resident((1, d)),
                    pl.BlockSpec((tm, rope.shape[1]), rope_map),
                    _resident((N_GATE_COLS, 1)), _resident(w_mconv.shape)]
                 + [_resident(a.shape) for a in conv]
                 + [_layer_resident(w["in_t"], l)],
        out_specs=[pl.BlockSpec((c, tm), col) if fm else pl.BlockSpec((tm, c), row) for c, _, fm in outs],
        out_shape=[jax.ShapeDtypeStruct((c, n) if fm else (n, c), t) for c, t, fm in outs],
        scratch_shapes=[pltpu.VMEM((tm + 2 * CONV_HALO, 2 * M_WIDTH), F32),
                        pltpu.VMEM((SUBLANES, tm + 2 * CONV_HALO, CONV_DIM), F32)],
        compiler_params=_cparams("parallel"),
        name="in_proj",
    )(x, x, x, mod3, g, rope, b_gate.reshape(N_GATE_COLS, 1), w_mconv, *conv, w["in_t"])


def _att_body(geom, sink_ref, q_ref, kp_ref, kc_ref, kn_ref, kx_ref, vp_ref, vc_ref, vn_ref, vx_ref, o_ref):
    blk = ATT_BLOCK
    nsub = ATT_STEP // blk
    step = pl.program_id(1) - geom.Lc // ATT_STEP
    group = N_Q_HEADS // N_KV_HEADS
    kk = lax.broadcasted_iota(jnp.int32, (3 * blk, blk), 0)
    qq = lax.broadcasted_iota(jnp.int32, (3 * blk, blk), 1)
    lo = lax.broadcasted_iota(jnp.int32, (blk, LANES), 1) < HEAD_DIM
    zero = jnp.zeros((blk, LANES), BF16)
    k_loc = jnp.concatenate([kp_ref[...], kc_ref[...], kn_ref[...]], axis=0)
    vt_loc = jnp.concatenate([vp_ref[...], vc_ref[...], vn_ref[...]], axis=1)
    k_ctx, vt_ctx = kx_ref[...], vx_ref[...]
    chains = []
    for sub in range(nsub):
        n = step * nsub + sub
        key_pos = (n - 1) * blk + kk
        q_pos = n * blk + qq
        valid = (jnp.abs(key_pos - q_pos) <= WINDOW) & (key_pos >= 0) & (key_pos < geom.S) & (n >= 0)
        bias = jnp.where(valid, 0.0, NEG_INF)
        bias = jnp.concatenate([bias] * group, axis=1)
        rows = slice(sub * blk, (sub + 1) * blk)
        win = slice(sub * blk, (sub + 3) * blk)
        for hk in range(N_KV_HEADS):
            ks = slice(hk * LANES, (hk + 1) * LANES)
            tiles = [q_ref[rows, (hk * group + 2 * p) * HEAD_DIM:(hk * group + 2 * p + 2) * HEAD_DIM]
                     for p in range(group // 2)]
            qs = jnp.concatenate([jnp.where(lo if half == 0 else ~lo, t, zero)
                                  for t in tiles for half in (0, 1)], axis=0)
            s = jnp.concatenate(
                [lax.dot_general(k_loc[win, ks], qs, _NT, preferred_element_type=F32) + bias,
                 lax.dot_general(k_ctx[:, ks], qs, _NT, preferred_element_type=F32)], axis=0)
            chains.append((rows, win, hk, ks, s))

    stage2 = []
    for rows, win, hk, ks, s in chains:
        sk = jnp.concatenate(
            [jnp.full((1, blk), sink_ref[hk * group + j] * LOG2E, F32) for j in range(group)], axis=1)
        m = jnp.maximum(jnp.max(s, axis=0, keepdims=True), sk)
        e = jnp.exp2(s - m)
        den = jnp.sum(e, axis=0, keepdims=True) + jnp.exp2(sk - m)
        stage2.append((rows, win, hk, ks, e.astype(BF16), den))

    for rows, win, hk, ks, e, den in stage2:
        vs = slice(hk * HEAD_DIM, (hk + 1) * HEAD_DIM)
        vt = jnp.concatenate([vt_loc[vs, win], vt_ctx[vs, :]], axis=1)
        vt = jnp.concatenate([vt, vt], axis=0)
        o = (_dot(vt, e) * (1.0 / den)).T
        for p in range(group // 2):
            pair = jnp.where(lo, o[2 * p * blk:(2 * p + 1) * blk], o[(2 * p + 1) * blk:(2 * p + 2) * blk])
            c0 = (hk * group + 2 * p) * HEAD_DIM
            o_ref[rows, c0:c0 + LANES] = pair.astype(BF16)


def _attention(geom, sink, q, k, vt):
    blk, stp = ATT_BLOCK, ATT_STEP
    nsub = stp // blk
    ncs, nss = geom.Lc // stp, geom.S // stp
    nsb = geom.S // blk
    base_s, base_b = geom.n_ctx // stp, geom.n_ctx // blk

    def q_blk(b, i):
        return jnp.where(i < ncs, b * ncs + i, base_s + b * nss + i - ncs)

    def edge_blk(off):
        return lambda b, i: base_b + b * nsb + jnp.clip((i - ncs) * nsub + off, 0, nsb - 1)

    cur_blk = lambda b, i: base_s + b * nss + jnp.clip(i - ncs, 0, nss - 1)
    ctx_blk = lambda b, i: b
    blocks = ((blk, edge_blk(-1)), (stp, cur_blk), (blk, edge_blk(nsub)), (geom.Lc, ctx_blk))
    rows = lambda f: (lambda b, i: (f(b, i), 0))
    cols = lambda f: (lambda b, i: (0, f(b, i)))
    return pl.pallas_call(
        functools.partial(_att_body, geom),
        grid=(geom.B, ncs + nss),
        in_specs=[pl.BlockSpec(memory_space=pltpu.SMEM),
                  pl.BlockSpec((stp, ATT_Q), rows(q_blk))]
                 + [pl.BlockSpec((size, KV_W), rows(f)) for size, f in blocks]
                 + [pl.BlockSpec((ATT_KV, size), cols(f)) for size, f in blocks],
        out_specs=pl.BlockSpec((stp, ATT_Q), rows(q_blk)),
        out_shape=jax.ShapeDtypeStruct(q.shape, BF16),
        compiler_params=_cparams("parallel", "parallel"),
        name="window_attention",
    )(sink, q, k, k, k, k, vt, vt, vt, vt)


def _split3(x):
    hi = x.astype(BF16)
    r1 = x - hi.astype(F32)
    mid = r1.astype(BF16)
    lo = (r1 - mid.astype(F32)).astype(BF16)
    return hi, mid, lo


def _mlstm_body(qf, kf, vf, gtf, qb, kb, vb, gtb, of, ob, st_ref, m_ref):
    @pl.when(pl.program_id(1) == 0)
    def _():
        st_ref[...] = jnp.zeros_like(st_ref)
        m_ref[...] = jnp.full_like(m_ref, NEG_INF)

    L = M_CHUNK
    r = lax.broadcasted_iota(jnp.int32, (L, L), 0)
    c = lax.broadcasted_iota(jnp.int32, (L, L), 1)
    ones = jnp.ones((M_AUG, L), BF16)
    refs = ((qf, kf, vf, gtf, of), (qb, kb, vb, gtb, ob))
    prep = []
    for direction in range(2):
        keep = (r <= c) if direction == 0 else (r >= c)
        tri = jnp.where(keep, 1.0, 0.0).astype(BF16)
        gates_t = refs[direction][3][...]
        hi, mid, lo = _split3(_log_sigmoid(gates_t))
        brow_all = _dot(hi, tri) + _dot(mid, tri) + _dot(lo, tri)
        u0 = direction * M_HEADS
        c_rows = gates_t[u0:u0 + M_HEADS] - brow_all[2 * M_HEADS + u0:2 * M_HEADS + u0 + M_HEADS]
        c_cols = jnp.concatenate([c_rows, jnp.zeros((LANES - M_HEADS, L), F32)], axis=0).T
        prep.append((keep, gates_t, brow_all, c_cols))

    units = []
    for direction in range(2):
        keep, gates_t, brow_all, c_cols = prep[direction]
        q_ref, k_ref = refs[direction][0], refs[direction][1]
        for h in range(M_HEADS):
            u = direction * M_HEADS + h
            hs = slice(h * M_HEAD_DIM, (h + 1) * M_HEAD_DIM)
            q, k = q_ref[:, hs], k_ref[:, hs]
            b_row = brow_all[2 * M_HEADS + u:2 * M_HEADS + u + 1, :]
            i_row = gates_t[u:u + 1, :]
            m_prev = m_ref[u:u + 1, 0:1]
            state = st_ref[u]
            dmat = jnp.where(keep, b_row + c_cols[:, h:h + 1], NEG_INF)
            a_row = b_row + m_prev
            mt = jnp.maximum(a_row, jnp.max(dmat, axis=0, keepdims=True))
            big = lax.dot_general(jnp.concatenate([k, state.astype(BF16)], axis=0), q, _NT,
                                  preferred_element_type=F32)
            units.append((direction, u, hs, k, b_row, i_row, m_prev, state, dmat, a_row, mt, big))

    units2 = []
    for (direction, u, hs, k, b_row, i_row, m_prev, state, dmat, a_row, mt, big) in units:
        vt_aug = jnp.concatenate([refs[direction][2][hs, :], ones], axis=0)
        w_row = jnp.exp(a_row - mt)
        smat = (big[:L] * jnp.exp(dmat - mt)).astype(BF16)
        tot = w_row * big[L:] + _dot(vt_aug, smat)
        den = tot[M_HEAD_DIM:M_HEAD_DIM + 1, :]
        scale = 1.0 / jnp.maximum(jnp.abs(den), jnp.exp(-mt))
        refs[direction][4][:, hs] = (tot[:M_HEAD_DIM] * scale).T
        units2.append((direction, u, k, b_row, i_row, m_prev, state, vt_aug))

    for (direction, u, k, b_row, i_row, m_prev, state, vt_aug) in units2:
        end = L - 1 if direction == 0 else 0
        total = b_row[:, end:end + 1]
        g_row = total - b_row + i_row
        m_new = jnp.maximum(total + m_prev, jnp.max(g_row, axis=1, keepdims=True))
        decay = jnp.exp(total + m_prev - m_new)
        vw = (vt_aug.astype(F32) * jnp.exp(g_row - m_new)).astype(BF16)
        st_ref[u] = decay * state + _dot(vw, k)
        m_ref[u:u + 1, :] = jnp.broadcast_to(m_new, (1, LANES))


def _mlstm_scan(geom, q, k, vt, gates_t):
    L = M_CHUNK
    ncc, nsc = geom.Lc // L, geom.S // L
    base = geom.n_ctx // L

    def fwd(b, j):
        return jnp.where(j < ncc, b * ncc + j, base + b * nsc + j - ncc)

    def bwd(b, j):
        return jnp.where(j < ncc, b * ncc + ncc - 1 - j, base + b * nsc + nsc - 1 - (j - ncc))

    def specs(m):
        rows = lambda b, j: (m(b, j), 0)
        cols = lambda b, j: (0, m(b, j))
        return [pl.BlockSpec((L, M_WIDTH), rows), pl.BlockSpec((L, M_WIDTH), rows),
                pl.BlockSpec((M_WIDTH, L), cols), pl.BlockSpec((N_GATE_COLS, L), cols)]

    n = q.shape[0]
    return pl.pallas_call(
        _mlstm_body,
        grid=(geom.B, ncc + nsc),
        in_specs=specs(fwd) + specs(bwd),
        out_specs=[pl.BlockSpec((L, M_WIDTH), lambda b, j: (fwd(b, j), 0)),
                   pl.BlockSpec((L, M_WIDTH), lambda b, j: (bwd(b, j), 0))],
        out_shape=[jax.ShapeDtypeStruct((n, M_WIDTH), F32)] * 2,
        scratch_shapes=[pltpu.VMEM((2 * M_HEADS, M_HEAD_DIM + M_AUG, M_HEAD_DIM), F32),
                        pltpu.VMEM((2 * M_HEADS, LANES), F32)],
        compiler_params=_cparams("parallel", "arbitrary"),
        name="mlstm_scan",
    )(q, k, vt, gates_t, q, k, vt, gates_t)


def _mix_ffn_body(final, x_ref, mod_ref, att_ref, cact_ref, hf_ref, hb_ref, om_ref, br_ref, gm_ref,
                  g_ref, gf_ref, wa_ref, wp_ref, wm_ref, wo_ref, wg_ref, wu_ref, wd_ref, o_ref):
    d = x_ref.shape[1]
    mod = mod_ref[0]
    hm = _sigmoid(om_ref[...].astype(F32)) * (hf_ref[...] + hb_ref[...])
    parts = []
    for h in range(M_HEADS):
        t = hm[:, h * M_HEAD_DIM:(h + 1) * M_HEAD_DIM]
        mu = jnp.mean(t, axis=-1, keepdims=True)
        tc = t - mu
        var = jnp.mean(tc * tc, axis=-1, keepdims=True)
        parts.append(tc * lax.rsqrt(var + EPS))
    hn = (jnp.concatenate(parts, axis=-1) * gm_ref[...]).astype(BF16)
    ya = _dot(att_ref[...], wa_ref[...])
    yb = _dot(cact_ref[...], wp_ref[...])
    yc = _dot(hn, wm_ref[...])
    merged = (_sigmoid(br_ref[:, 0:d].astype(F32)) * ya + _sigmoid(br_ref[:, d:2 * d].astype(F32)) * yb
              + _sigmoid(br_ref[:, 2 * d:3 * d].astype(F32)) * yc)
    x = x_ref[...] + mod[:, 2 * d:3 * d] * _dot(merged.astype(BF16), wo_ref[...])
    h = _norm_mod(x, g_ref[...], mod[:, 4 * d:5 * d], mod[:, 3 * d:4 * d]).astype(BF16)
    ff = wg_ref.shape[1]
    half = -(-ff // (2 * MXU_TILE)) * MXU_TILE
    acc = None
    for sl in (slice(0, half), slice(half, ff)):
        act = (_silu(_dot(h, wg_ref[:, sl])) * _dot(h, wu_ref[:, sl])).astype(BF16)
        t = _dot(act, wd_ref[sl, :])
        acc = t if acc is None else acc + t
    y = x + mod[:, 5 * d:6 * d] * acc
    o_ref[...] = _rms(y) * gf_ref[...] if final else y


def _mix_ffn(geom, l, x, mod3, att, cact, hf, hb, om, br, g_mnorm, g_ffn, g_final, w, latent_only):
    n, d = x.shape
    tm = _row_tile(geom, 512)
    skip = geom.n_ctx // tm if latent_only else 0
    row = lambda i: (i + skip, 0)
    mod_map = lambda i: (geom.mod_row(i + skip, tm), 0, 0)
    weights = (w["att_out"], w["pw"], w["mout"], w["out"], w["ff_gate"], w["ff_up"], w["ff_down"])
    return pl.pallas_call(
        functools.partial(_mix_ffn_body, latent_only),
        grid=(n // tm - skip,),
        in_specs=[pl.BlockSpec((tm, d), row),
                  pl.BlockSpec((1, 1, mod3.shape[2]), mod_map),
                  pl.BlockSpec((tm, ATT_Q), row), pl.BlockSpec((tm, CONV_DIM), row),
                  pl.BlockSpec((tm, M_WIDTH), row), pl.BlockSpec((tm, M_WIDTH), row),
                  pl.BlockSpec((tm, M_WIDTH), row), pl.BlockSpec((tm, 3 * d), row),
                  _resident((1, M_WIDTH)), _resident((1, d)), _resident((1, d))]
                 + [_layer_resident(a, l) for a in weights],
        out_specs=pl.BlockSpec((tm, d), lambda i: (i, 0)),
        out_shape=jax.ShapeDtypeStruct((n - skip * tm, d), F32),
        compiler_params=_cparams("parallel"),
        name="mix_ffn",
    )(x, mod3, att, cact, hf, hb, om, br, g_mnorm, g_ffn, g_final, *weights)


def _rope_table(geom):
    p = jnp.arange(geom.S)
    n_freq = HEAD_DIM // 4
    inv_freq = ROPE_BASE ** (-jnp.arange(n_freq, dtype=F32) / n_freq)
    ang_r = (p // GRID_W).astype(F32)[:, None] * inv_freq
    ang_c = (p % GRID_W).astype(F32)[:, None] * inv_freq
    ang = jnp.concatenate([ang_r, ang_r, ang_c, ang_c], axis=-1)
    reps = LANES // HEAD_DIM
    cos = jnp.tile(jnp.cos(ang), (1, reps))
    sin = jnp.tile(jnp.sin(ang), (1, reps))
    first_half = (jnp.arange(LANES) % (HEAD_DIM // 2)) < HEAD_DIM // 4
    lat = jnp.concatenate([cos, jnp.where(first_half, -sin, 0.0), jnp.where(first_half, 0.0, sin)], axis=1)
    ident = jnp.concatenate([jnp.ones((geom.Lc, LANES), F32), jnp.zeros((geom.Lc, 2 * LANES), F32)], axis=1)
    return jnp.concatenate([ident, lat], axis=0)


_TOK = {"qk": slice(0, ATT_Q + ATT_KV)}
_off = ATT_Q + 2 * ATT_KV
for _name, _size in (("glu", 2 * CONV_DIM), ("qkm", 2 * M_WIDTH), ("vm", M_WIDTH), ("om", M_WIDTH)):
    _TOK[_name] = slice(_off, _off + _size)
    _off += _size
_TOK_W = _off
assert all(sl.start % LANES == 0 for sl in _TOK.values()) and _TOK_W % LANES == 0


def _pack_weights(d, w_in, w_att_out, w_conv_pw, w_mlstm_out, w_out, w_ff_gate, w_ff_up, w_ff_down):
    return {
        "in_t": jnp.swapaxes(w_in, 1, 2).astype(BF16),
        "att_out": w_att_out.astype(BF16), "pw": w_conv_pw.astype(BF16), "mout": w_mlstm_out.astype(BF16),
        "out": w_out.astype(BF16), "ff_gate": w_ff_gate.astype(BF16), "ff_up": w_ff_up.astype(BF16),
        "ff_down": w_ff_down.astype(BF16),
    }


def kernel(x, c, ctx, c_ctx, w_ada, b_ada, g_norm_mix, g_norm_ffn, w_in, b_mgate, att_sink, w_att_out, w_conv_dw, b_conv_dw, g_conv_ln, b_conv_ln, w_conv_pw, w_mconv, g_mlstm_norm, w_mlstm_out, w_out, w_ff_gate, w_ff_up, w_ff_down, g_final):
    batch, seq, d = x.shape
    ctx_len = ctx.shape[1]
    depth = w_ada.shape[0]
    geom = _Geom(batch, seq, ctx_len)
    assert seq % ATT_STEP == 0 and ctx_len % ATT_STEP == 0 and seq % GRID_W == 0
    assert seq % M_CHUNK == 0 and ctx_len % M_CHUNK == 0 and d % LANES == 0
    assert LANES == 2 * HEAD_DIM and ATT_KV == LANES

    mod_rows = -(-(batch + 1) // SUBLANES) * SUBLANES
    cvec = jnp.zeros((mod_rows, d), F32).at[:batch].set(c).at[batch].set(c_ctx)
    mod = _modulation(cvec, w_ada, b_ada)
    rope = _rope_table(geom)
    xs = _assemble(geom, ctx.reshape(geom.n_ctx, d), x.reshape(batch * seq, d))
    g_fin = g_final.reshape(1, d)
    w = _pack_weights(d, w_in, w_att_out, w_conv_pw, w_mlstm_out, w_out, w_ff_gate, w_ff_up, w_ff_down)

    for l in range(depth):
        last = l == depth - 1
        mod3 = mod[l].reshape(mod_rows, 1, 6 * d)
        conv = (jnp.repeat(w_conv_dw[l], SUBLANES, axis=0), b_conv_dw[l].reshape(1, -1),
                g_conv_ln[l].reshape(1, -1), b_conv_ln[l].reshape(1, -1))
        q, k, vt, cact, qm, km, vmt, om, gates_t, br = _projection(
            geom, l, xs, mod3, g_norm_mix[l].reshape(1, d), rope, b_mgate[l], w_mconv[l], conv, w)
        att = _attention(geom, att_sink[l], q, k, vt)
        hf, hb = _mlstm_scan(geom, qm, km, vmt, gates_t)
        xs = _mix_ffn(geom, l, xs, mod3, att, cact, hf, hb, om, br, g_mlstm_norm[l].reshape(1, -1),
                      g_norm_ffn[l].reshape(1, d), g_fin, w, last)

    return xs.reshape(batch, seq, d)
```

```python
import functools

import jax
import jax.numpy as jnp
from jax import lax
from jax.experimental import pallas as pl
from jax.experimental.pallas import tpu as pltpu

F32 = jnp.float32
BF16 = jnp.bfloat16

GRID_W = 64
N_Q_HEADS = 8
N_KV_HEADS = 2
HEAD_DIM = 64
WINDOW = 128
ATT_BLOCK = 128
ROPE_BASE = 10000.0
ATT_Q = N_Q_HEADS * HEAD_DIM
ATT_KV = N_KV_HEADS * HEAD_DIM
CONV_DIM = 512
CONV_WIDTH = 31
M_HEADS = 4
M_HEAD_DIM = 128
M_WIDTH = M_HEADS * M_HEAD_DIM
M_SHORT_CONV = 3
N_GATE_COLS = 4 * M_HEADS
EPS = 1e-6
NEG_INF = -1e30
LOG2E = 1.4426950408889634

LANES = 128
SUBLANES = 8
MXU_TILE = 256
M_CHUNK = 256
M_AUG = 16
ATT_STEP = 256
CONV_HALO = 16
KV_W = N_KV_HEADS * LANES
VMEM_LIMIT = 52 * 1024 * 1024


def _cparams(*sem):
    return pltpu.CompilerParams(dimension_semantics=sem, vmem_limit_bytes=VMEM_LIMIT)


def _resident(shape):
    nd = len(shape)
    return pl.BlockSpec(shape, lambda *_: (0,) * nd, pipeline_mode=pl.Buffered(1))


def _layer_resident(arr, l):
    nd = arr.ndim
    return pl.BlockSpec((None,) + arr.shape[1:], lambda *_: (l,) + (0,) * (nd - 1), pipeline_mode=pl.Buffered(1))


def _sigmoid(x):
    return 1.0 / (1.0 + jnp.exp(-x))


def _silu(x):
    return x * _sigmoid(x)


def _log_sigmoid(x):
    return jnp.minimum(x, 0.0) - jnp.log(1.0 + jnp.exp(-jnp.abs(x)))


def _rms(x):
    return x * lax.rsqrt(jnp.mean(x * x, axis=-1, keepdims=True) + EPS)


def _norm_mod(x, g, sc, sh):
    return (_rms(x) * g) * (1.0 + sc) + sh


_NT = (((1,), (1,)), ((), ()))


def _ntdot(a, b):
    return lax.dot_general(a, b, _NT, preferred_element_type=F32)


def _dot(a, b):
    return jnp.dot(a, b, preferred_element_type=F32)


def _mod_body(c_ref, w_ref, b_ref, o_ref):
    h = _silu(c_ref[...]).astype(BF16)
    o_ref[0] = _dot(h, w_ref[0].astype(BF16)) + b_ref[0]


def _modulation(cvec, w_ada, b_ada):
    depth, d, n6 = w_ada.shape
    rows = cvec.shape[0]
    tn = n6 // 4
    return pl.pallas_call(
        _mod_body,
        grid=(depth, n6 // tn),
        in_specs=[pl.BlockSpec((rows, d), lambda l, j: (0, 0)),
                  pl.BlockSpec((1, d, tn), lambda l, j: (l, 0, j)),
                  pl.BlockSpec((1, 1, tn), lambda l, j: (l, 0, j))],
        out_specs=pl.BlockSpec((1, rows, tn), lambda l, j: (l, 0, j)),
        out_shape=jax.ShapeDtypeStruct((depth, rows, n6), F32),
        compiler_params=_cparams("arbitrary", "arbitrary"),
        name="ada_mod",
    )(cvec, w_ada, b_ada.reshape(depth, 1, n6))


class _Geom:
    def __init__(self, batch, seq, ctx_len):
        self.B, self.S, self.Lc = batch, seq, ctx_len
        self.n_ctx = batch * ctx_len
        self.N = self.n_ctx + batch * seq

    def mod_row(self, tile, tm):
        n_ctx_tiles = self.n_ctx // tm
        return jnp.where(tile < n_ctx_tiles, self.B, (tile - n_ctx_tiles) // (self.S // tm))

    def seq_tile(self, tile, tm):
        n_ctx_tiles = self.n_ctx // tm
        tc, ts = self.Lc // tm, self.S // tm
        return jnp.where(tile < n_ctx_tiles, tile % tc, tc + (tile - n_ctx_tiles) % ts)

    def seq_edges(self, tile, tm):
        n_ctx_tiles = self.n_ctx // tm
        tc, ts = self.Lc // tm, self.S // tm
        pos = jnp.where(tile < n_ctx_tiles, tile % tc, (tile - n_ctx_tiles) % ts)
        last = jnp.where(tile < n_ctx_tiles, tc - 1, ts - 1)
        return pos == 0, pos == last


def _row_tile(geom, want, within_sequence=False):
    tm = want
    while geom.n_ctx % tm or geom.S % tm or (within_sequence and geom.Lc % tm):
        tm //= 2
    return tm


def _halo_specs(geom, tm, halo, width):
    per = tm // halo
    last = geom.N // halo - 1
    return [pl.BlockSpec((halo, width), lambda i: (jnp.maximum(i * per - 1, 0), 0)),
            pl.BlockSpec((tm, width), lambda i: (i, 0)),
            pl.BlockSpec((halo, width), lambda i: (jnp.minimum((i + 1) * per, last), 0))]


def _assemble_body(n_ctx_tiles, ctx_ref, x_ref, o_ref):
    o_ref[...] = jnp.where(pl.program_id(0) < n_ctx_tiles, ctx_ref[...], x_ref[...])


def _assemble(geom, ctx2, x2):
    d = x2.shape[1]
    tm = _row_tile(geom, 1024)
    nct = geom.n_ctx // tm
    return pl.pallas_call(
        functools.partial(_assemble_body, nct),
        grid=(geom.N // tm,),
        in_specs=[pl.BlockSpec((tm, d), lambda i: (jnp.minimum(i, nct - 1), 0)),
                  pl.BlockSpec((tm, d), lambda i: (jnp.maximum(i - nct, 0), 0))],
        out_specs=pl.BlockSpec((tm, d), lambda i: (i, 0)),
        out_shape=jax.ShapeDtypeStruct((geom.N, d), F32),
        compiler_params=_cparams("arbitrary"),
        name="assemble_tokens",
    )(ctx2, x2)


def _rope(x, cos, sin_a, sin_b):
    return x * cos + pltpu.roll(x, LANES - 16, 1) * sin_a + pltpu.roll(x, 16, 1) * sin_b


def _proj_body(geom, tm, xp_ref, x_ref, xn_ref, mod_ref, g_ref, rope_ref, bgt_ref, wc_ref,
               wdw_ref, bdw_ref, gln_ref, bln_ref,
               wt_ref,
               q_ref, k_ref, vt_ref, cact_ref, qm_ref, km_ref, vmt_ref, om_ref, gatet_ref, br_ref,
               pad_ref, sh_ref):
    d = x_ref.shape[1]
    mod = mod_ref[0]
    g, sc, sh = g_ref[...], mod[:, d:2 * d], mod[:, 0:d]
    h = _norm_mod(x_ref[...], g, sc, sh).astype(BF16)
    cos, sa, sb = rope_ref[:, 0:LANES], rope_ref[:, LANES:2 * LANES], rope_ref[:, 2 * LANES:3 * LANES]

    halo = CONV_HALO
    first, last = geom.seq_edges(pl.program_id(0), tm)
    h_halo = _norm_mod(jnp.concatenate([xp_ref[...], xn_ref[...]], axis=0), g, sc, sh).astype(BF16)
    h_ext = jnp.concatenate([h, h_halo], axis=0)

    glu = _ntdot(h_ext, wt_ref[_TOK["glu"], :])
    u = glu[:, :CONV_DIM] * _sigmoid(glu[:, CONV_DIM:])
    sh_ref[0, 0:halo, :] = jnp.where(first, 0.0, u[tm:tm + halo])
    sh_ref[0, halo:halo + tm, :] = u[0:tm]
    sh_ref[0, halo + tm:2 * halo + tm, :] = jnp.where(last, 0.0, u[tm + halo:tm + 2 * halo])
    span = tm + 2 * halo - SUBLANES
    for s_ in range(1, SUBLANES):
        sh_ref[s_, 0:span, :] = sh_ref[0, s_:s_ + span, :]
    rows = 32

    def conv_rows(r0, zero):
        accs = [zero, None]
        for j in range(CONV_WIDTH):
            a_, s_ = divmod(halo + j - CONV_WIDTH // 2, SUBLANES)
            off = r0 + a_ * SUBLANES
            win = sh_ref[s_, off:off + rows, :].reshape(rows // SUBLANES, SUBLANES, -1)
            t = win * wdw_ref[j * SUBLANES:(j + 1) * SUBLANES, :][None]
            accs[j % 2] = t if accs[j % 2] is None else accs[j % 2] + t
        acc = accs[0] + accs[1]
        yv = acc.reshape(rows, -1) + bdw_ref[...]
        mu = jnp.mean(yv, axis=-1, keepdims=True)
        yc = yv - mu
        var = jnp.mean(yc * yc, axis=-1, keepdims=True)
        z = yc * lax.rsqrt(var + EPS) * gln_ref[...] + bln_ref[...]
        cact_ref[r0:r0 + rows, :] = _silu(z).astype(BF16)

    def late_zero(part):
        bits = pltpu.bitcast(part[0:rows, 0:CONV_DIM], jnp.uint32)
        return pltpu.bitcast((bits >> 16) >> 16, F32).reshape(rows // SUBLANES, SUBLANES, -1)

    n_groups = tm // rows
    br_cols = 2 * MXU_TILE
    br0 = _TOK_W + N_GATE_COLS
    n_br = (wt_ref.shape[0] - br0) // br_cols
    assert n_br + 2 == n_groups
    qkm = _ntdot(h_ext, wt_ref[_TOK["qkm"], :])
    zero = None
    qkv = om = None
    for gi in range(n_groups):
        conv_rows(gi * rows, zero)
        if gi < n_br:
            cs = slice(gi * br_cols, (gi + 1) * br_cols)
            part = _ntdot(h, wt_ref[br0 + gi * br_cols:br0 + (gi + 1) * br_cols, :])
            br_ref[:, cs] = part.astype(BF16)
        elif gi == n_br:
            part = qkv = _ntdot(h, wt_ref[_TOK["qk"], :])
        else:
            part = om = _ntdot(h, wt_ref[_TOK["om"], :])
        zero = late_zero(part)
    om_ref[...] = om.astype(BF16)

    pad_ref[0:halo, :] = jnp.where(first, 0.0, qkm[tm:tm + halo])
    pad_ref[halo:halo + tm, :] = qkm[0:tm]
    pad_ref[halo + tm:2 * halo + tm, :] = jnp.where(last, 0.0, qkm[tm + halo:tm + 2 * halo])
    v0 = ATT_Q + ATT_KV
    vmt = _ntdot(wt_ref[_TOK["vm"], :], h)
    vtt = _ntdot(wt_ref[v0:v0 + ATT_KV, :], h)
    gtt = _ntdot(wt_ref[_TOK_W:_TOK_W + N_GATE_COLS, :], h)
    acc = None
    for j in range(M_SHORT_CONV):
        off = halo + j - M_SHORT_CONV // 2
        t = pad_ref[off:off + tm, :] * wc_ref[j:j + 1, :]
        acc = t if acc is None else acc + t
    y = _silu(acc)
    qm_ref[...] = y[:, :M_WIDTH].astype(BF16)
    km_ref[...] = (y[:, M_WIDTH:] * M_HEAD_DIM ** -0.5).astype(BF16)

    scale = HEAD_DIM ** -0.5 * LOG2E
    for j in range(ATT_Q // LANES):
        sl = slice(j * LANES, (j + 1) * LANES)
        q_ref[:, sl] = _rope(qkv[:, sl] * scale, cos, sa, sb).astype(BF16)
    lo = lax.broadcasted_iota(jnp.int32, (tm, LANES), 1) < HEAD_DIM
    keys = _rope(qkv[:, ATT_Q:ATT_Q + ATT_KV], cos, sa, sb)
    swapped = pltpu.roll(keys, HEAD_DIM, 1)
    k_ref[:, 0:LANES] = jnp.where(lo, keys, swapped).astype(BF16)
    k_ref[:, LANES:2 * LANES] = jnp.where(lo, swapped, keys).astype(BF16)
    vmt_ref[...] = vmt.astype(BF16)
    vt_ref[...] = vtt.astype(BF16)
    gatet_ref[...] = gtt + bgt_ref[...]


def _projection(geom, l, x, mod3, g, rope, b_gate, w_mconv, conv, w):
    n, d = x.shape
    tm = _row_tile(geom, 256, within_sequence=True)
    row = lambda i: (i, 0)
    col = lambda i: (0, i)
    mod_map = lambda i: (geom.mod_row(i, tm), 0, 0)
    rope_map = lambda i: (geom.seq_tile(i, tm), 0)
    outs = ((ATT_Q, BF16, False), (KV_W, BF16, False), (ATT_KV, BF16, True), (CONV_DIM, BF16, False),
            (M_WIDTH, BF16, False), (M_WIDTH, BF16, False), (M_WIDTH, BF16, True), (M_WIDTH, BF16, False),
            (N_GATE_COLS, F32, True), (3 * d, BF16, False))
    return pl.pallas_call(
        functools.partial(_proj_body, geom, tm),
        grid=(n // tm,),
        in_specs=_halo_specs(geom, tm, CONV_HALO, d)
                 + [pl.BlockSpec((1, 1, mod3.shape[2]), mod_map),
                    _resident((1, d)),
                    pl.BlockSpec((tm, rope.shape[1]), rope_map),
                    _resident((N_GATE_COLS, 1)), _resident(w_mconv.shape)]
                 + [_resident(a.shape) for a in conv]
                 + [_layer_resident(w["in_t"], l)],
        out_specs=[pl.BlockSpec((c, tm), col) if fm else pl.BlockSpec((tm, c), row) for c, _, fm in outs],
        out_shape=[jax.ShapeDtypeStruct((c, n) if fm else (n, c), t) for c, t, fm in outs],
        scratch_shapes=[pltpu.VMEM((tm + 2 * CONV_HALO, 2 * M_WIDTH), F32),
                        pltpu.VMEM((SUBLANES, tm + 2 * CONV_HALO, CONV_DIM), F32)],
        compiler_params=_cparams("parallel"),
        name="in_proj",
    )(x, x, x, mod3, g, rope, b_gate.reshape(N_GATE_COLS, 1), w_mconv, *conv, w["in_t"])


def _att_body(geom, sink_ref, q_ref, kp_ref, kc_ref, kn_ref, kx_ref, vp_ref, vc_ref, vn_ref, vx_ref, o_ref):
    blk = ATT_BLOCK
    nsub = ATT_STEP // blk
    step = pl.program_id(1) - geom.Lc // ATT_STEP
    group = N_Q_HEADS // N_KV_HEADS
    kk = lax.broadcasted_iota(jnp.int32, (3 * blk, blk), 0)
    qq = lax.broadcasted_iota(jnp.int32, (3 * blk, blk), 1)
    lo = lax.broadcasted_iota(jnp.int32, (blk, LANES), 1) < HEAD_DIM
    zero = jnp.zeros((blk, LANES), BF16)
    k_loc = jnp.concatenate([kp_ref[...], kc_ref[...], kn_ref[...]], axis=0)
    vt_loc = jnp.concatenate([vp_ref[...], vc_ref[...], vn_ref[...]], axis=1)
    k_ctx, vt_ctx = kx_ref[...], vx_ref[...]
    band = jnp.abs(kk - blk - qq) <= WINDOW
    chains = []
    for sub in range(nsub):
        n = step * nsub + sub
        lo_row = jnp.where(n < 0, 3 * blk, jnp.maximum((1 - n) * blk, 0))
        hi_row = geom.S - (n - 1) * blk
        bias = jnp.where(band & (kk >= lo_row) & (kk < hi_row), 0.0, NEG_INF)
        bias = jnp.concatenate([bias] * group, axis=1)
        rows = slice(sub * blk, (sub + 1) * blk)
        win = slice(sub * blk, (sub + 3) * blk)
        for hk in range(N_KV_HEADS):
            ks = slice(hk * LANES, (hk + 1) * LANES)
            tiles = [q_ref[rows, (hk * group + 2 * p) * HEAD_DIM:(hk * group + 2 * p + 2) * HEAD_DIM]
                     for p in range(group // 2)]
            qs = jnp.concatenate([jnp.where(lo if half == 0 else ~lo, t, zero)
                                  for t in tiles for half in (0, 1)], axis=0)
            s = jnp.concatenate(
                [lax.dot_general(k_loc[win, ks], qs, _NT, preferred_element_type=F32) + bias,
                 lax.dot_general(k_ctx[:, ks], qs, _NT, preferred_element_type=F32)], axis=0)
            chains.append((rows, win, hk, ks, s))

    stage2 = []
    for rows, win, hk, ks, s in chains:
        sk = jnp.concatenate(
            [jnp.full((1, blk), sink_ref[hk * group + j] * LOG2E, F32) for j in range(group)], axis=1)
        m = jnp.maximum(jnp.max(s, axis=0, keepdims=True), sk)
        e = jnp.exp2(s - m)
        den = jnp.sum(e, axis=0, keepdims=True) + jnp.exp2(sk - m)
        stage2.append((rows, win, hk, ks, e.astype(BF16), den))

    for rows, win, hk, ks, e, den in stage2:
        vs = slice(hk * HEAD_DIM, (hk + 1) * HEAD_DIM)
        vt = jnp.concatenate([vt_loc[vs, win], vt_ctx[vs, :]], axis=1)
        vt = jnp.concatenate([vt, vt], axis=0)
        o = (_dot(vt, e) * (1.0 / den)).T
        for p in range(group // 2):
            pair = jnp.where(lo, o[2 * p * blk:(2 * p + 1) * blk], o[(2 * p + 1) * blk:(2 * p + 2) * blk])
            c0 = (hk * group + 2 * p) * HEAD_DIM
            o_ref[rows, c0:c0 + LANES] = pair.astype(BF16)


def _attention(geom, sink, q, k, vt):
    blk, stp = ATT_BLOCK, ATT_STEP
    nsub = stp // blk
    ncs, nss = geom.Lc // stp, geom.S // stp
    nsb = geom.S // blk
    base_s, base_b = geom.n_ctx // stp, geom.n_ctx // blk

    def q_blk(b, i):
        return jnp.where(i < ncs, b * ncs + i, base_s + b * nss + i - ncs)

    def edge_blk(off):
        return lambda b, i: base_b + b * nsb + jnp.clip((i - ncs) * nsub + off, 0, nsb - 1)

    cur_blk = lambda b, i: base_s + b * nss + jnp.clip(i - ncs, 0, nss - 1)
    ctx_blk = lambda b, i: b
    blocks = ((blk, edge_blk(-1)), (stp, cur_blk), (blk, edge_blk(nsub)), (geom.Lc, ctx_blk))
    rows = lambda f: (lambda b, i: (f(b, i), 0))
    cols = lambda f: (lambda b, i: (0, f(b, i)))
    return pl.pallas_call(
        functools.partial(_att_body, geom),
        grid=(geom.B, ncs + nss),
        in_specs=[pl.BlockSpec(memory_space=pltpu.SMEM),
                  pl.BlockSpec((stp, ATT_Q), rows(q_blk))]
                 + [pl.BlockSpec((size, KV_W), rows(f)) for size, f in blocks]
                 + [pl.BlockSpec((ATT_KV, size), cols(f)) for size, f in blocks],
        out_specs=pl.BlockSpec((stp, ATT_Q), rows(q_blk)),
        out_shape=jax.ShapeDtypeStruct(q.shape, BF16),
        compiler_params=_cparams("parallel", "parallel"),
        name="window_attention",
    )(sink, q, k, k, k, k, vt, vt, vt, vt)


def _split3(x):
    hi = x.astype(BF16)
    r1 = x - hi.astype(F32)
    mid = r1.astype(BF16)
    lo = (r1 - mid.astype(F32)).astype(BF16)
    return hi, mid, lo


def _mlstm_body(qf, kf, vf, gtf, qb, kb, vb, gtb, of, ob, st_ref, m_ref):
    @pl.when(pl.program_id(1) == 0)
    def _():
        st_ref[...] = jnp.zeros_like(st_ref)
        m_ref[...] = jnp.full_like(m_ref, NEG_INF)

    L = M_CHUNK
    r = lax.broadcasted_iota(jnp.int32, (L, L), 0)
    c = lax.broadcasted_iota(jnp.int32, (L, L), 1)
    ones = jnp.ones((M_AUG, L), BF16)
    refs = ((qf, kf, vf, gtf, of), (qb, kb, vb, gtb, ob))
    prep = []
    for direction in range(2):
        keep = (r <= c) if direction == 0 else (r >= c)
        tri = jnp.where(keep, 1.0, 0.0).astype(BF16)
        gates_t = refs[direction][3][...]
        hi, mid, lo = _split3(_log_sigmoid(gates_t))
        brow_all = _dot(hi, tri) + _dot(mid, tri) + _dot(lo, tri)
        u0 = direction * M_HEADS
        c_rows = gates_t[u0:u0 + M_HEADS] - brow_all[2 * M_HEADS + u0:2 * M_HEADS + u0 + M_HEADS]
        c_cols = jnp.concatenate([c_rows, jnp.zeros((LANES - M_HEADS, L), F32)], axis=0).T
        prep.append((keep, gates_t, brow_all, c_cols))

    units = []
    for direction in range(2):
        keep, gates_t, brow_all, c_cols = prep[direction]
        q_ref, k_ref = refs[direction][0], refs[direction][1]
        for h in range(M_HEADS):
            u = direction * M_HEADS + h
            hs = slice(h * M_HEAD_DIM, (h + 1) * M_HEAD_DIM)
            q, k = q_ref[:, hs], k_ref[:, hs]
            b_row = brow_all[2 * M_HEADS + u:2 * M_HEADS + u + 1, :]
            i_row = gates_t[u:u + 1, :]
            m_prev = m_ref[u:u + 1, 0:1]
            state = st_ref[u]
            dmat = jnp.where(keep, b_row + c_cols[:, h:h + 1], NEG_INF)
            a_row = b_row + m_prev
            mt = jnp.maximum(a_row, jnp.max(dmat, axis=0, keepdims=True))
            big = lax.dot_general(jnp.concatenate([k, state.astype(BF16)], axis=0), q, _NT,
                                  preferred_element_type=F32)
            units.append((direction, u, hs, k, b_row, i_row, m_prev, state, dmat, a_row, mt, big))

    units2 = []
    for (direction, u, hs, k, b_row, i_row, m_prev, state, dmat, a_row, mt, big) in units:
        vt_aug = jnp.concatenate([refs[direction][2][hs, :], ones], axis=0)
        w_row = jnp.exp(a_row - mt)
        smat = (big[:L] * jnp.exp(dmat - mt)).astype(BF16)
        tot = w_row * big[L:] + _dot(vt_aug, smat)
        den = tot[M_HEAD_DIM:M_HEAD_DIM + 1, :]
        scale = 1.0 / jnp.maximum(jnp.abs(den), jnp.exp(-mt))
        refs[direction][4][:, hs] = (tot[:M_HEAD_DIM] * scale).T
        units2.append((direction, u, k, b_row, i_row, m_prev, state, vt_aug))

    for (direction, u, k, b_row, i_row, m_prev, state, vt_aug) in units2:
        end = L - 1 if direction == 0 else 0
        total = b_row[:, end:end + 1]
        g_row = total - b_row + i_row
        m_new = jnp.maximum(total + m_prev, jnp.max(g_row, axis=1, keepdims=True))
        decay = jnp.exp(total + m_prev - m_new)
        vw = (vt_aug.astype(F32) * jnp.exp(g_row - m_new)).astype(BF16)
        st_ref[u] = decay * state + _dot(vw, k)
        m_ref[u:u + 1, :] = jnp.broadcast_to(m_new, (1, LANES))


def _mlstm_scan(geom, q, k, vt, gates_t):
    L = M_CHUNK
    ncc, nsc = geom.Lc // L, geom.S // L
    base = geom.n_ctx // L

    def fwd(b, j):
        return jnp.where(j < ncc, b * ncc + j, base + b * nsc + j - ncc)

    def bwd(b, j):
        return jnp.where(j < ncc, b * ncc + ncc - 1 - j, base + b * nsc + nsc - 1 - (j - ncc))

    def specs(m):
        rows = lambda b, j: (m(b, j), 0)
        cols = lambda b, j: (0, m(b, j))
        return [pl.BlockSpec((L, M_WIDTH), rows), pl.BlockSpec((L, M_WIDTH), rows),
                pl.BlockSpec((M_WIDTH, L), cols), pl.BlockSpec((N_GATE_COLS, L), cols)]

    n = q.shape[0]
    return pl.pallas_call(
        _mlstm_body,
        grid=(geom.B, ncc + nsc),
        in_specs=specs(fwd) + specs(bwd),
        out_specs=[pl.BlockSpec((L, M_WIDTH), lambda b, j: (fwd(b, j), 0)),
                   pl.BlockSpec((L, M_WIDTH), lambda b, j: (bwd(b, j), 0))],
        out_shape=[jax.ShapeDtypeStruct((n, M_WIDTH), F32)] * 2,
        scratch_shapes=[pltpu.VMEM((2 * M_HEADS, M_HEAD_DIM + M_AUG, M_HEAD_DIM), F32),
                        pltpu.VMEM((2 * M_HEADS, LANES), F32)],
        compiler_params=_cparams("parallel", "arbitrary"),
        name="mlstm_scan",
    )(q, k, vt, gates_t, q, k, vt, gates_t)


def _mix_ffn_body(final, x_ref, mod_ref, att_ref, cact_ref, hf_ref, hb_ref, om_ref, br_ref, gm_ref,
                  g_ref, gf_ref, wa_ref, wp_ref, wm_ref, wo_ref, wg_ref, wu_ref, wd_ref, o_ref):
    d = x_ref.shape[1]
    mod = mod_ref[0]
    hm = _sigmoid(om_ref[...].astype(F32)) * (hf_ref[...] + hb_ref[...])
    parts = []
    for h in range(M_HEADS):
        t = hm[:, h * M_HEAD_DIM:(h + 1) * M_HEAD_DIM]
        mu = jnp.mean(t, axis=-1, keepdims=True)
        tc = t - mu
        var = jnp.mean(tc * tc, axis=-1, keepdims=True)
        parts.append(tc * lax.rsqrt(var + EPS))
    hn = (jnp.concatenate(parts, axis=-1) * gm_ref[...]).astype(BF16)
    ya = _dot(att_ref[...], wa_ref[...])
    yb = _dot(cact_ref[...], wp_ref[...])
    yc = _dot(hn, wm_ref[...])
    merged = (_sigmoid(br_ref[:, 0:d].astype(F32)) * ya + _sigmoid(br_ref[:, d:2 * d].astype(F32)) * yb
              + _sigmoid(br_ref[:, 2 * d:3 * d].astype(F32)) * yc)
    x = x_ref[...] + mod[:, 2 * d:3 * d] * _dot(merged.astype(BF16), wo_ref[...])
    h = _norm_mod(x, g_ref[...], mod[:, 4 * d:5 * d], mod[:, 3 * d:4 * d]).astype(BF16)
    ff = wg_ref.shape[1]
    half = -(-ff // (2 * MXU_TILE)) * MXU_TILE
    acc = None
    for sl in (slice(0, half), slice(half, ff)):
        act = (_silu(_dot(h, wg_ref[:, sl])) * _dot(h, wu_ref[:, sl])).astype(BF16)
        t = _dot(act, wd_ref[sl, :])
        acc = t if acc is None else acc + t
    y = x + mod[:, 5 * d:6 * d] * acc
    o_ref[...] = _rms(y) * gf_ref[...] if final else y


def _mix_ffn(geom, l, x, mod3, att, cact, hf, hb, om, br, g_mnorm, g_ffn, g_final, w, latent_only):
    n, d = x.shape
    tm = _row_tile(geom, 512)
    skip = geom.n_ctx // tm if latent_only else 0
    row = lambda i: (i + skip, 0)
    mod_map = lambda i: (geom.mod_row(i + skip, tm), 0, 0)
    weights = (w["att_out"], w["pw"], w["mout"], w["out"], w["ff_gate"], w["ff_up"], w["ff_down"])
    return pl.pallas_call(
        functools.partial(_mix_ffn_body, latent_only),
        grid=(n // tm - skip,),
        in_specs=[pl.BlockSpec((tm, d), row),
                  pl.BlockSpec((1, 1, mod3.shape[2]), mod_map),
                  pl.BlockSpec((tm, ATT_Q), row), pl.BlockSpec((tm, CONV_DIM), row),
                  pl.BlockSpec((tm, M_WIDTH), row), pl.BlockSpec((tm, M_WIDTH), row),
                  pl.BlockSpec((tm, M_WIDTH), row), pl.BlockSpec((tm, 3 * d), row),
                  _resident((1, M_WIDTH)), _resident((1, d)), _resident((1, d))]
                 + [_layer_resident(a, l) for a in weights],
        out_specs=pl.BlockSpec((tm, d), lambda i: (i, 0)),
        out_shape=jax.ShapeDtypeStruct((n - skip * tm, d), F32),
        compiler_params=_cparams("parallel"),
        name="mix_ffn",
    )(x, mod3, att, cact, hf, hb, om, br, g_mnorm, g_ffn, g_final, *weights)


def _rope_table(geom):
    p = jnp.arange(geom.S)
    n_freq = HEAD_DIM // 4
    inv_freq = ROPE_BASE ** (-jnp.arange(n_freq, dtype=F32) / n_freq)
    ang_r = (p // GRID_W).astype(F32)[:, None] * inv_freq
    ang_c = (p % GRID_W).astype(F32)[:, None] * inv_freq
    ang = jnp.concatenate([ang_r, ang_r, ang_c, ang_c], axis=-1)
    reps = LANES // HEAD_DIM
    cos = jnp.tile(jnp.cos(ang), (1, reps))
    sin = jnp.tile(jnp.sin(ang), (1, reps))
    first_half = (jnp.arange(LANES) % (HEAD_DIM // 2)) < HEAD_DIM // 4
    lat = jnp.concatenate([cos, jnp.where(first_half, -sin, 0.0), jnp.where(first_half, 0.0, sin)], axis=1)
    ident = jnp.concatenate([jnp.ones((geom.Lc, LANES), F32), jnp.zeros((geom.Lc, 2 * LANES), F32)], axis=1)
    return jnp.concatenate([ident, lat], axis=0)


_TOK = {"qk": slice(0, ATT_Q + ATT_KV)}
_off = ATT_Q + 2 * ATT_KV
for _name, _size in (("glu", 2 * CONV_DIM), ("qkm", 2 * M_WIDTH), ("vm", M_WIDTH), ("om", M_WIDTH)):
    _TOK[_name] = slice(_off, _off + _size)
    _off += _size
_TOK_W = _off
assert all(sl.start % LANES == 0 for sl in _TOK.values()) and _TOK_W % LANES == 0


def _pack_weights(d, w_in, w_att_out, w_conv_pw, w_mlstm_out, w_out, w_ff_gate, w_ff_up, w_ff_down):
    return {
        "in_t": jnp.swapaxes(w_in, 1, 2).astype(BF16),
        "att_out": w_att_out.astype(BF16), "pw": w_conv_pw.astype(BF16), "mout": w_mlstm_out.astype(BF16),
        "out": w_out.astype(BF16), "ff_gate": w_ff_gate.astype(BF16), "ff_up": w_ff_up.astype(BF16),
        "ff_down": w_ff_down.astype(BF16),
    }


def kernel(x, c, ctx, c_ctx, w_ada, b_ada, g_norm_mix, g_norm_ffn, w_in, b_mgate, att_sink, w_att_out, w_conv_dw, b_conv_dw, g_conv_ln, b_conv_ln, w_conv_pw, w_mconv, g_mlstm_norm, w_mlstm_out, w_out, w_ff_gate, w_ff_up, w_ff_down, g_final):
    batch, seq, d = x.shape
    ctx_len = ctx.shape[1]
    depth = w_ada.shape[0]
    geom = _Geom(batch, seq, ctx_len)
    assert seq % ATT_STEP == 0 and ctx_len % ATT_STEP == 0 and seq % GRID_W == 0
    assert seq % M_CHUNK == 0 and ctx_len % M_CHUNK == 0 and d % LANES == 0
    assert LANES == 2 * HEAD_DIM and ATT_KV == LANES

    mod_rows = -(-(batch + 1) // SUBLANES) * SUBLANES
    cvec = jnp.zeros((mod_rows, d), F32).at[:batch].set(c).at[batch].set(c_ctx)
    mod = _modulation(cvec, w_ada, b_ada)
    rope = _rope_table(geom)
    xs = _assemble(geom, ctx.reshape(geom.n_ctx, d), x.reshape(batch * seq, d))
    g_fin = g_final.reshape(1, d)
    w = _pack_weights(d, w_in, w_att_out, w_conv_pw, w_mlstm_out, w_out, w_ff_gate, w_ff_up, w_ff_down)

    for l in range(depth):
        last = l == depth - 1
        mod3 = mod[l].reshape(mod_rows, 1, 6 * d)
        conv = (jnp.repeat(w_conv_dw[l], SUBLANES, axis=0), b_conv_dw[l].reshape(1, -1),
                g_conv_ln[l].reshape(1, -1), b_conv_ln[l].reshape(1, -1))
        q, k, vt, cact, qm, km, vmt, om, gates_t, br = _projection(
            geom, l, xs, mod3, g_norm_mix[l].reshape(1, d), rope, b_mgate[l], w_mconv[l], conv, w)
        att = _attention(geom, att_sink[l], q, k, vt)
        hf, hb = _mlstm_scan(geom, qm, km, vmt, gates_t)
        xs = _mix_ffn(geom, l, xs, mod3, att, cact, hf, hb, om, br, g_mlstm_norm[l].reshape(1, -1),
                      g_norm_ffn[l].reshape(1, d), g_fin, w, last)

    return xs.reshape(batch, seq, d)
```
